```python
import numpy as np
import jax
import jax.numpy as jnp
from jax import lax

D_MODEL = 4096
BATCH = 4
SEQ = 2048
DEPTH = 2

HEAD_DIM = 128
N_MIX_HEADS = D_MODEL // HEAD_DIM
ROPE_THETA = 10000.0
NORM_EPS = 1e-6
Q_BLOCK = 128
GATHER_Q_BLOCK = 16
TINY = 1e-30

MOBA_HEADS = N_MIX_HEADS // 2
MOBA_BLOCK = 256
MOBA_TOPK = 3
NSA_HEADS = N_MIX_HEADS - MOBA_HEADS
NSA_KV_GROUPS = 4
NSA_CMP_LEN = 32
NSA_CMP_STRIDE = 16
NSA_SEL_BLOCK = 64
NSA_SEL_TOPN = 16
NSA_WINDOW = 512
DIL_CONFIGS = ((128, 1), (512, 4), (2048, 16))
DIL_GROUPS = len(DIL_CONFIGS)
DIL_HEADS = N_MIX_HEADS // 4
SB_HEADS = N_MIX_HEADS - DIL_HEADS
MEM_LEN = 256
XATTN_HEADS = 4
FFN_HIDDEN = -((-8 * D_MODEL) // (3 * 256)) * 256

EVEN_SPLITS = (MOBA_HEADS * HEAD_DIM,) * 3 + (NSA_HEADS * HEAD_DIM,) + (NSA_KV_GROUPS * HEAD_DIM,) * 6 + (3 * NSA_HEADS,)
ODD_SPLITS = (SB_HEADS * HEAD_DIM,) * 3 + (DIL_GROUPS * DIL_HEADS * HEAD_DIM,) * 3
EVEN_COLS = sum(EVEN_SPLITS)
ODD_COLS = sum(ODD_SPLITS)

kernel_name = 'hybrid_moba_nsa_stickbreak_dilated'


def rms_norm(x, g):
    xf = x.astype(jnp.float32)
    y = xf * lax.rsqrt(jnp.mean(xf * xf, axis=-1, keepdims=True) + NORM_EPS)
    return (y * g.astype(jnp.float32)).astype(x.dtype)


def rope(x, pos):
    half = x.shape[-1] // 2
    inv_freq = ROPE_THETA ** (-jnp.arange(half, dtype=jnp.float32) / half)
    ang = pos.astype(jnp.float32)[:, None, :, None] * inv_freq
    cos = jnp.cos(ang).astype(x.dtype)
    sin = jnp.sin(ang).astype(x.dtype)
    x1, x2 = x[..., :half], x[..., half:]
    return jnp.concatenate([x1 * cos - x2 * sin, x1 * sin + x2 * cos], axis=-1)


def split_heads(y, n):
    b, t, _ = y.shape
    return y.reshape(b, t, n, -1).transpose(0, 2, 1, 3)


def merge_heads(y):
    b, h, t, d = y.shape
    return y.transpose(0, 2, 1, 3).reshape(b, t, h * d)


def split_cols(y, sizes):
    cuts = [int(c) for c in np.cumsum(sizes)[:-1]]
    return jnp.split(y, cuts, axis=-1)


def masked_softmax_lse(s, mask):
    s = jnp.where(mask, s, -jnp.inf)
    m = jnp.max(s, axis=-1, keepdims=True)
    m = jnp.where(jnp.isfinite(m), m, 0.0)
    e = jnp.exp(s - m)
    den = jnp.maximum(jnp.sum(e, axis=-1, keepdims=True), TINY)
    return e / den, (m + jnp.log(den))[..., 0]


def masked_softmax(s, mask):
    return masked_softmax_lse(s, mask)[0]


def sweep_queries(fn, n_queries, qb):
    outs = lax.map(fn, jnp.arange(n_queries // qb, dtype=jnp.int32) * qb)

    def assemble(o):
        n, b, h, _, d = o.shape
        return o.transpose(1, 2, 0, 3, 4).reshape(b, h, n_queries, d)
    return jax.tree_util.tree_map(assemble, outs)


def gather_blocks(blocks, idx):
    return jax.vmap(jax.vmap(lambda bl, ix: bl[ix]))(blocks, idx)


def moba_attention(q, k, v):
    b, h, t, d = q.shape
    scale = d ** -0.5
    nblk = -(-t // MOBA_BLOCK)
    pad = ((0, 0), (0, 0), (0, nblk * MOBA_BLOCK - t), (0, 0))
    kp, vp = jnp.pad(k, pad), jnp.pad(v, pad)
    kb = kp.reshape(b, h, nblk, MOBA_BLOCK, d)
    vb = vp.reshape(b, h, nblk, MOBA_BLOCK, d)
    kmean = jnp.mean(kb.astype(jnp.float32), axis=3).astype(q.dtype)
    gate = jnp.einsum('bhtd,bhnd->bhtn', q, kmean, preferred_element_type=jnp.float32)
    cur = jnp.arange(t) // MOBA_BLOCK
    past = jnp.arange(nblk)[None, :] < cur[:, None]
    topk = min(MOBA_TOPK, nblk)
    _, sel = lax.top_k(jnp.where(past, gate, -jnp.inf), topk)
    sel_ok = sel < cur[:, None]

    def chunk(q0):
        qb = GATHER_Q_BLOCK
        qc = lax.dynamic_slice_in_dim(q, q0, qb, axis=2)
        selc = lax.dynamic_slice_in_dim(sel, q0, qb, axis=2)
        okc = lax.dynamic_slice_in_dim(sel_ok, q0, qb, axis=2)
        tq = q0 + jnp.arange(qb)
        own0 = (q0 // MOBA_BLOCK) * MOBA_BLOCK
        ko = lax.dynamic_slice_in_dim(kp, own0, MOBA_BLOCK, axis=2)
        vo = lax.dynamic_slice_in_dim(vp, own0, MOBA_BLOCK, axis=2)
        s_own = jnp.einsum('bhqd,bhkd->bhqk', qc, ko, preferred_element_type=jnp.float32) * scale
        m_own = (own0 + jnp.arange(MOBA_BLOCK))[None, :] <= tq[:, None]
        kg = gather_blocks(kb, selc)
        vg = gather_blocks(vb, selc)
        s_sel = jnp.einsum('bhqd,bhqnkd->bhqnk', qc, kg, preferred_element_type=jnp.float32)
        s_sel = s_sel.reshape(b, h, qb, topk * MOBA_BLOCK) * scale
        m_sel = jnp.repeat(okc, MOBA_BLOCK, axis=-1)
        s = jnp.concatenate([s_own, s_sel], axis=-1)
        m = jnp.concatenate([jnp.broadcast_to(m_own, (b, h, qb, MOBA_BLOCK)), m_sel], axis=-1)
        p = masked_softmax(s, m).astype(v.dtype)
        p_sel = p[..., MOBA_BLOCK:].reshape(b, h, qb, topk, MOBA_BLOCK)
        return (jnp.einsum('bhqk,bhkd->bhqd', p[..., :MOBA_BLOCK], vo)
                + jnp.einsum('bhqnk,bhqnkd->bhqd', p_sel, vg))
    return sweep_queries(chunk, t, GATHER_Q_BLOCK)


def nsa_attention(q, k_cmp_raw, v_cmp_raw, k_slc, v_slc, k_win, v_win, gate_logits, pos,
                  pe_k, pe_v, phi_k, phi_v, g_kc):
    b, hq, t, d = q.shape
    g = k_slc.shape[1]
    hg = hq // g
    scale = d ** -0.5
    qg = q.reshape(b, g, hg, t, d)
    t_idx = jnp.arange(t)
    n_cmp = (t - NSA_CMP_LEN) // NSA_CMP_STRIDE + 1
    starts = np.arange(n_cmp) * NSA_CMP_STRIDE
    ends = starts + NSA_CMP_LEN - 1
    win_idx = starts[:, None] + np.arange(NSA_CMP_LEN)[None, :]
    k_c = jnp.einsum('bgnld,lde->bgne', k_cmp_raw[:, :, win_idx] + pe_k, phi_k)
    v_c = jnp.einsum('bgnld,lde->bgne', v_cmp_raw[:, :, win_idx] + pe_v, phi_v)
    k_c = rope(rms_norm(k_c, g_kc), pos[:, ends])
    s_c = jnp.einsum('bghtd,bgnd->bghtn', qg, k_c, preferred_element_type=jnp.float32) * scale
    p_c = masked_softmax(s_c, ends[None, :] <= t_idx[:, None])
    o_c = jnp.einsum('bghtn,bgnd->bghtd', p_c.astype(v_c.dtype), v_c).reshape(b, hq, t, d)
    n_sel = t // NSA_SEL_BLOCK
    sel_start = np.arange(n_sel) * NSA_SEL_BLOCK
    overlap = ((starts[:, None] < sel_start[None, :] + NSA_SEL_BLOCK)
               & (starts[:, None] + NSA_CMP_LEN > sel_start[None, :])).astype(np.float32)
    importance = jnp.einsum('bghtn,nj->bgtj', p_c, jnp.asarray(overlap))
    cur = t_idx // NSA_SEL_BLOCK
    blk = jnp.arange(n_sel)
    forced = (blk[None, :] == cur[:, None]) | (blk[None, :] == 0)
    importance = jnp.where(forced, jnp.inf,
                           jnp.where(blk[None, :] <= cur[:, None], importance, -jnp.inf))
    topn = min(NSA_SEL_TOPN, n_sel)
    _, sel = lax.top_k(importance, topn)
    ks_blocks = k_slc.reshape(b, g, n_sel, NSA_SEL_BLOCK, d)
    vs_blocks = v_slc.reshape(b, g, n_sel, NSA_SEL_BLOCK, d)
    wpad = ((0, 0), (0, 0), (NSA_WINDOW, 0), (0, 0))
    kw_pad, vw_pad = jnp.pad(k_win, wpad), jnp.pad(v_win, wpad)

    def chunk(q0):
        qb = GATHER_Q_BLOCK
        qc = lax.dynamic_slice_in_dim(qg, q0, qb, axis=3)
        tq = q0 + jnp.arange(qb)
        selc = lax.dynamic_slice_in_dim(sel, q0, qb, axis=2)
        kg = gather_blocks(ks_blocks, selc)
        vg = gather_blocks(vs_blocks, selc)
        s_s = jnp.einsum('bghqd,bgqnkd->bghqnk', qc, kg, preferred_element_type=jnp.float32) * scale
        key_pos = selc[..., None] * NSA_SEL_BLOCK + jnp.arange(NSA_SEL_BLOCK)
        m_s = (key_pos <= tq[:, None, None]).reshape(b, g, 1, qb, topn * NSA_SEL_BLOCK)
        p_s = masked_softmax(s_s.reshape(b, g, hg, qb, topn * NSA_SEL_BLOCK), m_s)
        p_s = p_s.reshape(b, g, hg, qb, topn, NSA_SEL_BLOCK).astype(vg.dtype)
        o_s = jnp.einsum('bghqnk,bgqnkd->bghqd', p_s, vg)
        span = qb + NSA_WINDOW
        kwc = lax.dynamic_slice_in_dim(kw_pad, q0, span, axis=2)
        vwc = lax.dynamic_slice_in_dim(vw_pad, q0, span, axis=2)
        s_w = jnp.einsum('bghqd,bgkd->bghqk', qc, kwc, preferred_element_type=jnp.float32) * scale
        kpos = q0 - NSA_WINDOW + jnp.arange(span)
        m_w = ((kpos[None, :] <= tq[:, None]) & (kpos[None, :] > tq[:, None] - NSA_WINDOW)
               & (kpos[None, :] >= 0))
        p_w = masked_softmax(s_w, m_w).astype(vwc.dtype)
        o_w = jnp.einsum('bghqk,bgkd->bghqd', p_w, vwc)
        return o_s.reshape(b, hq, qb, d), o_w.reshape(b, hq, qb, d)
    o_s, o_w = sweep_queries(chunk, t, GATHER_Q_BLOCK)
    gates = jax.nn.sigmoid(gate_logits.astype(jnp.float32)).astype(q.dtype).transpose(0, 2, 1, 3)
    return gates[..., 0:1] * o_c + gates[..., 1:2] * o_s + gates[..., 2:3] * o_w


def stick_breaking_attention(q, k, v):
    b, h, t, d = q.shape
    scale = d ** -0.5
    spos = jnp.arange(t)

    def chunk(q0):
        qc = lax.dynamic_slice_in_dim(q, q0, Q_BLOCK, axis=2)
        z = jnp.einsum('bhqd,bhkd->bhqk', qc, k, preferred_element_type=jnp.float32) * scale
        tq = q0 + jnp.arange(Q_BLOCK)
        strict = spos[None, :] < tq[:, None]
        log_rest = jnp.where(strict, jax.nn.log_sigmoid(-z), 0.0)
        after = lax.cumsum(log_rest, axis=3, reverse=True) - log_rest
        a = jnp.where(strict, jnp.exp(jax.nn.log_sigmoid(z) + after), 0.0)
        return jnp.einsum('bhqk,bhkd->bhqd', a.astype(v.dtype), v)
    return sweep_queries(chunk, t, Q_BLOCK)


def dilated_attention(q, k, v):
    b, ng, hd, t, d = q.shape
    scale = d ** -0.5
    max_w = max(w for w, _ in DIL_CONFIGS)
    pad = ((0, 0), (0, 0), (0, 0), (max_w, 0), (0, 0))
    kp, vp = jnp.pad(k, pad), jnp.pad(v, pad)

    def chunk(q0):
        tq = q0 + jnp.arange(GATHER_Q_BLOCK)
        outs, lses = [], []
        for gi, (window, dil) in enumerate(DIL_CONFIGS):
            offs = np.arange(window // dil + 1) * dil
            src = tq[:, None] - offs[None, :]
            kg = kp[:, gi][:, :, src + max_w]
            vg = vp[:, gi][:, :, src + max_w]
            qc = lax.dynamic_slice_in_dim(q[:, gi], q0, GATHER_Q_BLOCK, axis=2)
            s = jnp.einsum('bhqd,bhqkd->bhqk', qc, kg, preferred_element_type=jnp.float32) * scale
            p, lse = masked_softmax_lse(s, src >= 0)
            outs.append(jnp.einsum('bhqk,bhqkd->bhqd', p.astype(vg.dtype), vg))
            lses.append(lse)
        weights = jax.nn.softmax(jnp.stack(lses), axis=0)[..., None]
        return jnp.sum(weights.astype(v.dtype) * jnp.stack(outs), axis=0)
    return sweep_queries(chunk, t, GATHER_Q_BLOCK)


def even_mixer(h, pos, w_in, w_out, moba_gq, moba_gk, nsa_gq, nsa_gk_cmp, nsa_gk_slc, nsa_gk_win,
               nsa_pe_k, nsa_pe_v, nsa_phi_k, nsa_phi_v):
    b, t, _ = h.shape
    mq, mk, mv, nq, kc, vc, ks, vs, kw, vw, gl = split_cols(h @ w_in, EVEN_SPLITS)
    mq = rope(rms_norm(split_heads(mq, MOBA_HEADS), moba_gq), pos)
    mk = rope(rms_norm(split_heads(mk, MOBA_HEADS), moba_gk), pos)
    o_a = moba_attention(mq, mk, split_heads(mv, MOBA_HEADS))
    nq = rope(rms_norm(split_heads(nq, NSA_HEADS), nsa_gq), pos)
    ks = rope(rms_norm(split_heads(ks, NSA_KV_GROUPS), nsa_gk_slc), pos)
    kw = rope(rms_norm(split_heads(kw, NSA_KV_GROUPS), nsa_gk_win), pos)
    o_b = nsa_attention(nq, split_heads(kc, NSA_KV_GROUPS), split_heads(vc, NSA_KV_GROUPS),
                        ks, split_heads(vs, NSA_KV_GROUPS), kw, split_heads(vw, NSA_KV_GROUPS),
                        gl.reshape(b, t, NSA_HEADS, 3), pos, nsa_pe_k, nsa_pe_v,
                        nsa_phi_k, nsa_phi_v, nsa_gk_cmp)
    return jnp.concatenate([merge_heads(o_a), merge_heads(o_b)], axis=-1) @ w_out


def odd_mixer(h, pos, w_in, w_out, dil_gq, dil_gk):
    b, t, _ = h.shape
    sq, sk, sv, dq, dk, dv = split_cols(h @ w_in, ODD_SPLITS)
    o_c = stick_breaking_attention(split_heads(sq, SB_HEADS), split_heads(sk, SB_HEADS),
                                   split_heads(sv, SB_HEADS))
    nd = DIL_GROUPS * DIL_HEADS
    shape = (b, DIL_GROUPS, DIL_HEADS, t, HEAD_DIM)
    dq = rope(rms_norm(split_heads(dq, nd), dil_gq), pos).reshape(shape)
    dk = rope(rms_norm(split_heads(dk, nd), dil_gk), pos).reshape(shape)
    o_d = dilated_attention(dq, dk, split_heads(dv, nd).reshape(shape))
    return jnp.concatenate([merge_heads(o_c), merge_heads(o_d)], axis=-1) @ w_out


def memory_cross_attention(h, mem_n, wq, wkv, wo, gq, gk):
    q = rms_norm(split_heads(h @ wq, XATTN_HEADS), gq)
    k, v = jnp.split(mem_n @ wkv, 2, axis=-1)
    k = rms_norm(split_heads(k, XATTN_HEADS), gk)
    v = split_heads(v, XATTN_HEADS)
    s = jnp.einsum('bhtd,bhmd->bhtm', q, k, preferred_element_type=jnp.float32) * HEAD_DIM ** -0.5
    p = jax.nn.softmax(s, axis=-1).astype(v.dtype)
    return merge_heads(jnp.einsum('bhtm,bhmd->bhtd', p, v)) @ wo


def swiglu(h, wg, wu, wd):
    return (jax.nn.silu(h @ wg) * (h @ wu)) @ wd


def setup_inputs(seed: int = 0) -> dict:
    key = jax.random.key(seed)
    keys = jax.random.split(key, 48)
    counter = [0]

    def nk():
        counter[0] += 1
        return keys[counter[0] - 1]

    def w(shape, fan_in):
        return jax.random.normal(nk(), shape, jnp.float32) * fan_in ** -0.5

    def gain(shape):
        return 1.0 + 0.02 * jax.random.normal(nk(), shape, jnp.float32)

    n_even = (DEPTH + 1) // 2
    n_odd = DEPTH // 2
    mix_w = N_MIX_HEADS * HEAD_DIM
    xw = XATTN_HEADS * HEAD_DIM
    x = jax.random.normal(nk(), (BATCH, SEQ, D_MODEL), jnp.float32)
    mem = jax.random.normal(nk(), (BATCH, MEM_LEN, D_MODEL), jnp.float32)
    offsets = jax.random.randint(nk(), (BATCH, 1), 0, 1024, dtype=jnp.int32)
    positions = offsets + jnp.arange(SEQ, dtype=jnp.int32)[None, :]
    return {
        'x': x,
        'mem': mem,
        'positions': positions,
        'mix_norm': gain((DEPTH, D_MODEL)),
        'even_w_in': w((n_even, D_MODEL, EVEN_COLS), D_MODEL),
        'even_w_out': w((n_even, mix_w, D_MODEL), mix_w),
        'moba_gq': gain((n_even, HEAD_DIM)),
        'moba_gk': gain((n_even, HEAD_DIM)),
        'nsa_gq': gain((n_even, HEAD_DIM)),
        'nsa_gk_cmp': gain((n_even, HEAD_DIM)),
        'nsa_gk_slc': gain((n_even, HEAD_DIM)),
        'nsa_gk_win': gain((n_even, HEAD_DIM)),
        'nsa_pe_k': 0.1 * jax.random.normal(nk(), (n_even, NSA_CMP_LEN, HEAD_DIM), jnp.float32),
        'nsa_pe_v': 0.1 * jax.random.normal(nk(), (n_even, NSA_CMP_LEN, HEAD_DIM), jnp.float32),
        'nsa_phi_k': w((n_even, NSA_CMP_LEN, HEAD_DIM, HEAD_DIM), NSA_CMP_LEN * HEAD_DIM),
        'nsa_phi_v': w((n_even, NSA_CMP_LEN, HEAD_DIM, HEAD_DIM), NSA_CMP_LEN * HEAD_DIM),
        'odd_w_in': w((n_odd, D_MODEL, ODD_COLS), D_MODEL),
        'odd_w_out': w((n_odd, (SB_HEADS + DIL_HEADS) * HEAD_DIM, D_MODEL), (SB_HEADS + DIL_HEADS) * HEAD_DIM),
        'dil_gq': gain((n_odd, HEAD_DIM)),
        'dil_gk': gain((n_odd, HEAD_DIM)),
        'xattn_norm': gain((DEPTH, D_MODEL)),
        'mem_norm': gain((DEPTH, D_MODEL)),
        'xattn_wq': w((DEPTH, D_MODEL, xw), D_MODEL),
        'xattn_wkv': w((DEPTH, D_MODEL, 2 * xw), D_MODEL),
        'xattn_wo': w((DEPTH, xw, D_MODEL), xw),
        'xattn_gq': gain((DEPTH, HEAD_DIM)),
        'xattn_gk': gain((DEPTH, HEAD_DIM)),
        'ffn_norm': gain((DEPTH, D_MODEL)),
        'ffn_wg': w((DEPTH, D_MODEL, FFN_HIDDEN), D_MODEL),
        'ffn_wu': w((DEPTH, D_MODEL, FFN_HIDDEN), D_MODEL),
        'ffn_wd': w((DEPTH, FFN_HIDDEN, D_MODEL), FFN_HIDDEN),
    }


def reference(x, mem, positions, mix_norm, even_w_in, even_w_out, moba_gq, moba_gk, nsa_gq,
              nsa_gk_cmp, nsa_gk_slc, nsa_gk_win, nsa_pe_k, nsa_pe_v, nsa_phi_k, nsa_phi_v,
              odd_w_in, odd_w_out, dil_gq, dil_gk, xattn_norm, mem_norm, xattn_wq, xattn_wkv,
              xattn_wo, xattn_gq, xattn_gk, ffn_norm, ffn_wg, ffn_wu, ffn_wd):
    for layer in range(DEPTH):
        h = rms_norm(x, mix_norm[layer])
        if layer % 2 == 0:
            e = layer // 2
            x = x + even_mixer(h, positions, even_w_in[e], even_w_out[e], moba_gq[e], moba_gk[e],
                               nsa_gq[e], nsa_gk_cmp[e], nsa_gk_slc[e], nsa_gk_win[e],
                               nsa_pe_k[e], nsa_pe_v[e], nsa_phi_k[e], nsa_phi_v[e])
        else:
            o = layer // 2
            x = x + odd_mixer(h, positions, odd_w_in[o], odd_w_out[o], dil_gq[o], dil_gk[o])
        h = rms_norm(x, xattn_norm[layer])
        mem_n = rms_norm(mem, mem_norm[layer])
        x = x + memory_cross_attention(h, mem_n, xattn_wq[layer], xattn_wkv[layer], xattn_wo[layer],
                                       xattn_gq[layer], xattn_gk[layer])
        h = rms_norm(x, ffn_norm[layer])
        x = x + swiglu(h, ffn_wg[layer], ffn_wu[layer], ffn_wd[layer])
    return x
```

```python
import functools

import jax
import jax.numpy as jnp
from jax import lax
from jax.experimental import pallas as pl
from jax.experimental.pallas import tpu as pltpu

F32 = jnp.float32
BF16 = jnp.bfloat16
HIGHEST = lax.Precision.HIGHEST

HEAD_DIM = 128
HALF = HEAD_DIM // 2
ROPE_THETA = 10000.0
NORM_EPS = 1e-6
TINY = 1e-30
SCALE = HEAD_DIM ** -0.5

MOBA_HEADS = 16
MOBA_BLOCK = 256
MOBA_TOPK = 3
NSA_HEADS = 16
NSA_GROUPS = 4
NSA_HG = NSA_HEADS // NSA_GROUPS
NSA_CMP_LEN = 32
NSA_CMP_STRIDE = 16
NSA_SEL_BLOCK = 64
NSA_SEL_SHIFT = NSA_SEL_BLOCK.bit_length() - 1
NSA_SEL_TOPN = 16
NSA_WINDOW = 512
DIL_CONFIGS = ((128, 1), (512, 4), (2048, 16))
DIL_GROUPS = len(DIL_CONFIGS)
DIL_HEADS = 8
SB_HEADS = 24
XATTN_HEADS = 4

VMEM_LIMIT_BYTES = 56 * 1024 * 1024
NT_DIMS = (((1,), (1,)), ((), ()))


def _params(n_grid):
    return pltpu.CompilerParams(dimension_semantics=("arbitrary",) * n_grid,
                                vmem_limit_bytes=VMEM_LIMIT_BYTES)


def _nt(a, b, precision=None):
    return lax.dot_general(a, b, NT_DIMS, precision=precision, preferred_element_type=F32)


def _rms(x, g):
    return x * lax.rsqrt(jnp.mean(x * x, axis=-1, keepdims=True) + NORM_EPS) * g


def _rope(x, cos, sin_signed):
    return x * cos + pltpu.roll(x, HALF, 1) * sin_signed


def _rmsnorm_kernel(x_ref, g_ref, o_ref):
    o_ref[...] = _rms(x_ref[...], g_ref[...]).astype(o_ref.dtype)


def rmsnorm_rows(x, g, tm=256):
    m, d = x.shape
    tm = min(tm, m)
    return pl.pallas_call(
        _rmsnorm_kernel,
        grid=(m // tm,),
        in_specs=[pl.BlockSpec((tm, d), lambda i: (i, 0)), pl.BlockSpec((1, d), lambda i: (0, 0))],
        out_specs=pl.BlockSpec((tm, d), lambda i: (i, 0)),
        out_shape=jax.ShapeDtypeStruct((m, d), BF16),
        compiler_params=_params(1),
        name="rmsnorm",
    )(x, g.reshape(1, d))


def _matmul_kernel(*refs, nk, has_res):
    if has_res:
        a_ref, w_ref, r_ref, o_ref = refs
    else:
        a_ref, w_ref, o_ref = refs
    part = jnp.dot(a_ref[...], w_ref[...], preferred_element_type=F32)
    if nk == 1:
        if has_res:
            part = part + r_ref[...]
        o_ref[...] = part.astype(o_ref.dtype)
    else:
        k = pl.program_id(2)

        @pl.when(k == 0)
        def _():
            o_ref[...] = (part + r_ref[...]) if has_res else part

        @pl.when(k > 0)
        def _():
            o_ref[...] += part


def _contraction_tile(kdim, limit=6144):
    if kdim <= limit:
        return kdim
    return max(t for t in range(HEAD_DIM, limit + 1, HEAD_DIM) if kdim % t == 0)


def matmul(a, w, residual=None, out_dtype=F32, tm=1024, tn=512, tk=None):
    m, kdim = a.shape
    _, n = w.shape
    tm, tn = min(tm, m), min(tn, n)
    tk = _contraction_tile(kdim) if tk is None else tk
    nk = kdim // tk
    assert m % tm == 0 and kdim % tk == 0
    assert nk == 1 or out_dtype == F32
    in_specs = [pl.BlockSpec((tm, tk), lambda i, j, k: (i, k)),
                pl.BlockSpec((tk, tn), lambda i, j, k: (k, j))]
    args = [a, w]
    if residual is not None:
        in_specs.append(pl.BlockSpec((tm, tn), lambda i, j, k: (i, j)))
        args.append(residual)
    return pl.pallas_call(
        functools.partial(_matmul_kernel, nk=nk, has_res=residual is not None),
        grid=(m // tm, pl.cdiv(n, tn), nk),
        in_specs=in_specs,
        out_specs=pl.BlockSpec((tm, tn), lambda i, j, k: (i, j)),
        out_shape=jax.ShapeDtypeStruct((m, n), out_dtype),
        compiler_params=_params(3),
        name="matmul",
    )(*args)


def _gate_up_kernel(a_ref, wg_ref, wu_ref, o_ref):
    a = a_ref[...]
    g = jnp.dot(a, wg_ref[...], preferred_element_type=F32)
    u = jnp.dot(a, wu_ref[...], preferred_element_type=F32)
    o_ref[...] = (g * jax.nn.sigmoid(g) * u).astype(o_ref.dtype)


def swiglu_gate_up(a, wg, wu, tm=1024, tn=256):
    m, kdim = a.shape
    _, n = wg.shape
    tm, tn = min(tm, m), min(tn, n)
    return pl.pallas_call(
        _gate_up_kernel,
        grid=(m // tm, pl.cdiv(n, tn)),
        in_specs=[pl.BlockSpec((tm, kdim), lambda i, j: (i, 0)),
                  pl.BlockSpec((kdim, tn), lambda i, j: (0, j)),
                  pl.BlockSpec((kdim, tn), lambda i, j: (0, j))],
        out_specs=pl.BlockSpec((tm, tn), lambda i, j: (i, j)),
        out_shape=jax.ShapeDtypeStruct((m, n), BF16),
        compiler_params=_params(2),
        name="swiglu_gate_up",
    )(a, wg, wu)


def _rope_table_kernel(pos_ref, invf_ref, cos_ref, sin_ref):
    ang = pos_ref[...] * invf_ref[...]
    lane = lax.broadcasted_iota(jnp.int32, ang.shape, 1)
    s = jnp.sin(ang)
    cos_ref[...] = jnp.cos(ang)
    sin_ref[...] = jnp.where(lane < HALF, -s, s)


def rope_tables(positions, tr=256):
    n = positions.size
    tr = min(tr, n)
    inv_freq = ROPE_THETA ** (-jnp.arange(HALF, dtype=F32) / HALF)
    invf = jnp.concatenate([inv_freq, inv_freq]).reshape(1, HEAD_DIM)
    pos = jnp.broadcast_to(positions.astype(F32).reshape(n, 1), (n, HEAD_DIM))
    spec = pl.BlockSpec((tr, HEAD_DIM), lambda i: (i, 0))
    return pl.pallas_call(
        _rope_table_kernel,
        grid=(n // tr,),
        in_specs=[spec, pl.BlockSpec((1, HEAD_DIM), lambda i: (0, 0))],
        out_specs=[spec, spec],
        out_shape=[jax.ShapeDtypeStruct((n, HEAD_DIM), F32)] * 2,
        compiler_params=_params(1),
        name="rope_tables",
    )(pos, invf)


def _kprep_kernel(x_ref, g_ref, cos_ref, sin_ref, o_ref, *mean_ref):
    k = _rope(_rms(x_ref[...], g_ref[...]), cos_ref[...], sin_ref[...])
    o_ref[...] = k.astype(o_ref.dtype)
    if mean_ref:
        mean_ref[0][0] = jnp.mean(k, axis=0, keepdims=True)


def kprep(p, col0, nh, g, cos, sin, with_block_mean=False):
    n = p.shape[0]
    tr = MOBA_BLOCK
    out_specs = [pl.BlockSpec((tr, HEAD_DIM), lambda i, h: (i, h))]
    out_shape = [jax.ShapeDtypeStruct((n, nh * HEAD_DIM), BF16)]
    if with_block_mean:
        out_specs.append(pl.BlockSpec((1, 1, HEAD_DIM), lambda i, h: (i, 0, h)))
        out_shape.append(jax.ShapeDtypeStruct((n // tr, 1, nh * HEAD_DIM), F32))
    res = pl.pallas_call(
        _kprep_kernel,
        grid=(n // tr, nh),
        in_specs=[pl.BlockSpec((tr, HEAD_DIM), lambda i, h: (i, col0 + h)),
                  pl.BlockSpec((1, HEAD_DIM), lambda i, h: (0, 0)),
                  pl.BlockSpec((tr, HEAD_DIM), lambda i, h: (i, 0)),
                  pl.BlockSpec((tr, HEAD_DIM), lambda i, h: (i, 0))],
        out_specs=out_specs,
        out_shape=out_shape,
        compiler_params=_params(2),
        name="kprep",
    )(p, g.reshape(1, HEAD_DIM), cos, sin)
    return res if with_block_mean else res[0]


def _softmax_init(s, v, m_ref, l_ref, acc_ref):
    m = jnp.max(s, axis=-1, keepdims=True)
    p = jnp.exp(s - m)
    m_ref[...] = m
    l_ref[...] = jnp.sum(p, axis=-1, keepdims=True)
    acc_ref[...] = jnp.dot(p.astype(BF16), v, preferred_element_type=F32)


def _softmax_update(s, v, m_ref, l_ref, acc_ref):
    m_old = m_ref[...]
    m = jnp.maximum(m_old, jnp.max(s, axis=-1, keepdims=True))
    alpha = jnp.exp(m_old - m)
    p = jnp.exp(s - m)
    m_ref[...] = m
    l_ref[...] = alpha * l_ref[...] + jnp.sum(p, axis=-1, keepdims=True)
    acc_ref[...] = alpha * acc_ref[...] + jnp.dot(p.astype(BF16), v, preferred_element_type=F32)


def _top_rank(vals, n):
    idx = lax.broadcasted_iota(jnp.int32, vals.shape, 1)
    rank = jnp.zeros(vals.shape, jnp.int32)
    for m in range(n):
        c = vals[:, m:m + 1]
        ahead = (c > vals) | ((c == vals) & (idx > m))
        rank = rank + ahead.astype(jnp.int32)
    return rank


def _moba_kernel(q_ref, cos_ref, sin_ref, gq_ref, k_ref, km_ref, v_ref, o_ref,
                 vb_ref, m_ref, l_ref, acc_ref, *, nblk):
    qi = pl.program_id(2)
    blk = MOBA_BLOCK

    @pl.when(qi == 0)
    def _():
        vb_ref[...] = v_ref[...].astype(BF16)

    q = _rope(_rms(q_ref[...], gq_ref[...]), cos_ref[...], sin_ref[...])
    qb = (q * SCALE).astype(BF16)
    gate = _nt(q, km_ref[0], precision=HIGHEST)
    bidx = lax.broadcasted_iota(jnp.int32, gate.shape, 1)
    past = bidx < qi
    gate = jnp.where(past, gate, -jnp.inf)
    sel = (past & (_top_rank(gate, nblk) < MOBA_TOPK)).astype(F32)

    row = lax.broadcasted_iota(jnp.int32, (blk, blk), 0)
    col = lax.broadcasted_iota(jnp.int32, (blk, blk), 1)
    own = pl.ds(pl.multiple_of(qi * blk, blk), blk)
    s = jnp.where(col <= row, _nt(qb, k_ref[own, :]), -jnp.inf)
    _softmax_init(s, vb_ref[own, :], m_ref, l_ref, acc_ref)

    for n in range(nblk - 1):
        @pl.when(n < qi)
        def _(n=n):
            rows = pl.ds(n * blk, blk)
            s = jnp.where(sel[:, n:n + 1] > 0.5, _nt(qb, k_ref[rows, :]), -jnp.inf)
            _softmax_update(s, vb_ref[rows, :], m_ref, l_ref, acc_ref)

    o_ref[...] = (acc_ref[...] / l_ref[...]).astype(o_ref.dtype)


def moba_attention(p, q_col, v_col, kb, kmean, gq, cos, sin, batch, seq):
    nblk = seq // MOBA_BLOCK
    n = batch * seq
    blk = MOBA_BLOCK
    return pl.pallas_call(
        functools.partial(_moba_kernel, nblk=nblk),
        grid=(batch, MOBA_HEADS, nblk),
        in_specs=[pl.BlockSpec((blk, HEAD_DIM), lambda b, h, i: (b * nblk + i, q_col + h)),
                  pl.BlockSpec((blk, HEAD_DIM), lambda b, h, i: (b * nblk + i, 0)),
                  pl.BlockSpec((blk, HEAD_DIM), lambda b, h, i: (b * nblk + i, 0)),
                  pl.BlockSpec((1, HEAD_DIM), lambda b, h, i: (0, 0)),
                  pl.BlockSpec((seq, HEAD_DIM), lambda b, h, i: (b, h)),
                  pl.BlockSpec((1, nblk, HEAD_DIM), lambda b, h, i: (b, 0, h)),
                  pl.BlockSpec((seq, HEAD_DIM), lambda b, h, i: (b, v_col + h))],
        out_specs=pl.BlockSpec((blk, HEAD_DIM), lambda b, h, i: (b * nblk + i, h)),
        out_shape=jax.ShapeDtypeStruct((n, MOBA_HEADS * HEAD_DIM), BF16),
        scratch_shapes=[pltpu.VMEM((seq, HEAD_DIM), BF16),
                        pltpu.VMEM((blk, 1), F32), pltpu.VMEM((blk, 1), F32),
                        pltpu.VMEM((blk, HEAD_DIM), F32)],
        compiler_params=_params(3),
        name="moba",
    )(p, cos, sin, gq.reshape(1, HEAD_DIM), kb, kmean.reshape(batch, nblk, -1), p)


def _nsa_compress_kernel(kc_ref, vc_ref, pek_ref, pev_ref, phik_ref, phiv_ref, g_ref, cos_ref, sin_ref,
                         ko_ref, vo_ref, *, nslot):
    half = NSA_CMP_LEN // 2

    def compress(x_ref, pe_ref, phi_ref):
        first = jnp.zeros((nslot, HEAD_DIM), F32)
        second = jnp.zeros((nslot, HEAD_DIM), F32)
        for l in range(half):
            xl = x_ref[pl.ds(l, nslot, stride=NSA_CMP_STRIDE), :]
            first += jnp.dot((xl + pe_ref[l:l + 1, :]).astype(BF16), phi_ref[l].astype(BF16),
                             preferred_element_type=F32)
            second += jnp.dot((xl + pe_ref[half + l:half + l + 1, :]).astype(BF16),
                              phi_ref[half + l].astype(BF16), preferred_element_type=F32)
        return pltpu.roll(first, 1, 0) + second

    ends = pl.ds(NSA_CMP_STRIDE - 1, nslot, stride=NSA_CMP_STRIDE)
    kc = compress(kc_ref, pek_ref, phik_ref)
    kc = _rope(_rms(kc, g_ref[...]), cos_ref[ends, :], sin_ref[ends, :])
    ko_ref[0, 0] = kc.astype(BF16)
    vo_ref[0, 0] = compress(vc_ref, pev_ref, phiv_ref).astype(BF16)


def nsa_compress(p, kc_col, vc_col, pe_k, pe_v, phi_k, phi_v, g_kc, cos, sin, batch, seq):
    nslot = seq // NSA_CMP_STRIDE
    full = lambda shape: pl.BlockSpec(shape, lambda b, g: (0,) * len(shape))
    out_spec = pl.BlockSpec((1, 1, nslot, HEAD_DIM), lambda b, g: (b, g, 0, 0))
    out_shape = jax.ShapeDtypeStruct((batch, NSA_GROUPS, nslot, HEAD_DIM), BF16)
    return pl.pallas_call(
        functools.partial(_nsa_compress_kernel, nslot=nslot),
        grid=(batch, NSA_GROUPS),
        in_specs=[pl.BlockSpec((seq, HEAD_DIM), lambda b, g: (b, kc_col + g)),
                  pl.BlockSpec((seq, HEAD_DIM), lambda b, g: (b, vc_col + g)),
                  full((NSA_CMP_LEN, HEAD_DIM)), full((NSA_CMP_LEN, HEAD_DIM)),
                  full((NSA_CMP_LEN, HEAD_DIM, HEAD_DIM)), full((NSA_CMP_LEN, HEAD_DIM, HEAD_DIM)),
                  full((1, HEAD_DIM)),
                  pl.BlockSpec((seq, HEAD_DIM), lambda b, g: (b, 0)),
                  pl.BlockSpec((seq, HEAD_DIM), lambda b, g: (b, 0))],
        out_specs=[out_spec, out_spec],
        out_shape=[out_shape, out_shape],
        compiler_params=_params(2),
        name="nsa_compress",
    )(p, p, pe_k, pe_v, phi_k, phi_v, g_kc.reshape(1, HEAD_DIM), cos, sin)


def _nsa_kernel(q_ref, cos_ref, sin_ref, gq_ref, gl_ref, kc_ref, vc_ref, ks_ref, vs_ref, kw_ref, vw_ref,
                o_ref, vsb_ref, vwb_ref, m_ref, l_ref, acc_ref, *, tq, nsel, nslot):
    qi = pl.program_id(2)
    q0 = qi * tq
    hg = NSA_HG
    rows = hg * tq

    @pl.when(qi == 0)
    def _():
        vsb_ref[...] = vs_ref[...].astype(BF16)
        vwb_ref[...] = vw_ref[...].astype(BF16)

    cos, sin, gq = cos_ref[...], sin_ref[...], gq_ref[...]
    qs = []
    for h in range(hg):
        qh = _rope(_rms(q_ref[:, h * HEAD_DIM:(h + 1) * HEAD_DIM], gq), cos, sin)
        qs.append((qh * SCALE).astype(BF16))
    qst = jnp.concatenate(qs, axis=0)

    def tile_heads(x):
        return jnp.concatenate([x] * hg, axis=0)

    slot = lax.broadcasted_iota(jnp.int32, (tq, nslot), 1)
    tpos_c = q0 + lax.broadcasted_iota(jnp.int32, (tq, nslot), 0)
    valid = (slot >= 1) & (slot * NSA_CMP_STRIDE + (NSA_CMP_STRIDE - 1) <= tpos_c)
    bias_c = tile_heads(jnp.where(valid, 0.0, -jnp.inf))
    s_c = _nt(qst, kc_ref[0, 0]) + bias_c
    m_c = jnp.max(s_c, axis=-1, keepdims=True)
    m_c = jnp.where(m_c == -jnp.inf, 0.0, m_c)
    e_c = jnp.exp(s_c - m_c)
    p_c = e_c / jnp.maximum(jnp.sum(e_c, axis=-1, keepdims=True), TINY)
    o_c = jnp.dot(p_c.astype(BF16), vc_ref[0, 0], preferred_element_type=F32)

    p_sum = p_c[0:tq]
    for h in range(1, hg):
        p_sum = p_sum + p_c[h * tq:(h + 1) * tq]
    per_sel = NSA_SEL_BLOCK // NSA_CMP_STRIDE
    span = NSA_CMP_LEN // NSA_CMP_STRIDE
    j_idx = lax.broadcasted_iota(jnp.int32, (nslot, nsel), 0)
    b_idx = lax.broadcasted_iota(jnp.int32, (nslot, nsel), 1)
    overlap = ((j_idx >= 1) & (j_idx - 1 > per_sel * b_idx - span) & (j_idx - 1 < per_sel * (b_idx + 1)))
    imp = jnp.dot(p_sum, overlap.astype(F32), precision=HIGHEST, preferred_element_type=F32)
    blk = lax.broadcasted_iota(jnp.int32, (tq, nsel), 1)
    cur = (q0 + lax.broadcasted_iota(jnp.int32, (tq, nsel), 0)) >> NSA_SEL_SHIFT
    forced = (blk == cur) | (blk == 0)
    imp = jnp.where(forced, jnp.inf, jnp.where(blk <= cur, imp, -jnp.inf))
    sel = (_top_rank(imp, nsel) < min(NSA_SEL_TOPN, nsel)).astype(BF16)

    tk = tq
    row = lax.broadcasted_iota(jnp.int32, (tq, tk), 0)
    col = lax.broadcasted_iota(jnp.int32, (tq, tk), 1)
    e_row = lax.broadcasted_iota(jnp.int32, (nsel, tk), 0)
    e_col = lax.broadcasted_iota(jnp.int32, (nsel, tk), 1)

    def sel_keys(c):
        expand = (((c * tk + e_col) >> NSA_SEL_SHIFT) == e_row).astype(BF16)
        return jnp.dot(sel, expand, preferred_element_type=F32) > 0.5

    own = pl.ds(pl.multiple_of(q0, tq), tk)
    keep = sel_keys(qi) & (col <= row)
    s = _nt(qst, ks_ref[own, :]) + tile_heads(jnp.where(keep, 0.0, -jnp.inf))
    _softmax_init(s, vsb_ref[own, :], m_ref, l_ref, acc_ref)

    def slc_step(c, carry):
        keys = pl.ds(pl.multiple_of(c * tk, tk), tk)
        s = _nt(qst, ks_ref[keys, :]) + tile_heads(jnp.where(sel_keys(c), 0.0, -jnp.inf))
        _softmax_update(s, vsb_ref[keys, :], m_ref, l_ref, acc_ref)
        return carry

    lax.fori_loop(0, qi, slc_step, 0)
    o_s = acc_ref[...] / l_ref[...]

    s = _nt(qst, kw_ref[own, :]) + tile_heads(jnp.where(col <= row, 0.0, -jnp.inf))
    _softmax_init(s, vwb_ref[own, :], m_ref, l_ref, acc_ref)
    nback = NSA_WINDOW // tk
    for d in range(1, nback + 1):
        @pl.when(qi >= d)
        def _(d=d):
            keys = pl.ds(pl.multiple_of((qi - d) * tk, tk), tk)
            s = _nt(qst, kw_ref[keys, :])
            if d == nback:
                s = s + tile_heads(jnp.where(col > row, 0.0, -jnp.inf))
            _softmax_update(s, vwb_ref[keys, :], m_ref, l_ref, acc_ref)
    o_w = acc_ref[...] / l_ref[...]

    gates = jax.nn.sigmoid(gl_ref[...])
    for h in range(hg):
        hs = slice(h * tq, (h + 1) * tq)
        out = (gates[:, 3 * h:3 * h + 1] * o_c[hs] + gates[:, 3 * h + 1:3 * h + 2] * o_s[hs]
               + gates[:, 3 * h + 2:3 * h + 3] * o_w[hs])
        o_ref[:, h * HEAD_DIM:(h + 1) * HEAD_DIM] = out.astype(o_ref.dtype)


def nsa_attention(p, q_col, vs_col, vw_col, gl, kcmp, vcmp, ksb, kwb, gq, cos, sin, batch, seq, tq=128):
    tq = min(tq, seq)
    nq = seq // tq
    n = batch * seq
    gw = NSA_HG * HEAD_DIM
    nsel = seq // NSA_SEL_BLOCK
    nslot = seq // NSA_CMP_STRIDE
    qtile = lambda c: pl.BlockSpec((tq, HEAD_DIM), lambda b, g, i: (b * nq + i, c))
    head = lambda c0: pl.BlockSpec((seq, HEAD_DIM), lambda b, g, i: (b, c0 + g))
    cmp_spec = pl.BlockSpec((1, 1, nslot, HEAD_DIM), lambda b, g, i: (b, g, 0, 0))
    return pl.pallas_call(
        functools.partial(_nsa_kernel, tq=tq, nsel=nsel, nslot=nslot),
        grid=(batch, NSA_GROUPS, nq),
        in_specs=[pl.BlockSpec((tq, gw), lambda b, g, i: (b * nq + i, q_col // NSA_HG + g)),
                  qtile(0), qtile(0),
                  pl.BlockSpec((1, HEAD_DIM), lambda b, g, i: (0, 0)),
                  pl.BlockSpec((tq, HEAD_DIM), lambda b, g, i: (b * nq + i, g)),
                  cmp_spec, cmp_spec,
                  head(0), head(vs_col), head(0), head(vw_col)],
        out_specs=pl.BlockSpec((tq, gw), lambda b, g, i: (b * nq + i, g)),
        out_shape=jax.ShapeDtypeStruct((n, NSA_HEADS * HEAD_DIM), BF16),
        scratch_shapes=[pltpu.VMEM((seq, HEAD_DIM), BF16), pltpu.VMEM((seq, HEAD_DIM), BF16),
                        pltpu.VMEM((NSA_HG * tq, 1), F32), pltpu.VMEM((NSA_HG * tq, 1), F32),
                        pltpu.VMEM((NSA_HG * tq, HEAD_DIM), F32)],
        compiler_params=_params(3),
        name="nsa",
    )(p, cos, sin, gq.reshape(1, HEAD_DIM), gl, kcmp, vcmp, ksb, p, kwb, p)


def _sb_kernel(q_ref, k_ref, v_ref, o_ref, kb_ref, vb_ref, run_ref, acc_ref, *, tq):
    qi = pl.program_id(2)
    q0 = qi * tq
    tk = tq

    @pl.when(qi == 0)
    def _():
        kb_ref[...] = k_ref[...].astype(BF16)
        vb_ref[...] = v_ref[...].astype(BF16)

    qb = (q_ref[...] * SCALE).astype(BF16)
    row = lax.broadcasted_iota(jnp.int32, (tq, tk), 0)
    col = lax.broadcasted_iota(jnp.int32, (tq, tk), 1)
    later = (row > col).astype(BF16)

    def tile(c, strict):
        keys = pl.ds(pl.multiple_of(c * tk, tk), tk)
        z = _nt(qb, kb_ref[keys, :])
        log_beta = jnp.minimum(z, 0.0) - jnp.log1p(jnp.exp(-jnp.abs(z)))
        log_rest = log_beta - z
        if strict is not None:
            log_rest = jnp.where(strict, log_rest, 0.0)
        hi = log_rest.astype(BF16)
        lo = (log_rest - hi.astype(F32)).astype(BF16)
        inside = (jnp.dot(hi, later, preferred_element_type=F32)
                  + jnp.dot(lo, later, preferred_element_type=F32))
        a = jnp.exp(log_beta + inside + run_ref[...])
        if strict is not None:
            a = jnp.where(strict, a, 0.0)
        acc_ref[...] += jnp.dot(a.astype(BF16), vb_ref[keys, :], preferred_element_type=F32)
        run_ref[...] += jnp.sum(log_rest, axis=-1, keepdims=True)

    run_ref[...] = jnp.zeros_like(run_ref)
    acc_ref[...] = jnp.zeros_like(acc_ref)
    tile(qi, col < row)

    def step(i, carry):
        tile(qi - 1 - i, None)
        return carry

    lax.fori_loop(0, qi, step, 0)
    o_ref[...] = acc_ref[...].astype(o_ref.dtype)


def sb_attention(p, q_col, k_col, v_col, batch, seq, tq=128):
    tq = min(tq, seq)
    nq = seq // tq
    n = batch * seq
    head = lambda c0: pl.BlockSpec((seq, HEAD_DIM), lambda b, h, i: (b, c0 + h))
    return pl.pallas_call(
        functools.partial(_sb_kernel, tq=tq),
        grid=(batch, SB_HEADS, nq),
        in_specs=[pl.BlockSpec((tq, HEAD_DIM), lambda b, h, i: (b * nq + i, q_col + h)),
                  head(k_col), head(v_col)],
        out_specs=pl.BlockSpec((tq, HEAD_DIM), lambda b, h, i: (b * nq + i, h)),
        out_shape=jax.ShapeDtypeStruct((n, SB_HEADS * HEAD_DIM), BF16),
        scratch_shapes=[pltpu.VMEM((seq, HEAD_DIM), BF16), pltpu.VMEM((seq, HEAD_DIM), BF16),
                        pltpu.VMEM((tq, 1), F32), pltpu.VMEM((tq, HEAD_DIM), F32)],
        compiler_params=_params(3),
        name="stick_breaking",
    )(p, p, p)


def _dil_kernel(q0_ref, q1_ref, q2_ref, cos_ref, sin_ref, gq_ref, k0_ref, k1_ref, k2_ref,
                v0_ref, v1_ref, v2_ref, o_ref, vb_ref, m_ref, l_ref, acc_ref, *, tq, seq):
    qi = pl.program_id(2)
    q0 = qi * tq
    tk = tq
    q_refs, k_refs, v_refs = (q0_ref, q1_ref, q2_ref), (k0_ref, k1_ref, k2_ref), (v0_ref, v1_ref, v2_ref)

    @pl.when(qi == 0)
    def _():
        for gi in range(DIL_GROUPS):
            vb_ref[gi] = v_refs[gi][...].astype(BF16)

    cos, sin, gq = cos_ref[...], sin_ref[...], gq_ref[...]
    row = lax.broadcasted_iota(jnp.int32, (tq, tk), 0)
    col = lax.broadcasted_iota(jnp.int32, (tq, tk), 1)

    for gi, (window, dil) in enumerate(DIL_CONFIGS):
        qb = (_rope(_rms(q_refs[gi][...], gq), cos, sin) * SCALE).astype(BF16)

        def scores(c, back, gi=gi, qb=qb, window=window, dil=dil):
            keys = pl.ds(pl.multiple_of(c * tk, tk), tk)
            delta = row - col + back * tk
            ok = (delta >= 0) & (delta <= window) & ((delta & (dil - 1)) == 0)
            s = jnp.where(ok, _nt(qb, k_refs[gi][keys, :]), -jnp.inf)
            return s, vb_ref[gi, keys, :]

        if gi == 0:
            _softmax_init(*scores(qi, 0), m_ref, l_ref, acc_ref)
        else:
            _softmax_update(*scores(qi, 0), m_ref, l_ref, acc_ref)

        if window < seq:
            for d in range(1, window // tk + 1):
                @pl.when(qi >= d)
                def _(d=d, scores=scores):
                    _softmax_update(*scores(qi - d, d), m_ref, l_ref, acc_ref)
        else:
            def step(c, carry, scores=scores):
                _softmax_update(*scores(c, qi - c), m_ref, l_ref, acc_ref)
                return carry

            lax.fori_loop(0, qi, step, 0)

    o_ref[...] = (acc_ref[...] / l_ref[...]).astype(o_ref.dtype)


def dilated_attention(p, q_col, v_col, kb, gq, cos, sin, batch, seq, tq=128):
    tq = min(tq, seq)
    nq = seq // tq
    n = batch * seq
    qtile = lambda c0: pl.BlockSpec((tq, HEAD_DIM), lambda b, h, i: (b * nq + i, c0 + h))
    head = lambda c0: pl.BlockSpec((seq, HEAD_DIM), lambda b, h, i: (b, c0 + h))
    groups = range(DIL_GROUPS)
    return pl.pallas_call(
        functools.partial(_dil_kernel, tq=tq, seq=seq),
        grid=(batch, DIL_HEADS, nq),
        in_specs=([qtile(q_col + gi * DIL_HEADS) for gi in groups]
                  + [pl.BlockSpec((tq, HEAD_DIM), lambda b, h, i: (b * nq + i, 0))] * 2
                  + [pl.BlockSpec((1, HEAD_DIM), lambda b, h, i: (0, 0))]
                  + [head(gi * DIL_HEADS) for gi in groups]
                  + [head(v_col + gi * DIL_HEADS) for gi in groups]),
        out_specs=pl.BlockSpec((tq, HEAD_DIM), lambda b, h, i: (b * nq + i, h)),
        out_shape=jax.ShapeDtypeStruct((n, DIL_HEADS * HEAD_DIM), BF16),
        scratch_shapes=[pltpu.VMEM((DIL_GROUPS, seq, HEAD_DIM), BF16),
                        pltpu.VMEM((tq, 1), F32), pltpu.VMEM((tq, 1), F32),
                        pltpu.VMEM((tq, HEAD_DIM), F32)],
        compiler_params=_params(3),
        name="dilated",
    )(p, p, p, cos, sin, gq.reshape(1, HEAD_DIM), kb, kb, kb, p, p, p)


def _xattn_kernel(q_ref, kv_ref, gq_ref, gk_ref, o_ref):
    gq, gk = gq_ref[...], gk_ref[...]
    width = XATTN_HEADS * HEAD_DIM
    for h in range(XATTN_HEADS):
        hs = slice(h * HEAD_DIM, (h + 1) * HEAD_DIM)
        q = (_rms(q_ref[:, hs], gq) * SCALE).astype(BF16)
        k = _rms(kv_ref[:, hs], gk).astype(BF16)
        v = kv_ref[:, width + h * HEAD_DIM:width + (h + 1) * HEAD_DIM].astype(BF16)
        s = _nt(q, k)
        e = jnp.exp(s - jnp.max(s, axis=-1, keepdims=True))
        p = e / jnp.sum(e, axis=-1, keepdims=True)
        o_ref[:, hs] = jnp.dot(p.astype(BF16), v, preferred_element_type=F32).astype(o_ref.dtype)


def memory_cross_attention(q, kv, gq, gk, batch, seq, mem_len, tq=512):
    tq = min(tq, seq)
    nq = seq // tq
    width = XATTN_HEADS * HEAD_DIM
    return pl.pallas_call(
        _xattn_kernel,
        grid=(batch, nq),
        in_specs=[pl.BlockSpec((tq, width), lambda b, i: (b * nq + i, 0)),
                  pl.BlockSpec((mem_len, 2 * width), lambda b, i: (b, 0)),
                  pl.BlockSpec((1, HEAD_DIM), lambda b, i: (0, 0)),
                  pl.BlockSpec((1, HEAD_DIM), lambda b, i: (0, 0))],
        out_specs=pl.BlockSpec((tq, width), lambda b, i: (b * nq + i, 0)),
        out_shape=jax.ShapeDtypeStruct((batch * seq, width), BF16),
        compiler_params=_params(2),
        name="xattn",
    )(q, kv, gq.reshape(1, HEAD_DIM), gk.reshape(1, HEAD_DIM))


def _even_mixer(x2, h, cos, sin, batch, seq, w_in, w_out, moba_gq, moba_gk, nsa_gq, nsa_gk_cmp,
                nsa_gk_slc, nsa_gk_win, pe_k, pe_v, phi_k, phi_v):
    hd = HEAD_DIM
    main = (3 * MOBA_HEADS + NSA_HEADS + 6 * NSA_GROUPS) * hd
    p = matmul(h, w_in[:, :main].astype(BF16))
    w_gl = w_in[:, main:].reshape(-1, NSA_GROUPS, 3 * NSA_HG)
    w_gl = jnp.pad(w_gl, ((0, 0), (0, 0), (0, hd - 3 * NSA_HG))).reshape(-1, NSA_GROUPS * hd)
    gl = matmul(h, w_gl.astype(BF16))

    mkb, kmean = kprep(p, 16, MOBA_HEADS, moba_gk, cos, sin, with_block_mean=True)
    o_a = moba_attention(p, 0, 32, mkb, kmean, moba_gq, cos, sin, batch, seq)

    kcmp, vcmp = nsa_compress(p, 64, 68, pe_k, pe_v, phi_k, phi_v, nsa_gk_cmp, cos, sin, batch, seq)
    ksb = kprep(p, 72, NSA_GROUPS, nsa_gk_slc, cos, sin)
    kwb = kprep(p, 80, NSA_GROUPS, nsa_gk_win, cos, sin)
    o_b = nsa_attention(p, 48, 76, 84, gl, kcmp, vcmp, ksb, kwb, nsa_gq, cos, sin, batch, seq)

    split = MOBA_HEADS * hd
    x2 = matmul(o_a, w_out[:split].astype(BF16), residual=x2)
    return matmul(o_b, w_out[split:].astype(BF16), residual=x2)


def _odd_mixer(x2, h, cos, sin, batch, seq, w_in, w_out, dil_gq, dil_gk):
    hd = HEAD_DIM
    p = matmul(h, w_in.astype(BF16))
    o_c = sb_attention(p, 0, SB_HEADS, 2 * SB_HEADS, batch, seq)
    nd = DIL_GROUPS * DIL_HEADS
    dkb = kprep(p, 3 * SB_HEADS + nd, nd, dil_gk, cos, sin)
    o_d = dilated_attention(p, 3 * SB_HEADS, 3 * SB_HEADS + 2 * nd, dkb, dil_gq, cos, sin, batch, seq)
    split = SB_HEADS * hd
    x2 = matmul(o_c, w_out[:split].astype(BF16), residual=x2)
    return matmul(o_d, w_out[split:].astype(BF16), residual=x2)


def kernel(x, mem, positions, mix_norm, even_w_in, even_w_out, moba_gq, moba_gk, nsa_gq, nsa_gk_cmp, nsa_gk_slc, nsa_gk_win, nsa_pe_k, nsa_pe_v, nsa_phi_k, nsa_phi_v, odd_w_in, odd_w_out, dil_gq, dil_gk, xattn_norm, mem_norm, xattn_wq, xattn_wkv, xattn_wo, xattn_gq, xattn_gk, ffn_norm, ffn_wg, ffn_wu, ffn_wd):
    batch, seq, d = x.shape
    mem_len = mem.shape[1]
    depth = mix_norm.shape[0]
    x2 = x.reshape(batch * seq, d)
    mem2 = mem.reshape(batch * mem_len, d)
    cos, sin = rope_tables(positions)

    for layer in range(depth):
        h = rmsnorm_rows(x2, mix_norm[layer])
        if layer % 2 == 0:
            e = layer // 2
            x2 = _even_mixer(x2, h, cos, sin, batch, seq, even_w_in[e], even_w_out[e], moba_gq[e],
                             moba_gk[e], nsa_gq[e], nsa_gk_cmp[e], nsa_gk_slc[e], nsa_gk_win[e],
                             nsa_pe_k[e], nsa_pe_v[e], nsa_phi_k[e], nsa_phi_v[e])
        else:
            o = layer // 2
            x2 = _odd_mixer(x2, h, cos, sin, batch, seq, odd_w_in[o], odd_w_out[o], dil_gq[o], dil_gk[o])

        h = rmsnorm_rows(x2, xattn_norm[layer])
        mem_n = rmsnorm_rows(mem2, mem_norm[layer])
        q = matmul(h, xattn_wq[layer].astype(BF16))
        kv = matmul(mem_n, xattn_wkv[layer].astype(BF16))
        o_x = memory_cross_attention(q, kv, xattn_gq[layer], xattn_gk[layer], batch, seq, mem_len)
        x2 = matmul(o_x, xattn_wo[layer].astype(BF16), residual=x2)

        h = rmsnorm_rows(x2, ffn_norm[layer])
        hidden = swiglu_gate_up(h, ffn_wg[layer].astype(BF16), ffn_wu[layer].astype(BF16))
        x2 = matmul(hidden, ffn_wd[layer].astype(BF16), residual=x2)

    return x2.reshape(batch, seq, d)
```

```python
import functools

import jax
import jax.numpy as jnp
from jax import lax
from jax.experimental import pallas as pl
from jax.experimental.pallas import tpu as pltpu

F32 = jnp.float32
BF16 = jnp.bfloat16
HIGHEST = lax.Precision.HIGHEST

HEAD_DIM = 128
HALF = HEAD_DIM // 2
ROPE_THETA = 10000.0
NORM_EPS = 1e-6
TINY = 1e-30
SCALE = HEAD_DIM ** -0.5

MOBA_HEADS = 16
MOBA_BLOCK = 256
MOBA_TOPK = 3
NSA_HEADS = 16
NSA_GROUPS = 4
NSA_HG = NSA_HEADS // NSA_GROUPS
NSA_CMP_LEN = 32
NSA_CMP_STRIDE = 16
NSA_SEL_BLOCK = 64
NSA_SEL_SHIFT = NSA_SEL_BLOCK.bit_length() - 1
NSA_SEL_TOPN = 16
NSA_WINDOW = 512
DIL_CONFIGS = ((128, 1), (512, 4), (2048, 16))
DIL_GROUPS = len(DIL_CONFIGS)
DIL_HEADS = 8
DIL_SPAN = 128
SB_HEADS = 24
XATTN_HEADS = 4

VMEM_LIMIT_BYTES = 56 * 1024 * 1024
NT_DIMS = (((1,), (1,)), ((), ()))
PREP_ROWS = 256


def _params(n_grid):
    return pltpu.CompilerParams(dimension_semantics=("arbitrary",) * n_grid,
                                vmem_limit_bytes=VMEM_LIMIT_BYTES)


def _nt(a, b, precision=None):
    return lax.dot_general(a, b, NT_DIMS, precision=precision, preferred_element_type=F32)


def _rms(x, g):
    return x * lax.rsqrt(jnp.mean(x * x, axis=-1, keepdims=True) + NORM_EPS) * g


def _rope(x, cos, sin_signed):
    return x * cos + pltpu.roll(x, HALF, 1) * sin_signed


def _softmax_chunks(scores, values):
    m = jnp.max(functools.reduce(jnp.maximum, scores), axis=-1, keepdims=True)
    probs = [jnp.exp(s - m) for s in scores]
    l = jnp.sum(functools.reduce(lambda a, b: a + b, probs), axis=-1, keepdims=True)
    acc = None
    for p, v in zip(probs, values):
        part = jnp.dot(p.astype(BF16), v, preferred_element_type=F32)
        acc = part if acc is None else acc + part
    return acc, m, l


def _top_rank(vals, n):
    idx = lax.broadcasted_iota(jnp.int32, vals.shape, 1)
    rank = jnp.zeros(vals.shape, jnp.int32)
    for m in range(n):
        c = vals[:, m:m + 1]
        ahead = (c > vals) | ((c == vals) & (idx > m))
        rank = rank + ahead.astype(jnp.int32)
    return rank


def _rmsnorm_kernel(x_ref, g_ref, o_ref):
    o_ref[...] = _rms(x_ref[...], g_ref[...]).astype(o_ref.dtype)


def rmsnorm_rows(x, g, tm=256):
    m, d = x.shape
    tm = min(tm, m)
    return pl.pallas_call(
        _rmsnorm_kernel,
        grid=(m // tm,),
        in_specs=[pl.BlockSpec((tm, d), lambda i: (i, 0)), pl.BlockSpec((1, d), lambda i: (0, 0))],
        out_specs=pl.BlockSpec((tm, d), lambda i: (i, 0)),
        out_shape=jax.ShapeDtypeStruct((m, d), BF16),
        compiler_params=_params(1),
        name="rmsnorm",
    )(x, g.reshape(1, d))


def _matmul_kernel(*refs, nk, has_res):
    if has_res:
        a_ref, w_ref, r_ref, o_ref = refs
    else:
        a_ref, w_ref, o_ref = refs
    part = jnp.dot(a_ref[...], w_ref[...], preferred_element_type=F32)
    if nk == 1:
        if has_res:
            part = part + r_ref[...]
        o_ref[...] = part.astype(o_ref.dtype)
    else:
        k = pl.program_id(2)

        @pl.when(k == 0)
        def _():
            o_ref[...] = (part + r_ref[...]) if has_res else part

        @pl.when(k > 0)
        def _():
            o_ref[...] += part


def _contraction_tile(kdim, limit=6144):
    if kdim <= limit:
        return kdim
    return max(t for t in range(HEAD_DIM, limit + 1, HEAD_DIM) if kdim % t == 0)


def matmul(a, w, residual=None, out_dtype=F32, tm=1024, tn=512, tk=None):
    m, kdim = a.shape
    _, n = w.shape
    tm, tn = min(tm, m), min(tn, n)
    tk = _contraction_tile(kdim) if tk is None else tk
    nk = kdim // tk
    assert m % tm == 0 and kdim % tk == 0
    assert nk == 1 or out_dtype == F32
    in_specs = [pl.BlockSpec((tm, tk), lambda i, j, k: (i, k)),
                pl.BlockSpec((tk, tn), lambda i, j, k: (k, j))]
    args = [a, w]
    if residual is not None:
        in_specs.append(pl.BlockSpec((tm, tn), lambda i, j, k: (i, j)))
        args.append(residual)
    return pl.pallas_call(
        functools.partial(_matmul_kernel, nk=nk, has_res=residual is not None),
        grid=(m // tm, pl.cdiv(n, tn), nk),
        in_specs=in_specs,
        out_specs=pl.BlockSpec((tm, tn), lambda i, j, k: (i, j)),
        out_shape=jax.ShapeDtypeStruct((m, n), out_dtype),
        compiler_params=_params(3),
        name="matmul",
    )(*args)


def _gate_up_kernel(a_ref, wg_ref, wu_ref, o_ref):
    a = a_ref[...]
    g = jnp.dot(a, wg_ref[...], preferred_element_type=F32)
    u = jnp.dot(a, wu_ref[...], preferred_element_type=F32)
    o_ref[...] = (g * jax.nn.sigmoid(g) * u).astype(o_ref.dtype)


def swiglu_gate_up(a, wg, wu, tm=1024, tn=256):
    m, kdim = a.shape
    _, n = wg.shape
    tm, tn = min(tm, m), min(tn, n)
    return pl.pallas_call(
        _gate_up_kernel,
        grid=(m // tm, pl.cdiv(n, tn)),
        in_specs=[pl.BlockSpec((tm, kdim), lambda i, j: (i, 0)),
                  pl.BlockSpec((kdim, tn), lambda i, j: (0, j)),
                  pl.BlockSpec((kdim, tn), lambda i, j: (0, j))],
        out_specs=pl.BlockSpec((tm, tn), lambda i, j: (i, j)),
        out_shape=jax.ShapeDtypeStruct((m, n), BF16),
        compiler_params=_params(2),
        name="swiglu_gate_up",
    )(a, wg, wu)


def _rope_table_kernel(pos_ref, invf_ref, cos_ref, sin_ref):
    ang = pos_ref[...] * invf_ref[...]
    lane = lax.broadcasted_iota(jnp.int32, ang.shape, 1)
    s = jnp.sin(ang)
    cos_ref[...] = jnp.cos(ang)
    sin_ref[...] = jnp.where(lane < HALF, -s, s)


def rope_tables(positions, tr=256):
    n = positions.size
    tr = min(tr, n)
    inv_freq = ROPE_THETA ** (-jnp.arange(HALF, dtype=F32) / HALF)
    invf = jnp.concatenate([inv_freq, inv_freq]).reshape(1, HEAD_DIM)
    pos = jnp.broadcast_to(positions.astype(F32).reshape(n, 1), (n, HEAD_DIM))
    spec = pl.BlockSpec((tr, HEAD_DIM), lambda i: (i, 0))
    return pl.pallas_call(
        _rope_table_kernel,
        grid=(n // tr,),
        in_specs=[spec, pl.BlockSpec((1, HEAD_DIM), lambda i: (0, 0))],
        out_specs=[spec, spec],
        out_shape=[jax.ShapeDtypeStruct((n, HEAD_DIM), F32)] * 2,
        compiler_params=_params(1),
        name="rope_tables",
    )(pos, invf)


def _prep_keys(k_ref, g, cos_ref, sin_ref, out_ref, seq):
    for r0 in range(0, seq, PREP_ROWS):
        rows = slice(r0, min(r0 + PREP_ROWS, seq))
        out_ref[rows, :] = _rope(_rms(k_ref[rows, :], g), cos_ref[rows, :], sin_ref[rows, :]).astype(BF16)


def _row1(x):
    return x.reshape(1, HEAD_DIM)


_ROW_SPEC3 = pl.BlockSpec((1, HEAD_DIM), lambda b, h, i: (0, 0))


def _moba_kernel(q_ref, k_ref, v_ref, cos_ref, sin_ref, gq_ref, gk_ref, o_ref,
                 kb_ref, vb_ref, km_ref, *, nblk):
    qi = pl.program_id(2)
    blk = MOBA_BLOCK

    @pl.when(qi == 0)
    def _():
        for n in range(nblk):
            rows = slice(n * blk, (n + 1) * blk)
            k = _rope(_rms(k_ref[rows, :], gk_ref[...]), cos_ref[rows, :], sin_ref[rows, :])
            kb_ref[rows, :] = k.astype(BF16)
            km_ref[n:n + 1, :] = jnp.mean(k, axis=0, keepdims=True)
        vb_ref[...] = v_ref[...].astype(BF16)

    own = pl.ds(pl.multiple_of(qi * blk, blk), blk)
    q = _rope(_rms(q_ref[...], gq_ref[...]), cos_ref[own, :], sin_ref[own, :])
    qb = (q * SCALE).astype(BF16)
    gate = _nt(q, km_ref[...], precision=HIGHEST)
    bidx = lax.broadcasted_iota(jnp.int32, gate.shape, 1)
    past = bidx < qi
    gate = jnp.where(past, gate, -jnp.inf)
    sel = (past & (_top_rank(gate, nblk) < MOBA_TOPK)).astype(F32)
    row = lax.broadcasted_iota(jnp.int32, (blk, blk), 0)
    col = lax.broadcasted_iota(jnp.int32, (blk, blk), 1)

    def attend(n_past):
        scores = [jnp.where(col <= row, _nt(qb, kb_ref[own, :]), -jnp.inf)]
        values = [vb_ref[own, :]]
        for n in range(n_past):
            rows = slice(n * blk, (n + 1) * blk)
            scores.append(jnp.where(sel[:, n:n + 1] > 0.5, _nt(qb, kb_ref[rows, :]), -jnp.inf))
            values.append(vb_ref[rows, :])
        acc, _, l = _softmax_chunks(scores, values)
        o_ref[...] = (acc / l).astype(o_ref.dtype)

    half = nblk // 2
    if half >= 2:
        pl.when(qi < half)(lambda: attend(half - 1))
        pl.when(qi >= half)(lambda: attend(nblk - 1))
    else:
        attend(nblk - 1)


def moba_attention(p, q_col, k_col, v_col, gq, gk, cos, sin, batch, seq):
    nblk = seq // MOBA_BLOCK
    blk = MOBA_BLOCK
    head = lambda c0: pl.BlockSpec((seq, HEAD_DIM), lambda b, h, i: (b, c0 + h))
    table = pl.BlockSpec((seq, HEAD_DIM), lambda b, h, i: (b, 0))
    return pl.pallas_call(
        functools.partial(_moba_kernel, nblk=nblk),
        grid=(batch, MOBA_HEADS, nblk),
        in_specs=[pl.BlockSpec((blk, HEAD_DIM), lambda b, h, i: (b * nblk + i, q_col + h)),
                  head(k_col), head(v_col), table, table, _ROW_SPEC3, _ROW_SPEC3],
        out_specs=pl.BlockSpec((blk, HEAD_DIM), lambda b, h, i: (b * nblk + i, h)),
        out_shape=jax.ShapeDtypeStruct((batch * seq, MOBA_HEADS * HEAD_DIM), BF16),
        scratch_shapes=[pltpu.VMEM((seq, HEAD_DIM), BF16), pltpu.VMEM((seq, HEAD_DIM), BF16),
                        pltpu.VMEM((nblk, HEAD_DIM), F32)],
        compiler_params=_params(3),
        name="moba",
    )(p, p, p, cos, sin, _row1(gq), _row1(gk))


def _nsa_compress_kernel(kc_ref, vc_ref, pek_ref, pev_ref, phik_ref, phiv_ref, g_ref, cos_ref, sin_ref,
                         ko_ref, vo_ref, *, nslot):
    half = NSA_CMP_LEN // 2

    def compress(x_ref, pe_ref, phi_ref):
        first = jnp.zeros((nslot, HEAD_DIM), F32)
        second = jnp.zeros((nslot, HEAD_DIM), F32)
        for l in range(half):
            xl = x_ref[pl.ds(l, nslot, stride=NSA_CMP_STRIDE), :]
            first += jnp.dot((xl + pe_ref[l:l + 1, :]).astype(BF16), phi_ref[l].astype(BF16),
                             preferred_element_type=F32)
            second += jnp.dot((xl + pe_ref[half + l:half + l + 1, :]).astype(BF16),
                              phi_ref[half + l].astype(BF16), preferred_element_type=F32)
        return pltpu.roll(first, 1, 0) + second

    ends = pl.ds(NSA_CMP_STRIDE - 1, nslot, stride=NSA_CMP_STRIDE)
    kc = compress(kc_ref, pek_ref, phik_ref)
    kc = _rope(_rms(kc, g_ref[...]), cos_ref[ends, :], sin_ref[ends, :])
    ko_ref[0, 0] = kc.astype(BF16)
    vo_ref[0, 0] = compress(vc_ref, pev_ref, phiv_ref).astype(BF16)


def nsa_compress(p, kc_col, vc_col, pe_k, pe_v, phi_k, phi_v, g_kc, cos, sin, batch, seq):
    nslot = seq // NSA_CMP_STRIDE
    full = lambda shape: pl.BlockSpec(shape, lambda b, g: (0,) * len(shape))
    out_spec = pl.BlockSpec((1, 1, nslot, HEAD_DIM), lambda b, g: (b, g, 0, 0))
    out_shape = jax.ShapeDtypeStruct((batch, NSA_GROUPS, nslot, HEAD_DIM), BF16)
    return pl.pallas_call(
        functools.partial(_nsa_compress_kernel, nslot=nslot),
        grid=(batch, NSA_GROUPS),
        in_specs=[pl.BlockSpec((seq, HEAD_DIM), lambda b, g: (b, kc_col + g)),
                  pl.BlockSpec((seq, HEAD_DIM), lambda b, g: (b, vc_col + g)),
                  full((NSA_CMP_LEN, HEAD_DIM)), full((NSA_CMP_LEN, HEAD_DIM)),
                  full((NSA_CMP_LEN, HEAD_DIM, HEAD_DIM)), full((NSA_CMP_LEN, HEAD_DIM, HEAD_DIM)),
                  full((1, HEAD_DIM)),
                  pl.BlockSpec((seq, HEAD_DIM), lambda b, g: (b, 0)),
                  pl.BlockSpec((seq, HEAD_DIM), lambda b, g: (b, 0))],
        out_specs=[out_spec, out_spec],
        out_shape=[out_shape, out_shape],
        compiler_params=_params(2),
        name="nsa_compress",
    )(p, p, pe_k, pe_v, phi_k, phi_v, _row1(g_kc), cos, sin)


NSA_KEY_CHUNK = 256
NSA_WIDTH_STEP = 512


def _nsa_kernel(q_ref, gl_ref, kc_ref, vc_ref, ks_ref, vs_ref, kw_ref, vw_ref, cos_ref, sin_ref,
                gq_ref, gks_ref, gkw_ref, o_ref, ksb_ref, vsb_ref, kwb_ref, vwb_ref, *, tq, seq):
    qi = pl.program_id(2)
    q0 = qi * tq
    hg = NSA_HG
    nsel = seq // NSA_SEL_BLOCK
    nslot = seq // NSA_CMP_STRIDE

    @pl.when(qi == 0)
    def _():
        _prep_keys(ks_ref, gks_ref[...], cos_ref, sin_ref, ksb_ref, seq)
        _prep_keys(kw_ref, gkw_ref[...], cos_ref, sin_ref, kwb_ref, seq)
        vsb_ref[...] = vs_ref[...].astype(BF16)
        vwb_ref[...] = vw_ref[...].astype(BF16)

    own = pl.ds(pl.multiple_of(q0, tq), tq)
    cos, sin, gq = cos_ref[own, :], sin_ref[own, :], gq_ref[...]
    qs = []
    for h in range(hg):
        qh = _rope(_rms(q_ref[:, h * HEAD_DIM:(h + 1) * HEAD_DIM], gq), cos, sin)
        qs.append((qh * SCALE).astype(BF16))
    qst = jnp.concatenate(qs, axis=0)

    def tile_heads(x):
        return jnp.concatenate([x] * hg, axis=0)

    slot = lax.broadcasted_iota(jnp.int32, (tq, nslot), 1)
    tpos_c = q0 + lax.broadcasted_iota(jnp.int32, (tq, nslot), 0)
    valid = (slot >= 1) & (slot * NSA_CMP_STRIDE + (NSA_CMP_STRIDE - 1) <= tpos_c)
    s_c = _nt(qst, kc_ref[0, 0]) + tile_heads(jnp.where(valid, 0.0, -jnp.inf))
    m_c = jnp.max(s_c, axis=-1, keepdims=True)
    m_c = jnp.where(m_c == -jnp.inf, 0.0, m_c)
    e_c = jnp.exp(s_c - m_c)
    p_c = e_c / jnp.maximum(jnp.sum(e_c, axis=-1, keepdims=True), TINY)
    o_c = jnp.dot(p_c.astype(BF16), vc_ref[0, 0], preferred_element_type=F32)

    p_sum = p_c[0:tq]
    for h in range(1, hg):
        p_sum = p_sum + p_c[h * tq:(h + 1) * tq]
    per_sel = NSA_SEL_BLOCK // NSA_CMP_STRIDE
    span = NSA_CMP_LEN // NSA_CMP_STRIDE
    j_idx = lax.broadcasted_iota(jnp.int32, (nslot, nsel), 0)
    b_idx = lax.broadcasted_iota(jnp.int32, (nslot, nsel), 1)
    overlap = ((j_idx >= 1) & (j_idx - 1 > per_sel * b_idx - span) & (j_idx - 1 < per_sel * (b_idx + 1)))
    imp = jnp.dot(p_sum, overlap.astype(F32), precision=HIGHEST, preferred_element_type=F32)
    blk = lax.broadcasted_iota(jnp.int32, (tq, nsel), 1)
    cur = (q0 + lax.broadcasted_iota(jnp.int32, (tq, nsel), 0)) >> NSA_SEL_SHIFT
    forced = (blk == cur) | (blk == 0)
    imp = jnp.where(forced, jnp.inf, jnp.where(blk <= cur, imp, -jnp.inf))
    sel = (_top_rank(imp, nsel) < min(NSA_SEL_TOPN, nsel)).astype(BF16)

    row = lax.broadcasted_iota(jnp.int32, (tq, tq), 0)
    col = lax.broadcasted_iota(jnp.int32, (tq, tq), 1)
    nback = NSA_WINDOW // tq
    scores, values = [], []
    for d in range(nback + 1):
        keys = pl.ds(pl.multiple_of(jnp.maximum(qi - d, 0) * tq, tq), tq)
        s = _nt(qst, kwb_ref[keys, :])
        exists = jnp.where(qi >= d, 0.0, -jnp.inf)
        if d == 0:
            s = s + tile_heads(jnp.where(col <= row, 0.0, -jnp.inf))
        elif d == nback:
            s = s + tile_heads(jnp.where(col > row, exists, -jnp.inf))
        else:
            s = s + exists
        scores.append(s)
        values.append(vwb_ref[keys, :])
    acc_w, _, l_w = _softmax_chunks(scores, values)
    o_w = acc_w / l_w

    gates = jax.nn.sigmoid(gl_ref[...])

    ck = min(NSA_KEY_CHUNK, seq)
    e_row = lax.broadcasted_iota(jnp.int32, (nsel, ck), 0)
    e_col = lax.broadcasted_iota(jnp.int32, (nsel, ck), 1)
    k_col = lax.broadcasted_iota(jnp.int32, (tq, ck), 1)
    t_row = q0 + lax.broadcasted_iota(jnp.int32, (tq, ck), 0)

    def finish(width):
        scores, values = [], []
        for c in range(width // ck):
            expand = (((c * ck + e_col) >> NSA_SEL_SHIFT) == e_row).astype(BF16)
            chosen = jnp.dot(sel, expand, preferred_element_type=F32) > 0.5
            keep = chosen & (c * ck + k_col <= t_row)
            keys = slice(c * ck, (c + 1) * ck)
            scores.append(_nt(qst, ksb_ref[keys, :]) + tile_heads(jnp.where(keep, 0.0, -jnp.inf)))
            values.append(vsb_ref[keys, :])
        acc_s, _, l_s = _softmax_chunks(scores, values)
        o_s = acc_s / l_s
        for h in range(hg):
            hs = slice(h * tq, (h + 1) * tq)
            out = (gates[:, 3 * h:3 * h + 1] * o_c[hs] + gates[:, 3 * h + 1:3 * h + 2] * o_s[hs]
                   + gates[:, 3 * h + 2:3 * h + 3] * o_w[hs])
            o_ref[:, h * HEAD_DIM:(h + 1) * HEAD_DIM] = out.astype(o_ref.dtype)

    step = min(NSA_WIDTH_STEP, seq)
    for w in range(seq // step):
        pl.when((q0 // step) == w)(functools.partial(finish, (w + 1) * step))


def nsa_attention(p, q_col, ks_col, vs_col, kw_col, vw_col, gl, kcmp, vcmp, gq, gks, gkw, cos, sin,
                  batch, seq, tq=128):
    tq = min(tq, seq)
    nq = seq // tq
    gw = NSA_HG * HEAD_DIM
    nslot = seq // NSA_CMP_STRIDE
    head = lambda c0: pl.BlockSpec((seq, HEAD_DIM), lambda b, g, i: (b, c0 + g))
    table = pl.BlockSpec((seq, HEAD_DIM), lambda b, g, i: (b, 0))
    cmp_spec = pl.BlockSpec((1, 1, nslot, HEAD_DIM), lambda b, g, i: (b, g, 0, 0))
    return pl.pallas_call(
        functools.partial(_nsa_kernel, tq=tq, seq=seq),
        grid=(batch, NSA_GROUPS, nq),
        in_specs=[pl.BlockSpec((tq, gw), lambda b, g, i: (b * nq + i, q_col // NSA_HG + g)),
                  pl.BlockSpec((tq, HEAD_DIM), lambda b, g, i: (b * nq + i, g)),
                  cmp_spec, cmp_spec,
                  head(ks_col), head(vs_col), head(kw_col), head(vw_col), table, table,
                  _ROW_SPEC3, _ROW_SPEC3, _ROW_SPEC3],
        out_specs=pl.BlockSpec((tq, gw), lambda b, g, i: (b * nq + i, g)),
        out_shape=jax.ShapeDtypeStruct((batch * seq, NSA_HEADS * HEAD_DIM), BF16),
        scratch_shapes=[pltpu.VMEM((seq, HEAD_DIM), BF16)] * 4,
        compiler_params=_params(3),
        name="nsa",
    )(p, gl, kcmp, vcmp, p, p, p, p, cos, sin, _row1(gq), _row1(gks), _row1(gkw))


def _sb_kernel(q_ref, k_ref, v_ref, o_ref, kb_ref, vb_ref, run_ref, acc_ref, *, tq, tk):
    qi = pl.program_id(2)
    q0 = qi * tq
    per_q = tq // tk

    @pl.when(qi == 0)
    def _():
        kb_ref[...] = k_ref[...].astype(BF16)
        vb_ref[...] = v_ref[...].astype(BF16)

    qb = (q_ref[...] * SCALE).astype(BF16)
    r_idx = lax.broadcasted_iota(jnp.int32, (tk, tk + HEAD_DIM), 0)
    c_idx = lax.broadcasted_iota(jnp.int32, (tk, tk + HEAD_DIM), 1)
    later_ones = ((r_idx > c_idx) | (c_idx >= tk)).astype(BF16)
    row = lax.broadcasted_iota(jnp.int32, (tq, tk), 0)
    col = lax.broadcasted_iota(jnp.int32, (tq, tk), 1)

    def tile(c, on_diagonal):
        keys = pl.ds(pl.multiple_of(c * tk, tk), tk)
        z = _nt(qb, kb_ref[keys, :])
        log_beta = jnp.minimum(z, 0.0) - jnp.log(1.0 + jnp.exp(-jnp.abs(z)))
        log_rest = log_beta - z
        if on_diagonal:
            strict = c * tk + col < q0 + row
            log_rest = jnp.where(strict, log_rest, 0.0)
        hi = log_rest.astype(BF16)
        lo = (log_rest - hi.astype(F32)).astype(BF16)
        sums = (jnp.dot(hi, later_ones, preferred_element_type=F32)
                + jnp.dot(lo, later_ones, preferred_element_type=F32))
        run = run_ref[...]
        a = jnp.exp(log_beta + sums[:, :tk] + jnp.concatenate([run] * (tk // HEAD_DIM), axis=1))
        if on_diagonal:
            a = jnp.where(strict, a, 0.0)
        acc_ref[...] += jnp.dot(a.astype(BF16), vb_ref[keys, :], preferred_element_type=F32)
        run_ref[...] = run + sums[:, tk:]

    run_ref[...] = jnp.zeros_like(run_ref)
    acc_ref[...] = jnp.zeros_like(acc_ref)
    top = (qi + 1) * per_q - 1
    for j in range(per_q):
        tile(top - j, True)

    def step(i, carry):
        for j in range(per_q):
            tile(qi * per_q - 1 - i * per_q - j, False)
        return carry

    lax.fori_loop(0, qi, step, 0)
    o_ref[...] = acc_ref[...].astype(o_ref.dtype)


def sb_attention(p, q_col, k_col, v_col, batch, seq, tq=512, tk=256):
    tq = min(tq, seq)
    tk = min(tk, tq)
    nq = seq // tq
    head = lambda c0: pl.BlockSpec((seq, HEAD_DIM), lambda b, h, i: (b, c0 + h))
    return pl.pallas_call(
        functools.partial(_sb_kernel, tq=tq, tk=tk),
        grid=(batch, SB_HEADS, nq),
        in_specs=[pl.BlockSpec((tq, HEAD_DIM), lambda b, h, i: (b * nq + i, q_col + h)),
                  head(k_col), head(v_col)],
        out_specs=pl.BlockSpec((tq, HEAD_DIM), lambda b, h, i: (b * nq + i, h)),
        out_shape=jax.ShapeDtypeStruct((batch * seq, SB_HEADS * HEAD_DIM), BF16),
        scratch_shapes=[pltpu.VMEM((seq, HEAD_DIM), BF16), pltpu.VMEM((seq, HEAD_DIM), BF16),
                        pltpu.VMEM((tq, HEAD_DIM), F32), pltpu.VMEM((tq, HEAD_DIM), F32)],
        compiler_params=_params(3),
        name="stick_breaking",
    )(p, p, p)


def _dil_kernel(q0_ref, q1_ref, q2_ref, k0_ref, k1_ref, k2_ref, v0_ref, v1_ref, v2_ref,
                cos_ref, sin_ref, gq_ref, gk_ref, o_ref,
                tmp_ref, qd_ref, kd_ref, vd_ref, og0_ref, og1_ref, og2_ref, lse0_ref, lse1_ref, lse2_ref,
                *, seq):
    q_refs, k_refs, v_refs = (q0_ref, q1_ref, q2_ref), (k0_ref, k1_ref, k2_ref), (v0_ref, v1_ref, v2_ref)
    og_refs, lse_refs = (og0_ref, og1_ref, og2_ref), (lse0_ref, lse1_ref, lse2_ref)
    ta = DIL_SPAN
    row = lax.broadcasted_iota(jnp.int32, (ta, ta), 0)
    col = lax.broadcasted_iota(jnp.int32, (ta, ta), 1)

    for gi, (window, dil) in enumerate(DIL_CONFIGS):
        assert window == dil * DIL_SPAN and seq % (dil * ta) == 0
        n_a = seq // dil
        tiles_per_class = n_a // ta

        def class_major(dst_ref, rows, val, dil=dil, n_a=n_a):
            if dil == 1:
                dst_ref[rows, :] = val.astype(BF16)
                return
            tmp_ref[rows, :] = val
            per = (rows.stop - rows.start) // dil
            a0 = rows.start // dil
            for rho in range(dil):
                src = pl.ds(rows.start + rho, per, stride=dil)
                dst_ref[rho * n_a + a0:rho * n_a + a0 + per, :] = tmp_ref[src, :].astype(BF16)

        for r0 in range(0, seq, PREP_ROWS):
            rows = slice(r0, r0 + PREP_ROWS)
            cos, sin = cos_ref[rows, :], sin_ref[rows, :]
            class_major(qd_ref, rows, _rope(_rms(q_refs[gi][rows, :], gq_ref[...]), cos, sin) * SCALE)
            class_major(kd_ref, rows, _rope(_rms(k_refs[gi][rows, :], gk_ref[...]), cos, sin))
            class_major(vd_ref, rows, v_refs[gi][rows, :])

        for j in range(seq // ta):
            rho, at = divmod(j, tiles_per_class)
            rows = slice(j * ta, (j + 1) * ta)
            qj = qd_ref[rows, :]
            scores = [jnp.where(col <= row, _nt(qj, kd_ref[rows, :]), -jnp.inf)]
            values = [vd_ref[rows, :]]
            if at > 0:
                prev = slice((j - 1) * ta, j * ta)
                scores.append(jnp.where(col >= row, _nt(qj, kd_ref[prev, :]), -jnp.inf))
                values.append(vd_ref[prev, :])
            acc, m, l = _softmax_chunks(scores, values)
            tokens = pl.ds(dil * at * ta + rho, ta, stride=dil) if dil > 1 else rows
            og_refs[gi][tokens, :] = acc / l
            lse_refs[gi][tokens, :] = jnp.broadcast_to(m + jnp.log(l), (ta, HEAD_DIM))

    for r0 in range(0, seq, PREP_ROWS):
        rows = slice(r0, r0 + PREP_ROWS)
        lses = [ref[rows, :] for ref in lse_refs]
        top = functools.reduce(jnp.maximum, lses)
        ws = [jnp.exp(x - top) for x in lses]
        total = functools.reduce(lambda a, b: a + b, ws)
        out = functools.reduce(lambda a, b: a + b, [(w / total) * ref[rows, :] for w, ref in zip(ws, og_refs)])
        o_ref[rows, :] = out.astype(o_ref.dtype)


def dilated_attention(p, q_col, k_col, v_col, gq, gk, cos, sin, batch, seq):
    head = lambda c0: pl.BlockSpec((seq, HEAD_DIM), lambda b, h: (b, c0 + h))
    table = pl.BlockSpec((seq, HEAD_DIM), lambda b, h: (b, 0))
    row_spec = pl.BlockSpec((1, HEAD_DIM), lambda b, h: (0, 0))
    groups = range(DIL_GROUPS)
    return pl.pallas_call(
        functools.partial(_dil_kernel, seq=seq),
        grid=(batch, DIL_HEADS),
        in_specs=([head(q_col + gi * DIL_HEADS) for gi in groups]
                  + [head(k_col + gi * DIL_HEADS) for gi in groups]
                  + [head(v_col + gi * DIL_HEADS) for gi in groups]
                  + [table, table, row_spec, row_spec]),
        out_specs=pl.BlockSpec((seq, HEAD_DIM), lambda b, h: (b, h)),
        out_shape=jax.ShapeDtypeStruct((batch * seq, DIL_HEADS * HEAD_DIM), BF16),
        scratch_shapes=[pltpu.VMEM((seq, HEAD_DIM), F32)]
                       + [pltpu.VMEM((seq, HEAD_DIM), BF16)] * 3
                       + [pltpu.VMEM((seq, HEAD_DIM), F32)] * (2 * DIL_GROUPS),
        compiler_params=_params(2),
        name="dilated",
    )(*([p] * 9), cos, sin, _row1(gq), _row1(gk))


def _xattn_kernel(q_ref, kv_ref, gq_ref, gk_ref, o_ref):
    gq, gk = gq_ref[...], gk_ref[...]
    width = XATTN_HEADS * HEAD_DIM
    for h in range(XATTN_HEADS):
        hs = slice(h * HEAD_DIM, (h + 1) * HEAD_DIM)
        q = (_rms(q_ref[:, hs], gq) * SCALE).astype(BF16)
        k = _rms(kv_ref[:, hs], gk).astype(BF16)
        v = kv_ref[:, width + h * HEAD_DIM:width + (h + 1) * HEAD_DIM].astype(BF16)
        s = _nt(q, k)
        e = jnp.exp(s - jnp.max(s, axis=-1, keepdims=True))
        p = e / jnp.sum(e, axis=-1, keepdims=True)
        o_ref[:, hs] = jnp.dot(p.astype(BF16), v, preferred_element_type=F32).astype(o_ref.dtype)


def memory_cross_attention(q, kv, gq, gk, batch, seq, mem_len, tq=512):
    tq = min(tq, seq)
    nq = seq // tq
    width = XATTN_HEADS * HEAD_DIM
    return pl.pallas_call(
        _xattn_kernel,
        grid=(batch, nq),
        in_specs=[pl.BlockSpec((tq, width), lambda b, i: (b * nq + i, 0)),
                  pl.BlockSpec((mem_len, 2 * width), lambda b, i: (b, 0)),
                  pl.BlockSpec((1, HEAD_DIM), lambda b, i: (0, 0)),
                  pl.BlockSpec((1, HEAD_DIM), lambda b, i: (0, 0))],
        out_specs=pl.BlockSpec((tq, width), lambda b, i: (b * nq + i, 0)),
        out_shape=jax.ShapeDtypeStruct((batch * seq, width), BF16),
        compiler_params=_params(2),
        name="xattn",
    )(q, kv, _row1(gq), _row1(gk))


def _even_mixer(x2, h, cos, sin, batch, seq, w_in, w_out, moba_gq, moba_gk, nsa_gq, nsa_gk_cmp,
                nsa_gk_slc, nsa_gk_win, pe_k, pe_v, phi_k, phi_v):
    hd = HEAD_DIM
    main = (3 * MOBA_HEADS + NSA_HEADS + 6 * NSA_GROUPS) * hd
    p = matmul(h, w_in[:, :main].astype(BF16))
    w_gl = w_in[:, main:].reshape(-1, NSA_GROUPS, 3 * NSA_HG)
    w_gl = jnp.pad(w_gl, ((0, 0), (0, 0), (0, hd - 3 * NSA_HG))).reshape(-1, NSA_GROUPS * hd)
    gl = matmul(h, w_gl.astype(BF16))

    o_a = moba_attention(p, 0, 16, 32, moba_gq, moba_gk, cos, sin, batch, seq)
    kcmp, vcmp = nsa_compress(p, 64, 68, pe_k, pe_v, phi_k, phi_v, nsa_gk_cmp, cos, sin, batch, seq)
    o_b = nsa_attention(p, 48, 72, 76, 80, 84, gl, kcmp, vcmp, nsa_gq, nsa_gk_slc, nsa_gk_win,
                        cos, sin, batch, seq)

    split = MOBA_HEADS * hd
    x2 = matmul(o_a, w_out[:split].astype(BF16), residual=x2)
    return matmul(o_b, w_out[split:].astype(BF16), residual=x2)


def _odd_mixer(x2, h, cos, sin, batch, seq, w_in, w_out, dil_gq, dil_gk):
    hd = HEAD_DIM
    p = matmul(h, w_in.astype(BF16))
    o_c = sb_attention(p, 0, SB_HEADS, 2 * SB_HEADS, batch, seq)
    nd = DIL_GROUPS * DIL_HEADS
    o_d = dilated_attention(p, 3 * SB_HEADS, 3 * SB_HEADS + nd, 3 * SB_HEADS + 2 * nd, dil_gq, dil_gk,
                            cos, sin, batch, seq)
    split = SB_HEADS * hd
    x2 = matmul(o_c, w_out[:split].astype(BF16), residual=x2)
    return matmul(o_d, w_out[split:].astype(BF16), residual=x2)


def kernel(x, mem, positions, mix_norm, even_w_in, even_w_out, moba_gq, moba_gk, nsa_gq, nsa_gk_cmp, nsa_gk_slc, nsa_gk_win, nsa_pe_k, nsa_pe_v, nsa_phi_k, nsa_phi_v, odd_w_in, odd_w_out, dil_gq, dil_gk, xattn_norm, mem_norm, xattn_wq, xattn_wkv, xattn_wo, xattn_gq, xattn_gk, ffn_norm, ffn_wg, ffn_wu, ffn_wd):
    batch, seq, d = x.shape
    mem_len = mem.shape[1]
    depth = mix_norm.shape[0]
    x2 = x.reshape(batch * seq, d)
    mem2 = mem.reshape(batch * mem_len, d)
    cos, sin = rope_tables(positions)

    for layer in range(depth):
        h = rmsnorm_rows(x2, mix_norm[layer])
        if layer % 2 == 0:
            e = layer // 2
            x2 = _even_mixer(x2, h, cos, sin, batch, seq, even_w_in[e], even_w_out[e], moba_gq[e],
                             moba_gk[e], nsa_gq[e], nsa_gk_cmp[e], nsa_gk_slc[e], nsa_gk_win[e],
                             nsa_pe_k[e], nsa_pe_v[e], nsa_phi_k[e], nsa_phi_v[e])
        else:
            o = layer // 2
            x2 = _odd_mixer(x2, h, cos, sin, batch, seq, odd_w_in[o], odd_w_out[o], dil_gq[o], dil_gk[o])

        h = rmsnorm_rows(x2, xattn_norm[layer])
        mem_n = rmsnorm_rows(mem2, mem_norm[layer])
        q = matmul(h, xattn_wq[layer].astype(BF16))
        kv = matmul(mem_n, xattn_wkv[layer].astype(BF16))
        o_x = memory_cross_attention(q, kv, xattn_gq[layer], xattn_gk[layer], batch, seq, mem_len)
        x2 = matmul(o_x, xattn_wo[layer].astype(BF16), residual=x2)

        h = rmsnorm_rows(x2, ffn_norm[layer])
        hidden = swiglu_gate_up(h, ffn_wg[layer].astype(BF16), ffn_wu[layer].astype(BF16))
        x2 = matmul(hidden, ffn_wd[layer].astype(BF16), residual=x2)

    return x2.reshape(batch, seq, d)
```

```python
import functools

import jax
import jax.numpy as jnp
from jax import lax
from jax.experimental import pallas as pl
from jax.experimental.pallas import tpu as pltpu

F32 = jnp.float32
BF16 = jnp.bfloat16
HIGHEST = lax.Precision.HIGHEST

HEAD_DIM = 128
HALF = HEAD_DIM // 2
ROPE_THETA = 10000.0
NORM_EPS = 1e-6
TINY = 1e-30
SCALE = HEAD_DIM ** -0.5

MOBA_HEADS = 16
MOBA_BLOCK = 256
MOBA_TOPK = 3
NSA_HEADS = 16
NSA_GROUPS = 4
NSA_HG = NSA_HEADS // NSA_GROUPS
NSA_CMP_LEN = 32
NSA_CMP_STRIDE = 16
NSA_SEL_BLOCK = 64
NSA_SEL_SHIFT = NSA_SEL_BLOCK.bit_length() - 1
NSA_SEL_TOPN = 16
NSA_WINDOW = 512
DIL_CONFIGS = ((128, 1), (512, 4), (2048, 16))
DIL_GROUPS = len(DIL_CONFIGS)
DIL_HEADS = 8
DIL_SPAN = 128
SB_HEADS = 24
XATTN_HEADS = 4

VMEM_LIMIT_BYTES = 56 * 1024 * 1024
NT_DIMS = (((1,), (1,)), ((), ()))
PREP_ROWS = 256


def _params(n_grid):
    return pltpu.CompilerParams(dimension_semantics=("arbitrary",) * n_grid,
                                vmem_limit_bytes=VMEM_LIMIT_BYTES)


def _nt(a, b, precision=None):
    return lax.dot_general(a, b, NT_DIMS, precision=precision, preferred_element_type=F32)


def _rms(x, g):
    return x * lax.rsqrt(jnp.mean(x * x, axis=-1, keepdims=True) + NORM_EPS) * g


def _rope(x, cos, sin_signed):
    return x * cos + pltpu.roll(x, HALF, 1) * sin_signed


def _softmax_chunks(scores, values):
    m = jnp.max(functools.reduce(jnp.maximum, scores), axis=-1, keepdims=True)
    probs = [jnp.exp(s - m) for s in scores]
    l = jnp.sum(functools.reduce(lambda a, b: a + b, probs), axis=-1, keepdims=True)
    acc = None
    for p, v in zip(probs, values):
        part = jnp.dot(p.astype(BF16), v, preferred_element_type=F32)
        acc = part if acc is None else acc + part
    return acc, m, l


def _top_rank(vals):
    idx = lax.broadcasted_iota(jnp.int32, vals.shape, 0)
    rank = jnp.zeros(vals.shape, jnp.int32)
    for m in range(vals.shape[0]):
        c = vals[m:m + 1, :]
        ahead = (c > vals) | ((c == vals) & (idx > m))
        rank = rank + ahead.astype(jnp.int32)
    return rank


def _to_columns(x_t):
    rows = x_t.shape[1]
    eye = (lax.broadcasted_iota(jnp.int32, (rows, rows), 0)
           == lax.broadcasted_iota(jnp.int32, (rows, rows), 1)).astype(BF16)
    return _nt(eye, x_t.astype(BF16))


def _rmsnorm_kernel(x_ref, g_ref, o_ref):
    o_ref[...] = _rms(x_ref[...], g_ref[...]).astype(o_ref.dtype)


def rmsnorm_rows(x, g, tm=256):
    m, d = x.shape
    tm = min(tm, m)
    return pl.pallas_call(
        _rmsnorm_kernel,
        grid=(m // tm,),
        in_specs=[pl.BlockSpec((tm, d), lambda i: (i, 0)), pl.BlockSpec((1, d), lambda i: (0, 0))],
        out_specs=pl.BlockSpec((tm, d), lambda i: (i, 0)),
        out_shape=jax.ShapeDtypeStruct((m, d), BF16),
        compiler_params=_params(1),
        name="rmsnorm",
    )(x, g.reshape(1, d))


def _matmul_kernel(*refs, nk, has_res):
    if has_res:
        a_ref, w_ref, r_ref, o_ref = refs
    else:
        a_ref, w_ref, o_ref = refs
    part = jnp.dot(a_ref[...], w_ref[...].astype(BF16), preferred_element_type=F32)
    if nk == 1:
        if has_res:
            part = part + r_ref[...]
        o_ref[...] = part.astype(o_ref.dtype)
    else:
        k = pl.program_id(2)

        @pl.when(k == 0)
        def _():
            o_ref[...] = (part + r_ref[...]) if has_res else part

        @pl.when(k > 0)
        def _():
            o_ref[...] += part


def _contraction_tile(kdim, limit=6144):
    if kdim <= limit:
        return kdim
    return max(t for t in range(HEAD_DIM, limit + 1, HEAD_DIM) if kdim % t == 0)


def matmul(a, w, layer=None, row0=0, n_cols=None, residual=None, out_dtype=F32, tm=1024, tn=None):
    m, kdim = a.shape
    n = w.shape[-1] if n_cols is None else n_cols
    tk = _contraction_tile(kdim)
    nk = kdim // tk
    if tn is None:
        tn = 512 if tk <= 4096 else 256
    tm, tn = min(tm, m), min(tn, n)
    assert m % tm == 0 and row0 % tk == 0
    assert nk == 1 or out_dtype == F32
    k0 = row0 // tk
    if layer is None:
        w_spec = pl.BlockSpec((tk, tn), lambda i, j, k: (k0 + k, j))
    else:
        w_spec = pl.BlockSpec((None, tk, tn), lambda i, j, k: (layer, k0 + k, j))
    in_specs = [pl.BlockSpec((tm, tk), lambda i, j, k: (i, k)), w_spec]
    args = [a, w]
    if residual is not None:
        in_specs.append(pl.BlockSpec((tm, tn), lambda i, j, k: (i, j)))
        args.append(residual)
    return pl.pallas_call(
        functools.partial(_matmul_kernel, nk=nk, has_res=residual is not None),
        grid=(m // tm, pl.cdiv(n, tn), nk),
        in_specs=in_specs,
        out_specs=pl.BlockSpec((tm, tn), lambda i, j, k: (i, j)),
        out_shape=jax.ShapeDtypeStruct((m, n), out_dtype),
        compiler_params=_params(3),
        name="matmul",
    )(*args)


def _gate_up_kernel(a_ref, wg_ref, wu_ref, o_ref):
    a = a_ref[...]
    g = jnp.dot(a, wg_ref[...].astype(BF16), preferred_element_type=F32)
    u = jnp.dot(a, wu_ref[...].astype(BF16), preferred_element_type=F32)
    o_ref[...] = (g * jax.nn.sigmoid(g) * u).astype(o_ref.dtype)


def swiglu_gate_up(a, wg, wu, layer, tm=1024, tn=256):
    m, kdim = a.shape
    n = wg.shape[-1]
    tm, tn = min(tm, m), min(tn, n)
    w_spec = pl.BlockSpec((None, kdim, tn), lambda i, j: (layer, 0, j))
    return pl.pallas_call(
        _gate_up_kernel,
        grid=(m // tm, pl.cdiv(n, tn)),
        in_specs=[pl.BlockSpec((tm, kdim), lambda i, j: (i, 0)), w_spec, w_spec],
        out_specs=pl.BlockSpec((tm, tn), lambda i, j: (i, j)),
        out_shape=jax.ShapeDtypeStruct((m, n), BF16),
        compiler_params=_params(2),
        name="swiglu_gate_up",
    )(a, wg, wu)


def _rope_table_kernel(pos_ref, invf_ref, cos_ref, sin_ref):
    ang = pos_ref[...] * invf_ref[...]
    lane = lax.broadcasted_iota(jnp.int32, ang.shape, 1)
    s = jnp.sin(ang)
    cos_ref[...] = jnp.cos(ang)
    sin_ref[...] = jnp.where(lane < HALF, -s, s)


def rope_tables(positions, tr=256):
    n = positions.size
    tr = min(tr, n)
    inv_freq = ROPE_THETA ** (-jnp.arange(HALF, dtype=F32) / HALF)
    invf = jnp.concatenate([inv_freq, inv_freq]).reshape(1, HEAD_DIM)
    pos = jnp.broadcast_to(positions.astype(F32).reshape(n, 1), (n, HEAD_DIM))
    spec = pl.BlockSpec((tr, HEAD_DIM), lambda i: (i, 0))
    return pl.pallas_call(
        _rope_table_kernel,
        grid=(n // tr,),
        in_specs=[spec, pl.BlockSpec((1, HEAD_DIM), lambda i: (0, 0))],
        out_specs=[spec, spec],
        out_shape=[jax.ShapeDtypeStruct((n, HEAD_DIM), F32)] * 2,
        compiler_params=_params(1),
        name="rope_tables",
    )(pos, invf)


def _prep_keys(k_ref, g, cos_ref, sin_ref, out_ref, seq):
    for r0 in range(0, seq, PREP_ROWS):
        rows = slice(r0, min(r0 + PREP_ROWS, seq))
        out_ref[rows, :] = _rope(_rms(k_ref[rows, :], g), cos_ref[rows, :], sin_ref[rows, :]).astype(BF16)


def _row1(x):
    return x.reshape(1, HEAD_DIM)


_ROW_SPEC3 = pl.BlockSpec((1, HEAD_DIM), lambda b, h, i: (0, 0))


def _moba_kernel(q_ref, k_ref, v_ref, cos_ref, sin_ref, gq_ref, gk_ref, o_ref,
                 kb_ref, vb_ref, km_ref, *, nblk):
    qi = pl.program_id(2)
    blk = MOBA_BLOCK

    @pl.when(qi == 0)
    def _():
        for n in range(nblk):
            rows = slice(n * blk, (n + 1) * blk)
            k = _rope(_rms(k_ref[rows, :], gk_ref[...]), cos_ref[rows, :], sin_ref[rows, :])
            kb_ref[rows, :] = k.astype(BF16)
            km_ref[n:n + 1, :] = jnp.mean(k, axis=0, keepdims=True)
        vb_ref[...] = v_ref[...].astype(BF16)

    own = pl.ds(pl.multiple_of(qi * blk, blk), blk)
    q = _rope(_rms(q_ref[...], gq_ref[...]), cos_ref[own, :], sin_ref[own, :])
    qb = (q * SCALE).astype(BF16)
    gate = _nt(km_ref[...], q, precision=HIGHEST)
    past = lax.broadcasted_iota(jnp.int32, gate.shape, 0) < qi
    gate = jnp.where(past, gate, -jnp.inf)
    sel = _to_columns((past & (_top_rank(gate) < MOBA_TOPK)).astype(F32))
    row = lax.broadcasted_iota(jnp.int32, (blk, blk), 0)
    col = lax.broadcasted_iota(jnp.int32, (blk, blk), 1)

    def attend(n_past):
        scores = [jnp.where(col <= row, _nt(qb, kb_ref[own, :]), -jnp.inf)]
        values = [vb_ref[own, :]]
        for n in range(n_past):
            rows = slice(n * blk, (n + 1) * blk)
            scores.append(jnp.where(sel[:, n:n + 1] > 0.5, _nt(qb, kb_ref[rows, :]), -jnp.inf))
            values.append(vb_ref[rows, :])
        acc, _, l = _softmax_chunks(scores, values)
        o_ref[...] = (acc / l).astype(o_ref.dtype)

    if nblk % 2 == 0:
        for pair in range(nblk // 2):
            pl.when(qi // 2 == pair)(functools.partial(attend, 2 * pair + 1))
    else:
        attend(nblk - 1)


def moba_attention(p, q_col, k_col, v_col, gq, gk, cos, sin, batch, seq):
    nblk = seq // MOBA_BLOCK
    blk = MOBA_BLOCK
    head = lambda c0: pl.BlockSpec((seq, HEAD_DIM), lambda b, h, i: (b, c0 + h))
    table = pl.BlockSpec((seq, HEAD_DIM), lambda b, h, i: (b, 0))
    return pl.pallas_call(
        functools.partial(_moba_kernel, nblk=nblk),
        grid=(batch, MOBA_HEADS, nblk),
        in_specs=[pl.BlockSpec((blk, HEAD_DIM), lambda b, h, i: (b * nblk + i, q_col + h)),
                  head(k_col), head(v_col), table, table, _ROW_SPEC3, _ROW_SPEC3],
        out_specs=pl.BlockSpec((blk, HEAD_DIM), lambda b, h, i: (b * nblk + i, h)),
        out_shape=jax.ShapeDtypeStruct((batch * seq, MOBA_HEADS * HEAD_DIM), BF16),
        scratch_shapes=[pltpu.VMEM((seq, HEAD_DIM), BF16), pltpu.VMEM((seq, HEAD_DIM), BF16),
                        pltpu.VMEM((nblk, HEAD_DIM), F32)],
        compiler_params=_params(3),
        name="moba",
    )(p, p, p, cos, sin, _row1(gq), _row1(gk))


def _nsa_compress_kernel(kc_ref, vc_ref, pek_ref, pev_ref, phik_ref, phiv_ref, g_ref, cos_ref, sin_ref,
                         ko_ref, vo_ref, *, nslot):
    half = NSA_CMP_LEN // 2

    def compress(x_ref, pe_ref, phi_ref):
        first = jnp.zeros((nslot, HEAD_DIM), F32)
        second = jnp.zeros((nslot, HEAD_DIM), F32)
        for l in range(half):
            xl = x_ref[pl.ds(l, nslot, stride=NSA_CMP_STRIDE), :]
            first += jnp.dot((xl + pe_ref[l:l + 1, :]).astype(BF16), phi_ref[l].astype(BF16),
                             preferred_element_type=F32)
            second += jnp.dot((xl + pe_ref[half + l:half + l + 1, :]).astype(BF16),
                              phi_ref[half + l].astype(BF16), preferred_element_type=F32)
        return pltpu.roll(first, 1, 0) + second

    ends = pl.ds(NSA_CMP_STRIDE - 1, nslot, stride=NSA_CMP_STRIDE)
    kc = compress(kc_ref, pek_ref, phik_ref)
    kc = _rope(_rms(kc, g_ref[...]), cos_ref[ends, :], sin_ref[ends, :])
    ko_ref[0, 0] = kc.astype(BF16)
    vo_ref[0, 0] = compress(vc_ref, pev_ref, phiv_ref).astype(BF16)


def nsa_compress(p, kc_col, vc_col, pe_k, pe_v, phi_k, phi_v, g_kc, cos, sin, batch, seq):
    nslot = seq // NSA_CMP_STRIDE
    full = lambda shape: pl.BlockSpec(shape, lambda b, g: (0,) * len(shape))
    out_spec = pl.BlockSpec((1, 1, nslot, HEAD_DIM), lambda b, g: (b, g, 0, 0))
    out_shape = jax.ShapeDtypeStruct((batch, NSA_GROUPS, nslot, HEAD_DIM), BF16)
    return pl.pallas_call(
        functools.partial(_nsa_compress_kernel, nslot=nslot),
        grid=(batch, NSA_GROUPS),
        in_specs=[pl.BlockSpec((seq, HEAD_DIM), lambda b, g: (b, kc_col + g)),
                  pl.BlockSpec((seq, HEAD_DIM), lambda b, g: (b, vc_col + g)),
                  full((NSA_CMP_LEN, HEAD_DIM)), full((NSA_CMP_LEN, HEAD_DIM)),
                  full((NSA_CMP_LEN, HEAD_DIM, HEAD_DIM)), full((NSA_CMP_LEN, HEAD_DIM, HEAD_DIM)),
                  full((1, HEAD_DIM)),
                  pl.BlockSpec((seq, HEAD_DIM), lambda b, g: (b, 0)),
                  pl.BlockSpec((seq, HEAD_DIM), lambda b, g: (b, 0))],
        out_specs=[out_spec, out_spec],
        out_shape=[out_shape, out_shape],
        compiler_params=_params(2),
        name="nsa_compress",
    )(p, p, pe_k, pe_v, phi_k, phi_v, _row1(g_kc), cos, sin)


NSA_KEY_CHUNK = 256
NSA_WIDTH_STEP = 512


def _nsa_kernel(q_ref, gl_ref, kc_ref, vc_ref, ks_ref, vs_ref, kw_ref, vw_ref, cos_ref, sin_ref,
                gq_ref, gks_ref, gkw_ref, o_ref, ksb_ref, vsb_ref, kwb_ref, vwb_ref, *, tq, seq):
    qi = pl.program_id(2)
    q0 = qi * tq
    hg = NSA_HG
    nsel = seq // NSA_SEL_BLOCK
    nslot = seq // NSA_CMP_STRIDE

    @pl.when(qi == 0)
    def _():
        _prep_keys(ks_ref, gks_ref[...], cos_ref, sin_ref, ksb_ref, seq)
        _prep_keys(kw_ref, gkw_ref[...], cos_ref, sin_ref, kwb_ref, seq)
        vsb_ref[...] = vs_ref[...].astype(BF16)
        vwb_ref[...] = vw_ref[...].astype(BF16)

    own = pl.ds(pl.multiple_of(q0, tq), tq)
    cos, sin, gq = cos_ref[own, :], sin_ref[own, :], gq_ref[...]
    qs = []
    for h in range(hg):
        qh = _rope(_rms(q_ref[:, h * HEAD_DIM:(h + 1) * HEAD_DIM], gq), cos, sin)
        qs.append((qh * SCALE).astype(BF16))
    qst = jnp.concatenate(qs, axis=0)

    def tile_heads(x):
        return jnp.concatenate([x] * hg, axis=0)

    slot = lax.broadcasted_iota(jnp.int32, (tq, nslot), 1)
    tpos_c = q0 + lax.broadcasted_iota(jnp.int32, (tq, nslot), 0)
    valid = (slot >= 1) & (slot * NSA_CMP_STRIDE + (NSA_CMP_STRIDE - 1) <= tpos_c)
    s_c = _nt(qst, kc_ref[0, 0]) + tile_heads(jnp.where(valid, 0.0, -jnp.inf))
    m_c = jnp.max(s_c, axis=-1, keepdims=True)
    m_c = jnp.where(m_c == -jnp.inf, 0.0, m_c)
    e_c = jnp.exp(s_c - m_c)
    p_c = e_c / jnp.maximum(jnp.sum(e_c, axis=-1, keepdims=True), TINY)
    o_c = jnp.dot(p_c.astype(BF16), vc_ref[0, 0], preferred_element_type=F32)

    p_sum = p_c[0:tq]
    for h in range(1, hg):
        p_sum = p_sum + p_c[h * tq:(h + 1) * tq]
    per_sel = NSA_SEL_BLOCK // NSA_CMP_STRIDE
    span = NSA_CMP_LEN // NSA_CMP_STRIDE
    b_idx = lax.broadcasted_iota(jnp.int32, (nsel, nslot), 0)
    j_idx = lax.broadcasted_iota(jnp.int32, (nsel, nslot), 1)
    overlap = ((j_idx >= 1) & (j_idx - 1 > per_sel * b_idx - span) & (j_idx - 1 < per_sel * (b_idx + 1)))
    imp = _nt(overlap.astype(F32), p_sum, precision=HIGHEST)
    blk = lax.broadcasted_iota(jnp.int32, (nsel, tq), 0)
    cur = (q0 + lax.broadcasted_iota(jnp.int32, (nsel, tq), 1)) >> NSA_SEL_SHIFT
    forced = (blk == cur) | (blk == 0)
    imp = jnp.where(forced, jnp.inf, jnp.where(blk <= cur, imp, -jnp.inf))
    sel = _to_columns((_top_rank(imp) < min(NSA_SEL_TOPN, nsel)).astype(F32)).astype(BF16)

    row = lax.broadcasted_iota(jnp.int32, (tq, tq), 0)
    col = lax.broadcasted_iota(jnp.int32, (tq, tq), 1)
    nback = NSA_WINDOW // tq
    scores, values = [], []
    for d in range(nback + 1):
        keys = pl.ds(pl.multiple_of(jnp.maximum(qi - d, 0) * tq, tq), tq)
        s = _nt(qst, kwb_ref[keys, :])
        exists = jnp.where(qi >= d, 0.0, -jnp.inf)
        if d == 0:
            s = s + tile_heads(jnp.where(col <= row, 0.0, -jnp.inf))
        elif d == nback:
            s = s + tile_heads(jnp.where(col > row, exists, -jnp.inf))
        else:
            s = s + exists
        scores.append(s)
        values.append(vwb_ref[keys, :])
    acc_w, _, l_w = _softmax_chunks(scores, values)
    o_w = acc_w / l_w

    gates = jax.nn.sigmoid(gl_ref[...])

    ck = min(NSA_KEY_CHUNK, seq)
    e_row = lax.broadcasted_iota(jnp.int32, (nsel, ck), 0)
    e_col = lax.broadcasted_iota(jnp.int32, (nsel, ck), 1)
    k_col = lax.broadcasted_iota(jnp.int32, (tq, ck), 1)
    t_row = q0 + lax.broadcasted_iota(jnp.int32, (tq, ck), 0)

    def finish(width):
        scores, values = [], []
        for c in range(width // ck):
            expand = (((c * ck + e_col) >> NSA_SEL_SHIFT) == e_row).astype(BF16)
            chosen = jnp.dot(sel, expand, preferred_element_type=F32) > 0.5
            keep = chosen & (c * ck + k_col <= t_row)
            keys = slice(c * ck, (c + 1) * ck)
            scores.append(_nt(qst, ksb_ref[keys, :]) + tile_heads(jnp.where(keep, 0.0, -jnp.inf)))
            values.append(vsb_ref[keys, :])
        acc_s, _, l_s = _softmax_chunks(scores, values)
        o_s = acc_s / l_s
        for h in range(hg):
            hs = slice(h * tq, (h + 1) * tq)
            out = (gates[:, 3 * h:3 * h + 1] * o_c[hs] + gates[:, 3 * h + 1:3 * h + 2] * o_s[hs]
                   + gates[:, 3 * h + 2:3 * h + 3] * o_w[hs])
            o_ref[:, h * HEAD_DIM:(h + 1) * HEAD_DIM] = out.astype(o_ref.dtype)

    step = min(NSA_WIDTH_STEP, seq)
    for w in range(seq // step):
        pl.when((q0 // step) == w)(functools.partial(finish, (w + 1) * step))


def nsa_attention(p, q_col, ks_col, vs_col, kw_col, vw_col, gl, kcmp, vcmp, gq, gks, gkw, cos, sin,
                  batch, seq, tq=128):
    tq = min(tq, seq)
    nq = seq // tq
    gw = NSA_HG * HEAD_DIM
    nslot = seq // NSA_CMP_STRIDE
    head = lambda c0: pl.BlockSpec((seq, HEAD_DIM), lambda b, g, i: (b, c0 + g))
    table = pl.BlockSpec((seq, HEAD_DIM), lambda b, g, i: (b, 0))
    cmp_spec = pl.BlockSpec((1, 1, nslot, HEAD_DIM), lambda b, g, i: (b, g, 0, 0))
    return pl.pallas_call(
        functools.partial(_nsa_kernel, tq=tq, seq=seq),
        grid=(batch, NSA_GROUPS, nq),
        in_specs=[pl.BlockSpec((tq, gw), lambda b, g, i: (b * nq + i, q_col // NSA_HG + g)),
                  pl.BlockSpec((tq, HEAD_DIM), lambda b, g, i: (b * nq + i, g)),
                  cmp_spec, cmp_spec,
                  head(ks_col), head(vs_col), head(kw_col), head(vw_col), table, table,
                  _ROW_SPEC3, _ROW_SPEC3, _ROW_SPEC3],
        out_specs=pl.BlockSpec((tq, gw), lambda b, g, i: (b * nq + i, g)),
        out_shape=jax.ShapeDtypeStruct((batch * seq, NSA_HEADS * HEAD_DIM), BF16),
        scratch_shapes=[pltpu.VMEM((seq, HEAD_DIM), BF16)] * 4,
        compiler_params=_params(3),
        name="nsa",
    )(p, gl, kcmp, vcmp, p, p, p, p, cos, sin, _row1(gq), _row1(gks), _row1(gkw))


def _sb_kernel(q_ref, k_ref, v_ref, o_ref, kb_ref, vb_ref, run_ref, acc_ref, *, tq, tk):
    qi = pl.program_id(2)
    q0 = qi * tq
    per_q = tq // tk

    @pl.when(qi == 0)
    def _():
        kb_ref[...] = k_ref[...].astype(BF16)
        vb_ref[...] = v_ref[...].astype(BF16)

    qb = (q_ref[...] * SCALE).astype(BF16)
    r_idx = lax.broadcasted_iota(jnp.int32, (tk, tk + HEAD_DIM), 0)
    c_idx = lax.broadcasted_iota(jnp.int32, (tk, tk + HEAD_DIM), 1)
    later_ones = ((r_idx > c_idx) | (c_idx >= tk)).astype(BF16)
    row = lax.broadcasted_iota(jnp.int32, (tq, tk), 0)
    col = lax.broadcasted_iota(jnp.int32, (tq, tk), 1)

    def tile(c, on_diagonal):
        keys = pl.ds(pl.multiple_of(c * tk, tk), tk)
        z = _nt(qb, kb_ref[keys, :])
        log_beta = jnp.minimum(z, 0.0) - jnp.log(1.0 + jnp.exp(-jnp.abs(z)))
        log_rest = log_beta - z
        if on_diagonal:
            strict = c * tk + col < q0 + row
            log_rest = jnp.where(strict, log_rest, 0.0)
        hi = log_rest.astype(BF16)
        lo = (log_rest - hi.astype(F32)).astype(BF16)
        sums = (jnp.dot(hi, later_ones, preferred_element_type=F32)
                + jnp.dot(lo, later_ones, preferred_element_type=F32))
        run = run_ref[...]
        a = jnp.exp(log_beta + sums[:, :tk] + jnp.concatenate([run] * (tk // HEAD_DIM), axis=1))
        if on_diagonal:
            a = jnp.where(strict, a, 0.0)
        acc_ref[...] += jnp.dot(a.astype(BF16), vb_ref[keys, :], preferred_element_type=F32)
        run_ref[...] = run + sums[:, tk:]

    run_ref[...] = jnp.zeros_like(run_ref)
    acc_ref[...] = jnp.zeros_like(acc_ref)
    top = (qi + 1) * per_q - 1
    for j in range(per_q):
        tile(top - j, True)

    def step(i, carry):
        for j in range(per_q):
            tile(qi * per_q - 1 - i * per_q - j, False)
        return carry

    lax.fori_loop(0, qi, step, 0)
    o_ref[...] = acc_ref[...].astype(o_ref.dtype)


def sb_attention(p, q_col, k_col, v_col, batch, seq, tq=512, tk=256):
    tq = min(tq, seq)
    tk = min(tk, tq)
    nq = seq // tq
    head = lambda c0: pl.BlockSpec((seq, HEAD_DIM), lambda b, h, i: (b, c0 + h))
    return pl.pallas_call(
        functools.partial(_sb_kernel, tq=tq, tk=tk),
        grid=(batch, SB_HEADS, nq),
        in_specs=[pl.BlockSpec((tq, HEAD_DIM), lambda b, h, i: (b * nq + i, q_col + h)),
                  head(k_col), head(v_col)],
        out_specs=pl.BlockSpec((tq, HEAD_DIM), lambda b, h, i: (b * nq + i, h)),
        out_shape=jax.ShapeDtypeStruct((batch * seq, SB_HEADS * HEAD_DIM), BF16),
        scratch_shapes=[pltpu.VMEM((seq, HEAD_DIM), BF16), pltpu.VMEM((seq, HEAD_DIM), BF16),
                        pltpu.VMEM((tq, HEAD_DIM), F32), pltpu.VMEM((tq, HEAD_DIM), F32)],
        compiler_params=_params(3),
        name="stick_breaking",
    )(p, p, p)


def _dil_kernel(q0_ref, q1_ref, q2_ref, k0_ref, k1_ref, k2_ref, v0_ref, v1_ref, v2_ref,
                cos_ref, sin_ref, gq_ref, gk_ref, o_ref,
                tmp_ref, qd_ref, kd_ref, vd_ref, og0_ref, og1_ref, og2_ref, lse0_ref, lse1_ref, lse2_ref,
                *, seq):
    q_refs, k_refs, v_refs = (q0_ref, q1_ref, q2_ref), (k0_ref, k1_ref, k2_ref), (v0_ref, v1_ref, v2_ref)
    og_refs, lse_refs = (og0_ref, og1_ref, og2_ref), (lse0_ref, lse1_ref, lse2_ref)
    ta = DIL_SPAN
    row = lax.broadcasted_iota(jnp.int32, (ta, ta), 0)
    col = lax.broadcasted_iota(jnp.int32, (ta, ta), 1)

    for gi, (window, dil) in enumerate(DIL_CONFIGS):
        assert window == dil * DIL_SPAN and seq % (dil * ta) == 0
        n_a = seq // dil
        tiles_per_class = n_a // ta

        def class_major(dst_ref, rows, val, dil=dil, n_a=n_a):
            if dil == 1:
                dst_ref[rows, :] = val.astype(BF16)
                return
            tmp_ref[rows, :] = val
            per = (rows.stop - rows.start) // dil
            a0 = rows.start // dil
            for rho in range(dil):
                src = pl.ds(rows.start + rho, per, stride=dil)
                dst_ref[rho * n_a + a0:rho * n_a + a0 + per, :] = tmp_ref[src, :].astype(BF16)

        for r0 in range(0, seq, PREP_ROWS):
            rows = slice(r0, r0 + PREP_ROWS)
            cos, sin = cos_ref[rows, :], sin_ref[rows, :]
            class_major(qd_ref, rows, _rope(_rms(q_refs[gi][rows, :], gq_ref[...]), cos, sin) * SCALE)
            class_major(kd_ref, rows, _rope(_rms(k_refs[gi][rows, :], gk_ref[...]), cos, sin))
            class_major(vd_ref, rows, v_refs[gi][rows, :])

        for j in range(seq // ta):
            rho, at = divmod(j, tiles_per_class)
            rows = slice(j * ta, (j + 1) * ta)
            qj = qd_ref[rows, :]
            scores = [jnp.where(col <= row, _nt(qj, kd_ref[rows, :]), -jnp.inf)]
            values = [vd_ref[rows, :]]
            if at > 0:
                prev = slice((j - 1) * ta, j * ta)
                scores.append(jnp.where(col >= row, _nt(qj, kd_ref[prev, :]), -jnp.inf))
                values.append(vd_ref[prev, :])
            acc, m, l = _softmax_chunks(scores, values)
            tokens = pl.ds(dil * at * ta + rho, ta, stride=dil) if dil > 1 else rows
            og_refs[gi][tokens, :] = acc / l
            lse_refs[gi][tokens, :] = jnp.broadcast_to(m + jnp.log(l), (ta, HEAD_DIM))

    for r0 in range(0, seq, PREP_ROWS):
        rows = slice(r0, r0 + PREP_ROWS)
        lses = [ref[rows, :] for ref in lse_refs]
        top = functools.reduce(jnp.maximum, lses)
        ws = [jnp.exp(x - top) for x in lses]
        total = functools.reduce(lambda a, b: a + b, ws)
        out = functools.reduce(lambda a, b: a + b, [(w / total) * ref[rows, :] for w, ref in zip(ws, og_refs)])
        o_ref[rows, :] = out.astype(o_ref.dtype)


def dilated_attention(p, q_col, k_col, v_col, gq, gk, cos, sin, batch, seq):
    head = lambda c0: pl.BlockSpec((seq, HEAD_DIM), lambda b, h: (b, c0 + h))
    table = pl.BlockSpec((seq, HEAD_DIM), lambda b, h: (b, 0))
    row_spec = pl.BlockSpec((1, HEAD_DIM), lambda b, h: (0, 0))
    groups = range(DIL_GROUPS)
    return pl.pallas_call(
        functools.partial(_dil_kernel, seq=seq),
        grid=(batch, DIL_HEADS),
        in_specs=([head(q_col + gi * DIL_HEADS) for gi in groups]
                  + [head(k_col + gi * DIL_HEADS) for gi in groups]
                  + [head(v_col + gi * DIL_HEADS) for gi in groups]
                  + [table, table, row_spec, row_spec]),
        out_specs=pl.BlockSpec((seq, HEAD_DIM), lambda b, h: (b, h)),
        out_shape=jax.ShapeDtypeStruct((batch * seq, DIL_HEADS * HEAD_DIM), BF16),
        scratch_shapes=[pltpu.VMEM((seq, HEAD_DIM), F32)]
                       + [pltpu.VMEM((seq, HEAD_DIM), BF16)] * 3
                       + [pltpu.VMEM((seq, HEAD_DIM), F32)] * (2 * DIL_GROUPS),
        compiler_params=_params(2),
        name="dilated",
    )(*([p] * 9), cos, sin, _row1(gq), _row1(gk))


def _xattn_kernel(q_ref, kv_ref, gq_ref, gk_ref, o_ref):
    gq, gk = gq_ref[...], gk_ref[...]
    width = XATTN_HEADS * HEAD_DIM
    for h in range(XATTN_HEADS):
        hs = slice(h * HEAD_DIM, (h + 1) * HEAD_DIM)
        q = (_rms(q_ref[:, hs], gq) * SCALE).astype(BF16)
        k = _rms(kv_ref[:, hs], gk).astype(BF16)
        v = kv_ref[:, width + h * HEAD_DIM:width + (h + 1) * HEAD_DIM].astype(BF16)
        s = _nt(q, k)
        e = jnp.exp(s - jnp.max(s, axis=-1, keepdims=True))
        p = e / jnp.sum(e, axis=-1, keepdims=True)
        o_ref[:, hs] = jnp.dot(p.astype(BF16), v, preferred_element_type=F32).astype(o_ref.dtype)


def memory_cross_attention(q, kv, gq, gk, batch, seq, mem_len, tq=512):
    tq = min(tq, seq)
    nq = seq // tq
    width = XATTN_HEADS * HEAD_DIM
    return pl.pallas_call(
        _xattn_kernel,
        grid=(batch, nq),
        in_specs=[pl.BlockSpec((tq, width), lambda b, i: (b * nq + i, 0)),
                  pl.BlockSpec((mem_len, 2 * width), lambda b, i: (b, 0)),
                  pl.BlockSpec((1, HEAD_DIM), lambda b, i: (0, 0)),
                  pl.BlockSpec((1, HEAD_DIM), lambda b, i: (0, 0))],
        out_specs=pl.BlockSpec((tq, width), lambda b, i: (b * nq + i, 0)),
        out_shape=jax.ShapeDtypeStruct((batch * seq, width), BF16),
        compiler_params=_params(2),
        name="xattn",
    )(q, kv, _row1(gq), _row1(gk))


def _even_mixer(x2, h, cos, sin, batch, seq, e, w_in, w_out, moba_gq, moba_gk, nsa_gq, nsa_gk_cmp,
                nsa_gk_slc, nsa_gk_win, pe_k, pe_v, phi_k, phi_v):
    hd = HEAD_DIM
    main = (3 * MOBA_HEADS + NSA_HEADS + 6 * NSA_GROUPS) * hd
    p = matmul(h, w_in, layer=e, n_cols=main)
    w_gl = w_in[e, :, main:].reshape(-1, NSA_GROUPS, 3 * NSA_HG)
    w_gl = jnp.pad(w_gl, ((0, 0), (0, 0), (0, hd - 3 * NSA_HG))).reshape(-1, NSA_GROUPS * hd)
    gl = matmul(h, w_gl)

    o_a = moba_attention(p, 0, 16, 32, moba_gq, moba_gk, cos, sin, batch, seq)
    kcmp, vcmp = nsa_compress(p, 64, 68, pe_k, pe_v, phi_k, phi_v, nsa_gk_cmp, cos, sin, batch, seq)
    o_b = nsa_attention(p, 48, 72, 76, 80, 84, gl, kcmp, vcmp, nsa_gq, nsa_gk_slc, nsa_gk_win,
                        cos, sin, batch, seq)

    x2 = matmul(o_a, w_out, layer=e, residual=x2)
    return matmul(o_b, w_out, layer=e, row0=MOBA_HEADS * hd, residual=x2)


def _odd_mixer(x2, h, cos, sin, batch, seq, o, w_in, w_out, dil_gq, dil_gk):
    hd = HEAD_DIM
    p = matmul(h, w_in, layer=o)
    o_c = sb_attention(p, 0, SB_HEADS, 2 * SB_HEADS, batch, seq)
    nd = DIL_GROUPS * DIL_HEADS
    o_d = dilated_attention(p, 3 * SB_HEADS, 3 * SB_HEADS + nd, 3 * SB_HEADS + 2 * nd, dil_gq, dil_gk,
                            cos, sin, batch, seq)
    x2 = matmul(o_c, w_out, layer=o, residual=x2)
    return matmul(o_d, w_out, layer=o, row0=SB_HEADS * hd, residual=x2)


def kernel(x, mem, positions, mix_norm, even_w_in, even_w_out, moba_gq, moba_gk, nsa_gq, nsa_gk_cmp, nsa_gk_slc, nsa_gk_win, nsa_pe_k, nsa_pe_v, nsa_phi_k, nsa_phi_v, odd_w_in, odd_w_out, dil_gq, dil_gk, xattn_norm, mem_norm, xattn_wq, xattn_wkv, xattn_wo, xattn_gq, xattn_gk, ffn_norm, ffn_wg, ffn_wu, ffn_wd):
    batch, seq, d = x.shape
    mem_len = mem.shape[1]
    depth = mix_norm.shape[0]
    x2 = x.reshape(batch * seq, d)
    mem2 = mem.reshape(batch * mem_len, d)
    cos, sin = rope_tables(positions)

    for layer in range(depth):
        h = rmsnorm_rows(x2, mix_norm[layer])
        if layer % 2 == 0:
            e = layer // 2
            x2 = _even_mixer(x2, h, cos, sin, batch, seq, e, even_w_in, even_w_out, moba_gq[e],
                             moba_gk[e], nsa_gq[e], nsa_gk_cmp[e], nsa_gk_slc[e], nsa_gk_win[e],
                             nsa_pe_k[e], nsa_pe_v[e], nsa_phi_k[e], nsa_phi_v[e])
        else:
            o = layer // 2
            x2 = _odd_mixer(x2, h, cos, sin, batch, seq, o, odd_w_in, odd_w_out, dil_gq[o], dil_gk[o])

        h = rmsnorm_rows(x2, xattn_norm[layer])
        mem_n = rmsnorm_rows(mem2, mem_norm[layer])
        q = matmul(h, xattn_wq, layer=layer)
        kv = matmul(mem_n, xattn_wkv, layer=layer)
        o_x = memory_cross_attention(q, kv, xattn_gq[layer], xattn_gk[layer], batch, seq, mem_len)
        x2 = matmul(o_x, xattn_wo, layer=layer, residual=x2)

        h = rmsnorm_rows(x2, ffn_norm[layer])
        hidden = swiglu_gate_up(h, ffn_wg, ffn_wu, layer)
        x2 = matmul(hidden, ffn_wd, layer=layer, residual=x2)

    return x2.reshape(batch, seq, d)
```

```python
import functools

import jax
import jax.numpy as jnp
from jax import lax
from jax.experimental import pallas as pl
from jax.experimental.pallas import tpu as pltpu

F32 = jnp.float32
BF16 = jnp.bfloat16
HIGHEST = lax.Precision.HIGHEST

HEAD_DIM = 128
HALF = HEAD_DIM // 2
ROPE_THETA = 10000.0
NORM_EPS = 1e-6
TINY = 1e-30
SCALE = HEAD_DIM ** -0.5

MOBA_HEADS = 16
MOBA_BLOCK = 256
MOBA_TOPK = 3
NSA_HEADS = 16
NSA_GROUPS = 4
NSA_HG = NSA_HEADS // NSA_GROUPS
NSA_CMP_LEN = 32
NSA_CMP_STRIDE = 16
NSA_SEL_BLOCK = 64
NSA_SEL_SHIFT = NSA_SEL_BLOCK.bit_length() - 1
NSA_SEL_TOPN = 16
NSA_WINDOW = 512
DIL_CONFIGS = ((128, 1), (512, 4), (2048, 16))
DIL_GROUPS = len(DIL_CONFIGS)
DIL_HEADS = 8
DIL_SPAN = 128
SB_HEADS = 24
XATTN_HEADS = 4

VMEM_LIMIT_BYTES = 56 * 1024 * 1024
NT_DIMS = (((1,), (1,)), ((), ()))
PREP_ROWS = 256


def _params(n_grid):
    return pltpu.CompilerParams(dimension_semantics=("arbitrary",) * n_grid,
                                vmem_limit_bytes=VMEM_LIMIT_BYTES)


def _nt(a, b, precision=None):
    return lax.dot_general(a, b, NT_DIMS, precision=precision, preferred_element_type=F32)


def _rms(x, g):
    return x * lax.rsqrt(jnp.mean(x * x, axis=-1, keepdims=True) + NORM_EPS) * g


def _rope(x, cos, sin_signed):
    return x * cos + pltpu.roll(x, HALF, 1) * sin_signed


def _softmax_chunks(scores, values):
    m = jnp.max(functools.reduce(jnp.maximum, scores), axis=-1, keepdims=True)
    probs = [jnp.exp(s - m) for s in scores]
    l = jnp.sum(functools.reduce(lambda a, b: a + b, probs), axis=-1, keepdims=True)
    acc = None
    for p, v in zip(probs, values):
        part = jnp.dot(p.astype(BF16), v, preferred_element_type=F32)
        acc = part if acc is None else acc + part
    return acc, m, l


def _top_rank(vals):
    idx = lax.broadcasted_iota(jnp.int32, vals.shape, 0)
    rank = jnp.zeros(vals.shape, jnp.int32)
    for m in range(vals.shape[0]):
        c = vals[m:m + 1, :]
        ahead = (c > vals) | ((c == vals) & (idx > m))
        rank = rank + ahead.astype(jnp.int32)
    return rank


def _to_columns(x_t):
    rows = x_t.shape[1]
    eye = (lax.broadcasted_iota(jnp.int32, (rows, rows), 0)
           == lax.broadcasted_iota(jnp.int32, (rows, rows), 1)).astype(BF16)
    return _nt(eye, x_t.astype(BF16))


def _rmsnorm_kernel(x_ref, g_ref, o_ref):
    o_ref[...] = _rms(x_ref[...], g_ref[...]).astype(o_ref.dtype)


def rmsnorm_rows(x, g, tm=256):
    m, d = x.shape
    tm = min(tm, m)
    return pl.pallas_call(
        _rmsnorm_kernel,
        grid=(m // tm,),
        in_specs=[pl.BlockSpec((tm, d), lambda i: (i, 0)), pl.BlockSpec((1, d), lambda i: (0, 0))],
        out_specs=pl.BlockSpec((tm, d), lambda i: (i, 0)),
        out_shape=jax.ShapeDtypeStruct((m, d), BF16),
        compiler_params=_params(1),
        name="rmsnorm",
    )(x, g.reshape(1, d))


def _matmul_kernel(*refs, has_res):
    if has_res:
        a_ref, w_ref, r_ref, o_ref = refs
    else:
        a_ref, w_ref, o_ref = refs
    out = jnp.dot(a_ref[...], w_ref[...].astype(BF16), preferred_element_type=F32)
    if has_res:
        out = out + r_ref[...]
    o_ref[...] = out.astype(o_ref.dtype)


def _contraction_tile(kdim, limit=6144):
    if kdim <= limit:
        return kdim
    return max(t for t in range(HEAD_DIM, limit + 1, HEAD_DIM) if kdim % t == 0)


def _matmul_chunk(a, w, layer, a_col0, k, row0, n, residual, out_dtype, tm):
    m = a.shape[0]
    tn = 512 if k <= 4096 else 256
    tm, tn = min(tm, m), min(tn, n)
    assert m % tm == 0 and a_col0 % k == 0 and row0 % k == 0
    ka, kw = a_col0 // k, row0 // k
    if layer is None:
        w_spec = pl.BlockSpec((k, tn), lambda i, j: (kw, j))
    else:
        w_spec = pl.BlockSpec((None, k, tn), lambda i, j: (layer, kw, j))
    in_specs = [pl.BlockSpec((tm, k), lambda i, j: (i, ka)), w_spec]
    args = [a, w]
    if residual is not None:
        in_specs.append(pl.BlockSpec((tm, tn), lambda i, j: (i, j)))
        args.append(residual)
    return pl.pallas_call(
        functools.partial(_matmul_kernel, has_res=residual is not None),
        grid=(m // tm, pl.cdiv(n, tn)),
        in_specs=in_specs,
        out_specs=pl.BlockSpec((tm, tn), lambda i, j: (i, j)),
        out_shape=jax.ShapeDtypeStruct((m, n), out_dtype),
        compiler_params=_params(2),
        name="matmul",
    )(*args)


def matmul(a, w, layer=None, row0=0, n_cols=None, residual=None, out_dtype=F32, tm=1024):
    kdim = a.shape[1]
    n = w.shape[-1] if n_cols is None else n_cols
    tk = _contraction_tile(kdim)
    out = residual
    for c in range(kdim // tk):
        last = c == kdim // tk - 1
        out = _matmul_chunk(a, w, layer, c * tk, tk, row0 + c * tk, n, out, out_dtype if last else F32, tm)
    return out


def _gate_up_kernel(a_ref, wg_ref, wu_ref, o_ref):
    a = a_ref[...]
    g = jnp.dot(a, wg_ref[...].astype(BF16), preferred_element_type=F32)
    u = jnp.dot(a, wu_ref[...].astype(BF16), preferred_element_type=F32)
    o_ref[...] = (g * jax.nn.sigmoid(g) * u).astype(o_ref.dtype)


def swiglu_gate_up(a, wg, wu, layer, tm=1024, tn=256):
    m, kdim = a.shape
    n = wg.shape[-1]
    tm, tn = min(tm, m), min(tn, n)
    w_spec = pl.BlockSpec((None, kdim, tn), lambda i, j: (layer, 0, j))
    return pl.pallas_call(
        _gate_up_kernel,
        grid=(m // tm, pl.cdiv(n, tn)),
        in_specs=[pl.BlockSpec((tm, kdim), lambda i, j: (i, 0)), w_spec, w_spec],
        out_specs=pl.BlockSpec((tm, tn), lambda i, j: (i, j)),
        out_shape=jax.ShapeDtypeStruct((m, n), BF16),
        compiler_params=_params(2),
        name="swiglu_gate_up",
    )(a, wg, wu)


def _rope_table_kernel(pos_ref, invf_ref, cos_ref, sin_ref):
    ang = pos_ref[...] * invf_ref[...]
    lane = lax.broadcasted_iota(jnp.int32, ang.shape, 1)
    s = jnp.sin(ang)
    cos_ref[...] = jnp.cos(ang)
    sin_ref[...] = jnp.where(lane < HALF, -s, s)


def rope_tables(positions, tr=256):
    n = positions.size
    tr = min(tr, n)
    inv_freq = ROPE_THETA ** (-jnp.arange(HALF, dtype=F32) / HALF)
    invf = jnp.concatenate([inv_freq, inv_freq]).reshape(1, HEAD_DIM)
    pos = jnp.broadcast_to(positions.astype(F32).reshape(n, 1), (n, HEAD_DIM))
    spec = pl.BlockSpec((tr, HEAD_DIM), lambda i: (i, 0))
    return pl.pallas_call(
        _rope_table_kernel,
        grid=(n // tr,),
        in_specs=[spec, pl.BlockSpec((1, HEAD_DIM), lambda i: (0, 0))],
        out_specs=[spec, spec],
        out_shape=[jax.ShapeDtypeStruct((n, HEAD_DIM), F32)] * 2,
        compiler_params=_params(1),
        name="rope_tables",
    )(pos, invf)


def _prep_keys(k_ref, g, cos_ref, sin_ref, out_ref, seq):
    for r0 in range(0, seq, PREP_ROWS):
        rows = slice(r0, min(r0 + PREP_ROWS, seq))
        out_ref[rows, :] = _rope(_rms(k_ref[rows, :], g), cos_ref[rows, :], sin_ref[rows, :]).astype(BF16)


def _row1(x):
    return x.reshape(1, HEAD_DIM)


_ROW_SPEC3 = pl.BlockSpec((1, HEAD_DIM), lambda b, h, i: (0, 0))


def _moba_kernel(q_ref, k_ref, v_ref, cos_ref, sin_ref, gq_ref, gk_ref, o_ref,
                 kb_ref, vb_ref, km_ref, *, nblk):
    qi = pl.program_id(2)
    blk = MOBA_BLOCK

    @pl.when(qi == 0)
    def _():
        for n in range(nblk):
            rows = slice(n * blk, (n + 1) * blk)
            k = _rope(_rms(k_ref[rows, :], gk_ref[...]), cos_ref[rows, :], sin_ref[rows, :])
            kb_ref[rows, :] = k.astype(BF16)
            km_ref[n:n + 1, :] = jnp.mean(k, axis=0, keepdims=True)
        vb_ref[...] = v_ref[...].astype(BF16)

    own = pl.ds(pl.multiple_of(qi * blk, blk), blk)
    q = _rope(_rms(q_ref[...], gq_ref[...]), cos_ref[own, :], sin_ref[own, :])
    qb = (q * SCALE).astype(BF16)
    gate = _nt(km_ref[...], q, precision=HIGHEST)
    past = lax.broadcasted_iota(jnp.int32, gate.shape, 0) < qi
    gate = jnp.where(past, gate, -jnp.inf)
    sel = _to_columns((past & (_top_rank(gate) < MOBA_TOPK)).astype(F32))
    row = lax.broadcasted_iota(jnp.int32, (blk, blk), 0)
    col = lax.broadcasted_iota(jnp.int32, (blk, blk), 1)

    def attend(n_past):
        scores = [jnp.where(col <= row, _nt(qb, kb_ref[own, :]), -jnp.inf)]
        values = [vb_ref[own, :]]
        for n in range(n_past):
            rows = slice(n * blk, (n + 1) * blk)
            scores.append(jnp.where(sel[:, n:n + 1] > 0.5, _nt(qb, kb_ref[rows, :]), -jnp.inf))
            values.append(vb_ref[rows, :])
        acc, _, l = _softmax_chunks(scores, values)
        o_ref[...] = (acc / l).astype(o_ref.dtype)

    if nblk % 2 == 0:
        for pair in range(nblk // 2):
            pl.when(qi // 2 == pair)(functools.partial(attend, 2 * pair + 1))
    else:
        attend(nblk - 1)


def moba_attention(p, q_col, k_col, v_col, gq, gk, cos, sin, batch, seq):
    nblk = seq // MOBA_BLOCK
    blk = MOBA_BLOCK
    head = lambda c0: pl.BlockSpec((seq, HEAD_DIM), lambda b, h, i: (b, c0 + h))
    table = pl.BlockSpec((seq, HEAD_DIM), lambda b, h, i: (b, 0))
    return pl.pallas_call(
        functools.partial(_moba_kernel, nblk=nblk),
        grid=(batch, MOBA_HEADS, nblk),
        in_specs=[pl.BlockSpec((blk, HEAD_DIM), lambda b, h, i: (b * nblk + i, q_col + h)),
                  head(k_col), head(v_col), table, table, _ROW_SPEC3, _ROW_SPEC3],
        out_specs=pl.BlockSpec((blk, HEAD_DIM), lambda b, h, i: (b * nblk + i, h)),
        out_shape=jax.ShapeDtypeStruct((batch * seq, MOBA_HEADS * HEAD_DIM), BF16),
        scratch_shapes=[pltpu.VMEM((seq, HEAD_DIM), BF16), pltpu.VMEM((seq, HEAD_DIM), BF16),
                        pltpu.VMEM((nblk, HEAD_DIM), F32)],
        compiler_params=_params(3),
        name="moba",
    )(p, p, p, cos, sin, _row1(gq), _row1(gk))


def _nsa_compress_kernel(kc_ref, vc_ref, pek_ref, pev_ref, phik_ref, phiv_ref, g_ref, cos_ref, sin_ref,
                         ko_ref, vo_ref, *, nslot):
    half = NSA_CMP_LEN // 2

    def compress(x_ref, pe_ref, phi_ref):
        first = jnp.zeros((nslot, HEAD_DIM), F32)
        second = jnp.zeros((nslot, HEAD_DIM), F32)
        for l in range(half):
            xl = x_ref[pl.ds(l, nslot, stride=NSA_CMP_STRIDE), :]
            first += jnp.dot((xl + pe_ref[l:l + 1, :]).astype(BF16), phi_ref[l].astype(BF16),
                             preferred_element_type=F32)
            second += jnp.dot((xl + pe_ref[half + l:half + l + 1, :]).astype(BF16),
                              phi_ref[half + l].astype(BF16), preferred_element_type=F32)
        return pltpu.roll(first, 1, 0) + second

    ends = pl.ds(NSA_CMP_STRIDE - 1, nslot, stride=NSA_CMP_STRIDE)
    kc = compress(kc_ref, pek_ref, phik_ref)
    kc = _rope(_rms(kc, g_ref[...]), cos_ref[ends, :], sin_ref[ends, :])
    ko_ref[0, 0] = kc.astype(BF16)
    vo_ref[0, 0] = compress(vc_ref, pev_ref, phiv_ref).astype(BF16)


def nsa_compress(p, kc_col, vc_col, pe_k, pe_v, phi_k, phi_v, g_kc, cos, sin, batch, seq):
    nslot = seq // NSA_CMP_STRIDE
    full = lambda shape: pl.BlockSpec(shape, lambda b, g: (0,) * len(shape))
    out_spec = pl.BlockSpec((1, 1, nslot, HEAD_DIM), lambda b, g: (b, g, 0, 0))
    out_shape = jax.ShapeDtypeStruct((batch, NSA_GROUPS, nslot, HEAD_DIM), BF16)
    return pl.pallas_call(
        functools.partial(_nsa_compress_kernel, nslot=nslot),
        grid=(batch, NSA_GROUPS),
        in_specs=[pl.BlockSpec((seq, HEAD_DIM), lambda b, g: (b, kc_col + g)),
                  pl.BlockSpec((seq, HEAD_DIM), lambda b, g: (b, vc_col + g)),
                  full((NSA_CMP_LEN, HEAD_DIM)), full((NSA_CMP_LEN, HEAD_DIM)),
                  full((NSA_CMP_LEN, HEAD_DIM, HEAD_DIM)), full((NSA_CMP_LEN, HEAD_DIM, HEAD_DIM)),
                  full((1, HEAD_DIM)),
                  pl.BlockSpec((seq, HEAD_DIM), lambda b, g: (b, 0)),
                  pl.BlockSpec((seq, HEAD_DIM), lambda b, g: (b, 0))],
        out_specs=[out_spec, out_spec],
        out_shape=[out_shape, out_shape],
        compiler_params=_params(2),
        name="nsa_compress",
    )(p, p, pe_k, pe_v, phi_k, phi_v, _row1(g_kc), cos, sin)


NSA_KEY_CHUNK = 256
NSA_WIDTH_STEP = 512


def _nsa_kernel(q_ref, gl_ref, kc_ref, vc_ref, ks_ref, vs_ref, kw_ref, vw_ref, cos_ref, sin_ref,
                gq_ref, gks_ref, gkw_ref, o_ref, ksb_ref, vsb_ref, kwb_ref, vwb_ref, *, tq, seq):
    qi = pl.program_id(2)
    q0 = qi * tq
    hg = NSA_HG
    nsel = seq // NSA_SEL_BLOCK
    nslot = seq // NSA_CMP_STRIDE

    @pl.when(qi == 0)
    def _():
        _prep_keys(ks_ref, gks_ref[...], cos_ref, sin_ref, ksb_ref, seq)
        _prep_keys(kw_ref, gkw_ref[...], cos_ref, sin_ref, kwb_ref, seq)
        vsb_ref[...] = vs_ref[...].astype(BF16)
        vwb_ref[...] = vw_ref[...].astype(BF16)

    own = pl.ds(pl.multiple_of(q0, tq), tq)
    cos, sin, gq = cos_ref[own, :], sin_ref[own, :], gq_ref[...]
    qs = []
    for h in range(hg):
        qh = _rope(_rms(q_ref[:, h * HEAD_DIM:(h + 1) * HEAD_DIM], gq), cos, sin)
        qs.append((qh * SCALE).astype(BF16))
    qst = jnp.concatenate(qs, axis=0)

    def tile_heads(x):
        return jnp.concatenate([x] * hg, axis=0)

    slot = lax.broadcasted_iota(jnp.int32, (tq, nslot), 1)
    tpos_c = q0 + lax.broadcasted_iota(jnp.int32, (tq, nslot), 0)
    valid = (slot >= 1) & (slot * NSA_CMP_STRIDE + (NSA_CMP_STRIDE - 1) <= tpos_c)
    s_c = _nt(qst, kc_ref[0, 0]) + tile_heads(jnp.where(valid, 0.0, -jnp.inf))
    m_c = jnp.max(s_c, axis=-1, keepdims=True)
    m_c = jnp.where(m_c == -jnp.inf, 0.0, m_c)
    e_c = jnp.exp(s_c - m_c)
    p_c = e_c / jnp.maximum(jnp.sum(e_c, axis=-1, keepdims=True), TINY)
    o_c = jnp.dot(p_c.astype(BF16), vc_ref[0, 0], preferred_element_type=F32)

    p_sum = p_c[0:tq]
    for h in range(1, hg):
        p_sum = p_sum + p_c[h * tq:(h + 1) * tq]
    per_sel = NSA_SEL_BLOCK // NSA_CMP_STRIDE
    span = NSA_CMP_LEN // NSA_CMP_STRIDE
    b_idx = lax.broadcasted_iota(jnp.int32, (nsel, nslot), 0)
    j_idx = lax.broadcasted_iota(jnp.int32, (nsel, nslot), 1)
    overlap = ((j_idx >= 1) & (j_idx - 1 > per_sel * b_idx - span) & (j_idx - 1 < per_sel * (b_idx + 1)))
    imp = _nt(overlap.astype(F32), p_sum, precision=HIGHEST)
    blk = lax.broadcasted_iota(jnp.int32, (nsel, tq), 0)
    cur = (q0 + lax.broadcasted_iota(jnp.int32, (nsel, tq), 1)) >> NSA_SEL_SHIFT
    forced = (blk == cur) | (blk == 0)
    imp = jnp.where(forced, jnp.inf, jnp.where(blk <= cur, imp, -jnp.inf))
    sel = _to_columns((_top_rank(imp) < min(NSA_SEL_TOPN, nsel)).astype(F32)).astype(BF16)

    row = lax.broadcasted_iota(jnp.int32, (tq, tq), 0)
    col = lax.broadcasted_iota(jnp.int32, (tq, tq), 1)
    nback = NSA_WINDOW // tq
    scores, values = [], []
    for d in range(nback + 1):
        keys = pl.ds(pl.multiple_of(jnp.maximum(qi - d, 0) * tq, tq), tq)
        s = _nt(qst, kwb_ref[keys, :])
        exists = jnp.where(qi >= d, 0.0, -jnp.inf)
        if d == 0:
            s = s + tile_heads(jnp.where(col <= row, 0.0, -jnp.inf))
        elif d == nback:
            s = s + tile_heads(jnp.where(col > row, exists, -jnp.inf))
        else:
            s = s + exists
        scores.append(s)
        values.append(vwb_ref[keys, :])
    acc_w, _, l_w = _softmax_chunks(scores, values)
    o_w = acc_w / l_w

    per_group = 3 * hg
    shift = (HEAD_DIM - per_group * pl.program_id(1)) % HEAD_DIM
    gates = jax.nn.sigmoid(pltpu.roll(gl_ref[...], shift, 1))

    ck = min(NSA_KEY_CHUNK, seq)
    e_row = lax.broadcasted_iota(jnp.int32, (nsel, ck), 0)
    e_col = lax.broadcasted_iota(jnp.int32, (nsel, ck), 1)
    k_col = lax.broadcasted_iota(jnp.int32, (tq, ck), 1)
    t_row = q0 + lax.broadcasted_iota(jnp.int32, (tq, ck), 0)

    def finish(width):
        scores, values = [], []
        for c in range(width // ck):
            expand = (((c * ck + e_col) >> NSA_SEL_SHIFT) == e_row).astype(BF16)
            chosen = jnp.dot(sel, expand, preferred_element_type=F32) > 0.5
            keep = chosen & (c * ck + k_col <= t_row)
            keys = slice(c * ck, (c + 1) * ck)
            scores.append(_nt(qst, ksb_ref[keys, :]) + tile_heads(jnp.where(keep, 0.0, -jnp.inf)))
            values.append(vsb_ref[keys, :])
        acc_s, _, l_s = _softmax_chunks(scores, values)
        o_s = acc_s / l_s
        for h in range(hg):
            hs = slice(h * tq, (h + 1) * tq)
            out = (gates[:, 3 * h:3 * h + 1] * o_c[hs] + gates[:, 3 * h + 1:3 * h + 2] * o_s[hs]
                   + gates[:, 3 * h + 2:3 * h + 3] * o_w[hs])
            o_ref[:, h * HEAD_DIM:(h + 1) * HEAD_DIM] = out.astype(o_ref.dtype)

    step = min(NSA_WIDTH_STEP, seq)
    for w in range(seq // step):
        pl.when((q0 // step) == w)(functools.partial(finish, (w + 1) * step))


def nsa_attention(p, q_col, ks_col, vs_col, kw_col, vw_col, gl_col, kcmp, vcmp, gq, gks, gkw, cos, sin,
                  batch, seq, tq=128):
    tq = min(tq, seq)
    nq = seq // tq
    gw = NSA_HG * HEAD_DIM
    nslot = seq // NSA_CMP_STRIDE
    head = lambda c0: pl.BlockSpec((seq, HEAD_DIM), lambda b, g, i: (b, c0 + g))
    table = pl.BlockSpec((seq, HEAD_DIM), lambda b, g, i: (b, 0))
    cmp_spec = pl.BlockSpec((1, 1, nslot, HEAD_DIM), lambda b, g, i: (b, g, 0, 0))
    return pl.pallas_call(
        functools.partial(_nsa_kernel, tq=tq, seq=seq),
        grid=(batch, NSA_GROUPS, nq),
        in_specs=[pl.BlockSpec((tq, gw), lambda b, g, i: (b * nq + i, q_col // NSA_HG + g)),
                  pl.BlockSpec((tq, HEAD_DIM), lambda b, g, i: (b * nq + i, gl_col)),
                  cmp_spec, cmp_spec,
                  head(ks_col), head(vs_col), head(kw_col), head(vw_col), table, table,
                  _ROW_SPEC3, _ROW_SPEC3, _ROW_SPEC3],
        out_specs=pl.BlockSpec((tq, gw), lambda b, g, i: (b * nq + i, g)),
        out_shape=jax.ShapeDtypeStruct((batch * seq, NSA_HEADS * HEAD_DIM), BF16),
        scratch_shapes=[pltpu.VMEM((seq, HEAD_DIM), BF16)] * 4,
        compiler_params=_params(3),
        name="nsa",
    )(p, p, kcmp, vcmp, p, p, p, p, cos, sin, _row1(gq), _row1(gks), _row1(gkw))


def _sb_kernel(q_ref, k_ref, v_ref, o_ref, kb_ref, vb_ref, run_ref, acc_ref, *, tq, tk):
    qi = pl.program_id(2)
    per_q = tq // tk

    @pl.when(qi == 0)
    def _():
        kb_ref[...] = k_ref[...].astype(BF16)
        vb_ref[...] = v_ref[...].astype(BF16)

    later = (lax.broadcasted_iota(jnp.int32, (tk, tk), 0)
             > lax.broadcasted_iota(jnp.int32, (tk, tk), 1)).astype(BF16)
    row = lax.broadcasted_iota(jnp.int32, (tq, tk), 0)
    col = lax.broadcasted_iota(jnp.int32, (tq, tk), 1)
    qb = (q_ref[...] * SCALE).astype(BF16)

    def tiles(first, key0s):
        keys = [pl.ds(pl.multiple_of((first - j) * tk, tk), tk) for j in range(len(key0s))]
        zs = [_nt(qb, kb_ref[kk, :]) for kk in keys]
        log_betas, log_rests, stricts = [], [], []
        for z, key0 in zip(zs, key0s):
            log_beta = jnp.minimum(z, 0.0) - jnp.log(1.0 + jnp.exp(-jnp.abs(z)))
            log_rest = log_beta - z
            strict = None
            if key0 is not None:
                strict = key0 + col < row
                log_rest = jnp.where(strict, log_rest, 0.0)
            log_betas.append(log_beta)
            log_rests.append(log_rest)
            stricts.append(strict)
        insides = []
        for log_rest in log_rests:
            hi = log_rest.astype(BF16)
            lo = (log_rest - hi.astype(F32)).astype(BF16)
            both = jnp.dot(jnp.concatenate([hi, lo], axis=0), later, preferred_element_type=F32)
            insides.append(both[:tq] + both[tq:])
        run = run_ref[...]
        acc = acc_ref[...]
        for kk, log_beta, log_rest, strict, inside in zip(keys, log_betas, log_rests, stricts, insides):
            a = jnp.exp(log_beta + inside + jnp.concatenate([run] * (tk // HEAD_DIM), axis=1))
            if strict is not None:
                a = jnp.where(strict, a, 0.0)
            acc = acc + jnp.dot(a.astype(BF16), vb_ref[kk, :], preferred_element_type=F32)
            run = run + jnp.sum(log_rest, axis=-1, keepdims=True)
        run_ref[...] = run
        acc_ref[...] = acc

    run_ref[...] = jnp.zeros_like(run_ref)
    acc_ref[...] = jnp.zeros_like(acc_ref)
    tiles((qi + 1) * per_q - 1, [(per_q - 1 - j) * tk for j in range(per_q)])

    def step(i, carry):
        tiles((qi - i) * per_q - 1, [None] * per_q)
        return carry

    lax.fori_loop(0, qi, step, 0)
    o_ref[...] = acc_ref[...].astype(o_ref.dtype)


def sb_attention(p, q_col, k_col, v_col, batch, seq, tq=512, tk=256):
    tq = min(tq, seq)
    tk = min(tk, tq)
    nq = seq // tq
    head = lambda c0: pl.BlockSpec((seq, HEAD_DIM), lambda b, h, i: (b, c0 + h))
    return pl.pallas_call(
        functools.partial(_sb_kernel, tq=tq, tk=tk),
        grid=(batch, SB_HEADS, nq),
        in_specs=[pl.BlockSpec((tq, HEAD_DIM), lambda b, h, i: (b * nq + i, q_col + h)),
                  head(k_col), head(v_col)],
        out_specs=pl.BlockSpec((tq, HEAD_DIM), lambda b, h, i: (b * nq + i, h)),
        out_shape=jax.ShapeDtypeStruct((batch * seq, SB_HEADS * HEAD_DIM), BF16),
        scratch_shapes=[pltpu.VMEM((seq, HEAD_DIM), BF16), pltpu.VMEM((seq, HEAD_DIM), BF16),
                        pltpu.VMEM((tq, HEAD_DIM), F32), pltpu.VMEM((tq, HEAD_DIM), F32)],
        compiler_params=_params(3),
        name="stick_breaking",
    )(p, p, p)


def _dil_kernel(q0_ref, q1_ref, q2_ref, k0_ref, k1_ref, k2_ref, v0_ref, v1_ref, v2_ref,
                cos_ref, sin_ref, gq_ref, gk_ref, o_ref,
                tmp_ref, qd_ref, kd_ref, vd_ref, og0_ref, og1_ref, og2_ref, lse0_ref, lse1_ref, lse2_ref,
                *, seq):
    q_refs, k_refs, v_refs = (q0_ref, q1_ref, q2_ref), (k0_ref, k1_ref, k2_ref), (v0_ref, v1_ref, v2_ref)
    og_refs, lse_refs = (og0_ref, og1_ref, og2_ref), (lse0_ref, lse1_ref, lse2_ref)
    ta = DIL_SPAN
    row = lax.broadcasted_iota(jnp.int32, (ta, ta), 0)
    col = lax.broadcasted_iota(jnp.int32, (ta, ta), 1)

    for gi, (window, dil) in enumerate(DIL_CONFIGS):
        assert window == dil * DIL_SPAN and seq % (dil * ta) == 0
        n_a = seq // dil
        tiles_per_class = n_a // ta

        def class_major(dst_ref, rows, val, dil=dil, n_a=n_a):
            if dil == 1:
                dst_ref[rows, :] = val.astype(BF16)
                return
            tmp_ref[rows, :] = val
            per = (rows.stop - rows.start) // dil
            a0 = rows.start // dil
            for rho in range(dil):
                src = pl.ds(rows.start + rho, per, stride=dil)
                dst_ref[rho * n_a + a0:rho * n_a + a0 + per, :] = tmp_ref[src, :].astype(BF16)

        for r0 in range(0, seq, PREP_ROWS):
            rows = slice(r0, r0 + PREP_ROWS)
            cos, sin = cos_ref[rows, :], sin_ref[rows, :]
            class_major(qd_ref, rows, _rope(_rms(q_refs[gi][rows, :], gq_ref[...]), cos, sin) * SCALE)
            class_major(kd_ref, rows, _rope(_rms(k_refs[gi][rows, :], gk_ref[...]), cos, sin))
            class_major(vd_ref, rows, v_refs[gi][rows, :])

        for j in range(seq // ta):
            rho, at = divmod(j, tiles_per_class)
            rows = slice(j * ta, (j + 1) * ta)
            qj = qd_ref[rows, :]
            scores = [jnp.where(col <= row, _nt(qj, kd_ref[rows, :]), -jnp.inf)]
            values = [vd_ref[rows, :]]
            if at > 0:
                prev = slice((j - 1) * ta, j * ta)
                scores.append(jnp.where(col >= row, _nt(qj, kd_ref[prev, :]), -jnp.inf))
                values.append(vd_ref[prev, :])
            acc, m, l = _softmax_chunks(scores, values)
            tokens = pl.ds(dil * at * ta + rho, ta, stride=dil) if dil > 1 else rows
            og_refs[gi][tokens, :] = acc / l
            lse_refs[gi][tokens, :] = jnp.broadcast_to(m + jnp.log(l), (ta, HEAD_DIM))

    for r0 in range(0, seq, PREP_ROWS):
        rows = slice(r0, r0 + PREP_ROWS)
        lses = [ref[rows, :] for ref in lse_refs]
        top = functools.reduce(jnp.maximum, lses)
        ws = [jnp.exp(x - top) for x in lses]
        total = functools.reduce(lambda a, b: a + b, ws)
        out = functools.reduce(lambda a, b: a + b, [(w / total) * ref[rows, :] for w, ref in zip(ws, og_refs)])
        o_ref[rows, :] = out.astype(o_ref.dtype)


def dilated_attention(p, q_col, k_col, v_col, gq, gk, cos, sin, batch, seq):
    head = lambda c0: pl.BlockSpec((seq, HEAD_DIM), lambda b, h: (b, c0 + h))
    table = pl.BlockSpec((seq, HEAD_DIM), lambda b, h: (b, 0))
    row_spec = pl.BlockSpec((1, HEAD_DIM), lambda b, h: (0, 0))
    groups = range(DIL_GROUPS)
    return pl.pallas_call(
        functools.partial(_dil_kernel, seq=seq),
        grid=(batch, DIL_HEADS),
        in_specs=([head(q_col + gi * DIL_HEADS) for gi in groups]
                  + [head(k_col + gi * DIL_HEADS) for gi in groups]
                  + [head(v_col + gi * DIL_HEADS) for gi in groups]
                  + [table, table, row_spec, row_spec]),
        out_specs=pl.BlockSpec((seq, HEAD_DIM), lambda b, h: (b, h)),
        out_shape=jax.ShapeDtypeStruct((batch * seq, DIL_HEADS * HEAD_DIM), BF16),
        scratch_shapes=[pltpu.VMEM((seq, HEAD_DIM), F32)]
                       + [pltpu.VMEM((seq, HEAD_DIM), BF16)] * 3
                       + [pltpu.VMEM((seq, HEAD_DIM), F32)] * (2 * DIL_GROUPS),
        compiler_params=_params(2),
        name="dilated",
    )(*([p] * 9), cos, sin, _row1(gq), _row1(gk))


def _xattn_kernel(q_ref, kv_ref, gq_ref, gk_ref, o_ref):
    gq, gk = gq_ref[...], gk_ref[...]
    width = XATTN_HEADS * HEAD_DIM
    for h in range(XATTN_HEADS):
        hs = slice(h * HEAD_DIM, (h + 1) * HEAD_DIM)
        q = (_rms(q_ref[:, hs], gq) * SCALE).astype(BF16)
        k = _rms(kv_ref[:, hs], gk).astype(BF16)
        v = kv_ref[:, width + h * HEAD_DIM:width + (h + 1) * HEAD_DIM].astype(BF16)
        s = _nt(q, k)
        e = jnp.exp(s - jnp.max(s, axis=-1, keepdims=True))
        p = e / jnp.sum(e, axis=-1, keepdims=True)
        o_ref[:, hs] = jnp.dot(p.astype(BF16), v, preferred_element_type=F32).astype(o_ref.dtype)


def memory_cross_attention(q, kv, gq, gk, batch, seq, mem_len, tq=512):
    tq = min(tq, seq)
    nq = seq // tq
    width = XATTN_HEADS * HEAD_DIM
    return pl.pallas_call(
        _xattn_kernel,
        grid=(batch, nq),
        in_specs=[pl.BlockSpec((tq, width), lambda b, i: (b * nq + i, 0)),
                  pl.BlockSpec((mem_len, 2 * width), lambda b, i: (b, 0)),
                  pl.BlockSpec((1, HEAD_DIM), lambda b, i: (0, 0)),
                  pl.BlockSpec((1, HEAD_DIM), lambda b, i: (0, 0))],
        out_specs=pl.BlockSpec((tq, width), lambda b, i: (b * nq + i, 0)),
        out_shape=jax.ShapeDtypeStruct((batch * seq, width), BF16),
        compiler_params=_params(2),
        name="xattn",
    )(q, kv, _row1(gq), _row1(gk))


def _even_mixer(x2, h, cos, sin, batch, seq, e, w_in, w_out, moba_gq, moba_gk, nsa_gq, nsa_gk_cmp,
                nsa_gk_slc, nsa_gk_win, pe_k, pe_v, phi_k, phi_v):
    hd = HEAD_DIM
    main = (3 * MOBA_HEADS + NSA_HEADS + 6 * NSA_GROUPS) * hd
    p = matmul(h, w_in, layer=e)

    o_a = moba_attention(p, 0, 16, 32, moba_gq, moba_gk, cos, sin, batch, seq)
    kcmp, vcmp = nsa_compress(p, 64, 68, pe_k, pe_v, phi_k, phi_v, nsa_gk_cmp, cos, sin, batch, seq)
    o_b = nsa_attention(p, 48, 72, 76, 80, 84, main // hd, kcmp, vcmp, nsa_gq, nsa_gk_slc, nsa_gk_win,
                        cos, sin, batch, seq)

    x2 = matmul(o_a, w_out, layer=e, residual=x2)
    return matmul(o_b, w_out, layer=e, row0=MOBA_HEADS * hd, residual=x2)


def _odd_mixer(x2, h, cos, sin, batch, seq, o, w_in, w_out, dil_gq, dil_gk):
    hd = HEAD_DIM
    p = matmul(h, w_in, layer=o)
    o_c = sb_attention(p, 0, SB_HEADS, 2 * SB_HEADS, batch, seq)
    nd = DIL_GROUPS * DIL_HEADS
    o_d = dilated_attention(p, 3 * SB_HEADS, 3 * SB_HEADS + nd, 3 * SB_HEADS + 2 * nd, dil_gq, dil_gk,
                            cos, sin, batch, seq)
    x2 = matmul(o_c, w_out, layer=o, residual=x2)
    return matmul(o_d, w_out, layer=o, row0=SB_HEADS * hd, residual=x2)


def kernel(x, mem, positions, mix_norm, even_w_in, even_w_out, moba_gq, moba_gk, nsa_gq, nsa_gk_cmp, nsa_gk_slc, nsa_gk_win, nsa_pe_k, nsa_pe_v, nsa_phi_k, nsa_phi_v, odd_w_in, odd_w_out, dil_gq, dil_gk, xattn_norm, mem_norm, xattn_wq, xattn_wkv, xattn_wo, xattn_gq, xattn_gk, ffn_norm, ffn_wg, ffn_wu, ffn_wd):
    batch, seq, d = x.shape
    mem_len = mem.shape[1]
    depth = mix_norm.shape[0]
    x2 = x.reshape(batch * seq, d)
    mem2 = mem.reshape(batch * mem_len, d)
    cos, sin = rope_tables(positions)

    for layer in range(depth):
        h = rmsnorm_rows(x2, mix_norm[layer])
        if layer % 2 == 0:
            e = layer // 2
            x2 = _even_mixer(x2, h, cos, sin, batch, seq, e, even_w_in, even_w_out, moba_gq[e],
                             moba_gk[e], nsa_gq[e], nsa_gk_cmp[e], nsa_gk_slc[e], nsa_gk_win[e],
                             nsa_pe_k[e], nsa_pe_v[e], nsa_phi_k[e], nsa_phi_v[e])
        else:
            o = layer // 2
            x2 = _odd_mixer(x2, h, cos, sin, batch, seq, o, odd_w_in, odd_w_out, dil_gq[o], dil_gk[o])

        h = rmsnorm_rows(x2, xattn_norm[layer])
        mem_n = rmsnorm_rows(mem2, mem_norm[layer])
        q = matmul(h, xattn_wq, layer=layer)
        kv = matmul(mem_n, xattn_wkv, layer=layer)
        o_x = memory_cross_attention(q, kv, xattn_gq[layer], xattn_gk[layer], batch, seq, mem_len)
        x2 = matmul(o_x, xattn_wo, layer=layer, residual=x2)

        h = rmsnorm_rows(x2, ffn_norm[layer])
        hidden = swiglu_gate_up(h, ffn_wg, ffn_wu, layer)
        x2 = matmul(hidden, ffn_wd, layer=layer, residual=x2)

    return x2.reshape(batch, seq, d)
```

```python
import functools

import jax
import jax.numpy as jnp
from jax import lax
from jax.experimental import pallas as pl
from jax.experimental.pallas import tpu as pltpu

F32 = jnp.float32
BF16 = jnp.bfloat16
HIGHEST = lax.Precision.HIGHEST

HEAD_DIM = 128
HALF = HEAD_DIM // 2
ROPE_THETA = 10000.0
NORM_EPS = 1e-6
TINY = 1e-30
SCALE = HEAD_DIM ** -0.5

MOBA_HEADS = 16
MOBA_BLOCK = 256
MOBA_TOPK = 3
NSA_HEADS = 16
NSA_GROUPS = 4
NSA_HG = NSA_HEADS // NSA_GROUPS
NSA_CMP_LEN = 32
NSA_CMP_STRIDE = 16
NSA_SEL_BLOCK = 64
NSA_SEL_SHIFT = NSA_SEL_BLOCK.bit_length() - 1
NSA_SEL_TOPN = 16
NSA_WINDOW = 512
DIL_CONFIGS = ((128, 1), (512, 4), (2048, 16))
DIL_GROUPS = len(DIL_CONFIGS)
DIL_HEADS = 8
DIL_SPAN = 128
SB_HEADS = 24
XATTN_HEADS = 4

VMEM_LIMIT_BYTES = 56 * 1024 * 1024
NT_DIMS = (((1,), (1,)), ((), ()))
PREP_ROWS = 256
MATMUL_ROW_TILE = 1024
BIG_ROW_TILE = 2048


def _params(n_grid):
    return pltpu.CompilerParams(dimension_semantics=("arbitrary",) * n_grid,
                                vmem_limit_bytes=VMEM_LIMIT_BYTES)


def _nt(a, b, precision=None):
    return lax.dot_general(a, b, NT_DIMS, precision=precision, preferred_element_type=F32)


def _rms(x, g):
    return x * lax.rsqrt(jnp.mean(x * x, axis=-1, keepdims=True) + NORM_EPS) * g


def _rope(x, cos, sin_signed):
    return x * cos + pltpu.roll(x, HALF, 1) * sin_signed


def _softmax_chunks(scores, values):
    m = jnp.max(functools.reduce(jnp.maximum, scores), axis=-1, keepdims=True)
    probs = [jnp.exp(s - m) for s in scores]
    l = jnp.sum(functools.reduce(lambda a, b: a + b, probs), axis=-1, keepdims=True)
    acc = None
    for p, v in zip(probs, values):
        part = jnp.dot(p.astype(BF16), v, preferred_element_type=F32)
        acc = part if acc is None else acc + part
    return acc, m, l


def _top_rank(vals):
    idx = lax.broadcasted_iota(jnp.int32, vals.shape, 0)
    rank = jnp.zeros(vals.shape, jnp.int32)
    for m in range(vals.shape[0]):
        c = vals[m:m + 1, :]
        ahead = (c > vals) | ((c == vals) & (idx > m))
        rank = rank + ahead.astype(jnp.int32)
    return rank


def _to_columns(x_t):
    rows = x_t.shape[1]
    eye = (lax.broadcasted_iota(jnp.int32, (rows, rows), 0)
           == lax.broadcasted_iota(jnp.int32, (rows, rows), 1)).astype(BF16)
    return _nt(eye, x_t.astype(BF16))


def _rmsnorm_kernel(x_ref, g_ref, o_ref):
    o_ref[...] = _rms(x_ref[...], g_ref[...]).astype(o_ref.dtype)


def rmsnorm_rows(x, g, tm=256):
    m, d = x.shape
    tm = min(tm, m)
    return pl.pallas_call(
        _rmsnorm_kernel,
        grid=(m // tm,),
        in_specs=[pl.BlockSpec((tm, d), lambda i: (i, 0)), pl.BlockSpec((1, d), lambda i: (0, 0))],
        out_specs=pl.BlockSpec((tm, d), lambda i: (i, 0)),
        out_shape=jax.ShapeDtypeStruct((m, d), BF16),
        compiler_params=_params(1),
        name="rmsnorm",
    )(x, g.reshape(1, d))


def _matmul_kernel(*refs, n_lhs, has_res):
    a_refs, w_refs = refs[:n_lhs], refs[n_lhs:2 * n_lhs]
    out = None
    for a_ref, w_ref in zip(a_refs, w_refs):
        part = jnp.dot(a_ref[...], w_ref[...].astype(BF16), preferred_element_type=F32)
        out = part if out is None else out + part
    if has_res:
        out = out + refs[2 * n_lhs][...]
    refs[-1][...] = out.astype(refs[-1].dtype)


def _contraction_tile(kdim, limit=6144):
    if kdim <= limit:
        return kdim
    return max(t for t in range(HEAD_DIM, limit + 1, HEAD_DIM) if kdim % t == 0)


def _matmul_call(pieces, w, layer, n, residual, out_dtype, tm):
    m = pieces[0][0].shape[0]
    tn = 512 if sum(k for _, _, k, _ in pieces) <= 4096 else 256
    tm, tn = min(tm, m), min(tn, n)
    assert m % tm == 0
    lhs_mode = dict(pipeline_mode=pl.Buffered(1)) if tm > MATMUL_ROW_TILE else {}
    a_specs, w_specs = [], []
    for _, col0, k, row0 in pieces:
        assert col0 % k == 0 and row0 % k == 0
        a_specs.append(pl.BlockSpec((tm, k), lambda i, j, c=col0 // k: (i, c), **lhs_mode))
        if layer is None:
            w_specs.append(pl.BlockSpec((k, tn), lambda i, j, r=row0 // k: (r, j)))
        else:
            w_specs.append(pl.BlockSpec((None, k, tn), lambda i, j, r=row0 // k: (layer, r, j)))
    args = [a for a, _, _, _ in pieces] + [w] * len(pieces)
    in_specs = a_specs + w_specs
    if residual is not None:
        in_specs.append(pl.BlockSpec((tm, tn), lambda i, j: (i, j)))
        args.append(residual)
    return pl.pallas_call(
        functools.partial(_matmul_kernel, n_lhs=len(pieces), has_res=residual is not None),
        grid=(m // tm, pl.cdiv(n, tn)),
        in_specs=in_specs,
        out_specs=pl.BlockSpec((tm, tn), lambda i, j: (i, j)),
        out_shape=jax.ShapeDtypeStruct((m, n), out_dtype),
        compiler_params=_params(2),
        name="matmul",
    )(*args)


def matmul(lhs, w, layer=None, residual=None, out_dtype=F32, tm=MATMUL_ROW_TILE):
    arrays = lhs if isinstance(lhs, (list, tuple)) else [lhs]
    n = w.shape[-1]
    if len(arrays) > 1:
        pieces, row0 = [], 0
        for a in arrays:
            pieces.append((a, 0, a.shape[1], row0))
            row0 += a.shape[1]
        return _matmul_call(pieces, w, layer, n, residual, out_dtype, tm)
    a = arrays[0]
    kdim = a.shape[1]
    tk = _contraction_tile(kdim)
    out = residual
    for c in range(kdim // tk):
        last = c == kdim // tk - 1
        out = _matmul_call([(a, c * tk, tk, c * tk)], w, layer, n, out, out_dtype if last else F32, tm)
    return out


def _gate_up_kernel(a_ref, wg_ref, wu_ref, o_ref):
    a = a_ref[...]
    g = jnp.dot(a, wg_ref[...].astype(BF16), preferred_element_type=F32)
    u = jnp.dot(a, wu_ref[...].astype(BF16), preferred_element_type=F32)
    o_ref[...] = (g * jax.nn.sigmoid(g) * u).astype(o_ref.dtype)


def swiglu_gate_up(a, wg, wu, layer, tm=BIG_ROW_TILE, tn=256):
    m, kdim = a.shape
    n = wg.shape[-1]
    tm, tn = min(tm, m), min(tn, n)
    w_spec = pl.BlockSpec((None, kdim, tn), lambda i, j: (layer, 0, j))
    lhs_mode = dict(pipeline_mode=pl.Buffered(1)) if tm > MATMUL_ROW_TILE else {}
    return pl.pallas_call(
        _gate_up_kernel,
        grid=(m // tm, pl.cdiv(n, tn)),
        in_specs=[pl.BlockSpec((tm, kdim), lambda i, j: (i, 0), **lhs_mode), w_spec, w_spec],
        out_specs=pl.BlockSpec((tm, tn), lambda i, j: (i, j)),
        out_shape=jax.ShapeDtypeStruct((m, n), BF16),
        compiler_params=_params(2),
        name="swiglu_gate_up",
    )(a, wg, wu)


def _rope_table_kernel(pos_ref, invf_ref, cos_ref, sin_ref):
    ang = pos_ref[...] * invf_ref[...]
    lane = lax.broadcasted_iota(jnp.int32, ang.shape, 1)
    s = jnp.sin(ang)
    cos_ref[...] = jnp.cos(ang)
    sin_ref[...] = jnp.where(lane < HALF, -s, s)


def rope_tables(positions, tr=256):
    n = positions.size
    tr = min(tr, n)
    inv_freq = ROPE_THETA ** (-jnp.arange(HALF, dtype=F32) / HALF)
    invf = jnp.concatenate([inv_freq, inv_freq]).reshape(1, HEAD_DIM)
    pos = jnp.broadcast_to(positions.astype(F32).reshape(n, 1), (n, HEAD_DIM))
    spec = pl.BlockSpec((tr, HEAD_DIM), lambda i: (i, 0))
    return pl.pallas_call(
        _rope_table_kernel,
        grid=(n // tr,),
        in_specs=[spec, pl.BlockSpec((1, HEAD_DIM), lambda i: (0, 0))],
        out_specs=[spec, spec],
        out_shape=[jax.ShapeDtypeStruct((n, HEAD_DIM), F32)] * 2,
        compiler_params=_params(1),
        name="rope_tables",
    )(pos, invf)


def _prep_keys(k_ref, g, cos_ref, sin_ref, out_ref, seq):
    for r0 in range(0, seq, PREP_ROWS):
        rows = slice(r0, min(r0 + PREP_ROWS, seq))
        out_ref[rows, :] = _rope(_rms(k_ref[rows, :], g), cos_ref[rows, :], sin_ref[rows, :]).astype(BF16)


def _row1(x):
    return x.reshape(1, HEAD_DIM)


_ROW_SPEC3 = pl.BlockSpec((1, HEAD_DIM), lambda b, h, i: (0, 0))


def _moba_kernel(q_ref, k_ref, v_ref, cos_ref, sin_ref, gq_ref, gk_ref, o_ref, kb_ref, vb_ref, *, nblk):
    blk = MOBA_BLOCK
    means = []
    for n in range(nblk):
        rows = slice(n * blk, (n + 1) * blk)
        k = _rope(_rms(k_ref[rows, :], gk_ref[...]), cos_ref[rows, :], sin_ref[rows, :])
        kb_ref[rows, :] = k.astype(BF16)
        means.append(jnp.mean(k, axis=0, keepdims=True))
    vb_ref[...] = v_ref[...].astype(BF16)
    row = lax.broadcasted_iota(jnp.int32, (blk, blk), 0)
    col = lax.broadcasted_iota(jnp.int32, (blk, blk), 1)

    for i in range(nblk):
        own = slice(i * blk, (i + 1) * blk)
        q = _rope(_rms(q_ref[own, :], gq_ref[...]), cos_ref[own, :], sin_ref[own, :])
        qb = (q * SCALE).astype(BF16)
        scores = [jnp.where(col <= row, _nt(qb, kb_ref[own, :]), -jnp.inf)]
        values = [vb_ref[own, :]]
        if i > MOBA_TOPK:
            gate = _nt(jnp.concatenate(means[:i], axis=0), q, precision=HIGHEST)
            sel = _to_columns((_top_rank(gate) < MOBA_TOPK).astype(F32))
        for n in range(i):
            rows = slice(n * blk, (n + 1) * blk)
            s = _nt(qb, kb_ref[rows, :])
            if i > MOBA_TOPK:
                s = jnp.where(sel[:, n:n + 1] > 0.5, s, -jnp.inf)
            scores.append(s)
            values.append(vb_ref[rows, :])
        acc, _, l = _softmax_chunks(scores, values)
        o_ref[own, :] = (acc / l).astype(o_ref.dtype)


def moba_attention(p, q_col, k_col, v_col, gq, gk, cos, sin, batch, seq):
    nblk = seq // MOBA_BLOCK
    head = lambda c0: pl.BlockSpec((seq, HEAD_DIM), lambda b, h: (b, c0 + h))
    table = pl.BlockSpec((seq, HEAD_DIM), lambda b, h: (b, 0))
    row_spec = pl.BlockSpec((1, HEAD_DIM), lambda b, h: (0, 0))
    return pl.pallas_call(
        functools.partial(_moba_kernel, nblk=nblk),
        grid=(batch, MOBA_HEADS),
        in_specs=[head(q_col), head(k_col), head(v_col), table, table, row_spec, row_spec],
        out_specs=pl.BlockSpec((seq, HEAD_DIM), lambda b, h: (b, h)),
        out_shape=jax.ShapeDtypeStruct((batch * seq, MOBA_HEADS * HEAD_DIM), BF16),
        scratch_shapes=[pltpu.VMEM((seq, HEAD_DIM), BF16), pltpu.VMEM((seq, HEAD_DIM), BF16)],
        compiler_params=_params(2),
        name="moba",
    )(p, p, p, cos, sin, _row1(gq), _row1(gk))


def _nsa_compress_kernel(kc_ref, vc_ref, pek_ref, pev_ref, phik_ref, phiv_ref, g_ref, cos_ref, sin_ref,
                         ko_ref, vo_ref, *, nslot):
    half = NSA_CMP_LEN // 2

    def compress(x_ref, pe_ref, phi_ref):
        first = jnp.zeros((nslot, HEAD_DIM), F32)
        second = jnp.zeros((nslot, HEAD_DIM), F32)
        for l in range(half):
            xl = x_ref[pl.ds(l, nslot, stride=NSA_CMP_STRIDE), :]
            first += jnp.dot((xl + pe_ref[l:l + 1, :]).astype(BF16), phi_ref[l].astype(BF16),
                             preferred_element_type=F32)
            second += jnp.dot((xl + pe_ref[half + l:half + l + 1, :]).astype(BF16),
                              phi_ref[half + l].astype(BF16), preferred_element_type=F32)
        return pltpu.roll(first, 1, 0) + second

    ends = pl.ds(NSA_CMP_STRIDE - 1, nslot, stride=NSA_CMP_STRIDE)
    kc = compress(kc_ref, pek_ref, phik_ref)
    kc = _rope(_rms(kc, g_ref[...]), cos_ref[ends, :], sin_ref[ends, :])
    ko_ref[0, 0] = kc.astype(BF16)
    vo_ref[0, 0] = compress(vc_ref, pev_ref, phiv_ref).astype(BF16)


def nsa_compress(p, kc_col, vc_col, pe_k, pe_v, phi_k, phi_v, g_kc, cos, sin, batch, seq):
    nslot = seq // NSA_CMP_STRIDE
    full = lambda shape: pl.BlockSpec(shape, lambda b, g: (0,) * len(shape))
    out_spec = pl.BlockSpec((1, 1, nslot, HEAD_DIM), lambda b, g: (b, g, 0, 0))
    out_shape = jax.ShapeDtypeStruct((batch, NSA_GROUPS, nslot, HEAD_DIM), BF16)
    return pl.pallas_call(
        functools.partial(_nsa_compress_kernel, nslot=nslot),
        grid=(batch, NSA_GROUPS),
        in_specs=[pl.BlockSpec((seq, HEAD_DIM), lambda b, g: (b, kc_col + g)),
                  pl.BlockSpec((seq, HEAD_DIM), lambda b, g: (b, vc_col + g)),
                  full((NSA_CMP_LEN, HEAD_DIM)), full((NSA_CMP_LEN, HEAD_DIM)),
                  full((NSA_CMP_LEN, HEAD_DIM, HEAD_DIM)), full((NSA_CMP_LEN, HEAD_DIM, HEAD_DIM)),
                  full((1, HEAD_DIM)),
                  pl.BlockSpec((seq, HEAD_DIM), lambda b, g: (b, 0)),
                  pl.BlockSpec((seq, HEAD_DIM), lambda b, g: (b, 0))],
        out_specs=[out_spec, out_spec],
        out_shape=[out_shape, out_shape],
        compiler_params=_params(2),
        name="nsa_compress",
    )(p, p, pe_k, pe_v, phi_k, phi_v, _row1(g_kc), cos, sin)


NSA_KEY_CHUNK = 256
NSA_WIDTH_STEP = 512


def _nsa_kernel(q_ref, gl_ref, kc_ref, vc_ref, ks_ref, vs_ref, kw_ref, vw_ref, cos_ref, sin_ref,
                gq_ref, gks_ref, gkw_ref, o_ref, ksb_ref, vsb_ref, kwb_ref, vwb_ref, *, tq, seq):
    qi = pl.program_id(2)
    q0 = qi * tq
    hg = NSA_HG
    nsel = seq // NSA_SEL_BLOCK
    nslot = seq // NSA_CMP_STRIDE

    @pl.when(qi == 0)
    def _():
        _prep_keys(ks_ref, gks_ref[...], cos_ref, sin_ref, ksb_ref, seq)
        _prep_keys(kw_ref, gkw_ref[...], cos_ref, sin_ref, kwb_ref, seq)
        vsb_ref[...] = vs_ref[...].astype(BF16)
        vwb_ref[...] = vw_ref[...].astype(BF16)

    own = pl.ds(pl.multiple_of(q0, tq), tq)
    cos, sin, gq = cos_ref[own, :], sin_ref[own, :], gq_ref[...]
    qs = []
    for h in range(hg):
        qh = _rope(_rms(q_ref[:, h * HEAD_DIM:(h + 1) * HEAD_DIM], gq), cos, sin)
        qs.append((qh * SCALE).astype(BF16))
    qst = jnp.concatenate(qs, axis=0)

    def tile_heads(x):
        return jnp.concatenate([x] * hg, axis=0)

    slot = lax.broadcasted_iota(jnp.int32, (tq, nslot), 1)
    tpos_c = q0 + lax.broadcasted_iota(jnp.int32, (tq, nslot), 0)
    valid = (slot >= 1) & (slot * NSA_CMP_STRIDE + (NSA_CMP_STRIDE - 1) <= tpos_c)
    s_c = _nt(qst, kc_ref[0, 0]) + tile_heads(jnp.where(valid, 0.0, -jnp.inf))
    m_c = jnp.max(s_c, axis=-1, keepdims=True)
    m_c = jnp.where(m_c == -jnp.inf, 0.0, m_c)
    e_c = jnp.exp(s_c - m_c)
    p_c = e_c / jnp.maximum(jnp.sum(e_c, axis=-1, keepdims=True), TINY)
    o_c = jnp.dot(p_c.astype(BF16), vc_ref[0, 0], preferred_element_type=F32)

    p_sum = p_c[0:tq]
    for h in range(1, hg):
        p_sum = p_sum + p_c[h * tq:(h + 1) * tq]
    per_sel = NSA_SEL_BLOCK // NSA_CMP_STRIDE
    span = NSA_CMP_LEN // NSA_CMP_STRIDE
    b_idx = lax.broadcasted_iota(jnp.int32, (nsel, nslot), 0)
    j_idx = lax.broadcasted_iota(jnp.int32, (nsel, nslot), 1)
    overlap = ((j_idx >= 1) & (j_idx - 1 > per_sel * b_idx - span) & (j_idx - 1 < per_sel * (b_idx + 1)))
    imp = _nt(overlap.astype(F32), p_sum, precision=HIGHEST)
    blk = lax.broadcasted_iota(jnp.int32, (nsel, tq), 0)
    cur = (q0 + lax.broadcasted_iota(jnp.int32, (nsel, tq), 1)) >> NSA_SEL_SHIFT
    forced = (blk == cur) | (blk == 0)
    imp = jnp.where(forced, jnp.inf, jnp.where(blk <= cur, imp, -jnp.inf))
    sel = _to_columns((_top_rank(imp) < min(NSA_SEL_TOPN, nsel)).astype(F32)).astype(BF16)

    row = lax.broadcasted_iota(jnp.int32, (tq, tq), 0)
    col = lax.broadcasted_iota(jnp.int32, (tq, tq), 1)
    nback = NSA_WINDOW // tq
    scores, values = [], []
    for d in range(nback + 1):
        keys = pl.ds(pl.multiple_of(jnp.maximum(qi - d, 0) * tq, tq), tq)
        s = _nt(qst, kwb_ref[keys, :])
        exists = jnp.where(qi >= d, 0.0, -jnp.inf)
        if d == 0:
            s = s + tile_heads(jnp.where(col <= row, 0.0, -jnp.inf))
        elif d == nback:
            s = s + tile_heads(jnp.where(col > row, exists, -jnp.inf))
        else:
            s = s + exists
        scores.append(s)
        values.append(vwb_ref[keys, :])
    acc_w, _, l_w = _softmax_chunks(scores, values)
    o_w = acc_w / l_w

    per_group = 3 * hg
    shift = (HEAD_DIM - per_group * pl.program_id(1)) % HEAD_DIM
    gates = jax.nn.sigmoid(pltpu.roll(gl_ref[...], shift, 1))

    ck = min(NSA_KEY_CHUNK, seq)
    e_row = lax.broadcasted_iota(jnp.int32, (nsel, ck), 0)
    e_col = lax.broadcasted_iota(jnp.int32, (nsel, ck), 1)
    k_col = lax.broadcasted_iota(jnp.int32, (tq, ck), 1)
    t_row = q0 + lax.broadcasted_iota(jnp.int32, (tq, ck), 0)

    def finish(width):
        scores, values = [], []
        for c in range(width // ck):
            expand = (((c * ck + e_col) >> NSA_SEL_SHIFT) == e_row).astype(BF16)
            chosen = jnp.dot(sel, expand, preferred_element_type=F32) > 0.5
            keep = chosen & (c * ck + k_col <= t_row)
            keys = slice(c * ck, (c + 1) * ck)
            scores.append(_nt(qst, ksb_ref[keys, :]) + tile_heads(jnp.where(keep, 0.0, -jnp.inf)))
            values.append(vsb_ref[keys, :])
        acc_s, _, l_s = _softmax_chunks(scores, values)
        o_s = acc_s / l_s
        for h in range(hg):
            hs = slice(h * tq, (h + 1) * tq)
            out = (gates[:, 3 * h:3 * h + 1] * o_c[hs] + gates[:, 3 * h + 1:3 * h + 2] * o_s[hs]
                   + gates[:, 3 * h + 2:3 * h + 3] * o_w[hs])
            o_ref[:, h * HEAD_DIM:(h + 1) * HEAD_DIM] = out.astype(o_ref.dtype)

    step = min(NSA_WIDTH_STEP, seq)
    for w in range(seq // step):
        pl.when((q0 // step) == w)(functools.partial(finish, (w + 1) * step))


def nsa_attention(p, q_col, ks_col, vs_col, kw_col, vw_col, gl_col, kcmp, vcmp, gq, gks, gkw, cos, sin,
                  batch, seq, tq=128):
    tq = min(tq, seq)
    nq = seq // tq
    gw = NSA_HG * HEAD_DIM
    nslot = seq // NSA_CMP_STRIDE
    head = lambda c0: pl.BlockSpec((seq, HEAD_DIM), lambda b, g, i: (b, c0 + g))
    table = pl.BlockSpec((seq, HEAD_DIM), lambda b, g, i: (b, 0))
    cmp_spec = pl.BlockSpec((1, 1, nslot, HEAD_DIM), lambda b, g, i: (b, g, 0, 0))
    return pl.pallas_call(
        functools.partial(_nsa_kernel, tq=tq, seq=seq),
        grid=(batch, NSA_GROUPS, nq),
        in_specs=[pl.BlockSpec((tq, gw), lambda b, g, i: (b * nq + i, q_col // NSA_HG + g)),
                  pl.BlockSpec((tq, HEAD_DIM), lambda b, g, i: (b * nq + i, gl_col)),
                  cmp_spec, cmp_spec,
                  head(ks_col), head(vs_col), head(kw_col), head(vw_col), table, table,
                  _ROW_SPEC3, _ROW_SPEC3, _ROW_SPEC3],
        out_specs=pl.BlockSpec((tq, gw), lambda b, g, i: (b * nq + i, g)),
        out_shape=jax.ShapeDtypeStruct((batch * seq, NSA_HEADS * HEAD_DIM), BF16),
        scratch_shapes=[pltpu.VMEM((seq, HEAD_DIM), BF16)] * 4,
        compiler_params=_params(3),
        name="nsa",
    )(p, p, kcmp, vcmp, p, p, p, p, cos, sin, _row1(gq), _row1(gks), _row1(gkw))


def _sb_kernel(q_ref, k_ref, v_ref, o_ref, kb_ref, vb_ref, run_ref, acc_ref, *, tq, tk):
    qi = pl.program_id(2)
    per_q = tq // tk

    @pl.when(qi == 0)
    def _():
        kb_ref[...] = k_ref[...].astype(BF16)
        vb_ref[...] = v_ref[...].astype(BF16)

    later = (lax.broadcasted_iota(jnp.int32, (tk, tk), 0)
             > lax.broadcasted_iota(jnp.int32, (tk, tk), 1)).astype(BF16)
    row = lax.broadcasted_iota(jnp.int32, (tq, tk), 0)
    col = lax.broadcasted_iota(jnp.int32, (tq, tk), 1)
    qb = (q_ref[...] * SCALE).astype(BF16)

    def tiles(first, key0s):
        keys = [pl.ds(pl.multiple_of((first - j) * tk, tk), tk) for j in range(len(key0s))]
        zs = [_nt(qb, kb_ref[kk, :]) for kk in keys]
        log_betas, log_rests, stricts = [], [], []
        for z, key0 in zip(zs, key0s):
            log_beta = jnp.minimum(z, 0.0) - jnp.log(1.0 + jnp.exp(-jnp.abs(z)))
            log_rest = log_beta - z
            strict = None
            if key0 is not None:
                strict = key0 + col < row
                log_rest = jnp.where(strict, log_rest, 0.0)
            log_betas.append(log_beta)
            log_rests.append(log_rest)
            stricts.append(strict)
        insides = []
        for log_rest in log_rests:
            hi = log_rest.astype(BF16)
            lo = (log_rest - hi.astype(F32)).astype(BF16)
            both = jnp.dot(jnp.concatenate([hi, lo], axis=0), later, preferred_element_type=F32)
            insides.append(both[:tq] + both[tq:])
        run = run_ref[...]
        acc = acc_ref[...]
        for kk, log_beta, log_rest, strict, inside in zip(keys, log_betas, log_rests, stricts, insides):
            a = jnp.exp(log_beta + inside + jnp.concatenate([run] * (tk // HEAD_DIM), axis=1))
            if strict is not None:
                a = jnp.where(strict, a, 0.0)
            acc = acc + jnp.dot(a.astype(BF16), vb_ref[kk, :], preferred_element_type=F32)
            run = run + jnp.sum(log_rest, axis=-1, keepdims=True)
        run_ref[...] = run
        acc_ref[...] = acc

    run_ref[...] = jnp.zeros_like(run_ref)
    acc_ref[...] = jnp.zeros_like(acc_ref)
    tiles((qi + 1) * per_q - 1, [(per_q - 1 - j) * tk for j in range(per_q)])

    def step(i, carry):
        tiles((qi - i) * per_q - 1, [None] * per_q)
        return carry

    lax.fori_loop(0, qi, step, 0)
    o_ref[...] = acc_ref[...].astype(o_ref.dtype)


def sb_attention(p, q_col, k_col, v_col, batch, seq, tq=512, tk=256):
    tq = min(tq, seq)
    tk = min(tk, tq)
    nq = seq // tq
    head = lambda c0: pl.BlockSpec((seq, HEAD_DIM), lambda b, h, i: (b, c0 + h))
    return pl.pallas_call(
        functools.partial(_sb_kernel, tq=tq, tk=tk),
        grid=(batch, SB_HEADS, nq),
        in_specs=[pl.BlockSpec((tq, HEAD_DIM), lambda b, h, i: (b * nq + i, q_col + h)),
                  head(k_col), head(v_col)],
        out_specs=pl.BlockSpec((tq, HEAD_DIM), lambda b, h, i: (b * nq + i, h)),
        out_shape=jax.ShapeDtypeStruct((batch * seq, SB_HEADS * HEAD_DIM), BF16),
        scratch_shapes=[pltpu.VMEM((seq, HEAD_DIM), BF16), pltpu.VMEM((seq, HEAD_DIM), BF16),
                        pltpu.VMEM((tq, HEAD_DIM), F32), pltpu.VMEM((tq, HEAD_DIM), F32)],
        compiler_params=_params(3),
        name="stick_breaking",
    )(p, p, p)


def _dil_kernel(q0_ref, q1_ref, q2_ref, k0_ref, k1_ref, k2_ref, v0_ref, v1_ref, v2_ref,
                cos_ref, sin_ref, gq_ref, gk_ref, o_ref,
                tmp_ref, qd_ref, kd_ref, vd_ref, og0_ref, og1_ref, og2_ref, lse0_ref, lse1_ref, lse2_ref,
                *, seq):
    q_refs, k_refs, v_refs = (q0_ref, q1_ref, q2_ref), (k0_ref, k1_ref, k2_ref), (v0_ref, v1_ref, v2_ref)
    og_refs, lse_refs = (og0_ref, og1_ref, og2_ref), (lse0_ref, lse1_ref, lse2_ref)
    ta = DIL_SPAN
    row = lax.broadcasted_iota(jnp.int32, (ta, ta), 0)
    col = lax.broadcasted_iota(jnp.int32, (ta, ta), 1)

    for gi, (window, dil) in enumerate(DIL_CONFIGS):
        assert window == dil * DIL_SPAN and seq % (dil * ta) == 0
        n_a = seq // dil
        tiles_per_class = n_a // ta

        def class_major(dst_ref, rows, val, dil=dil, n_a=n_a):
            if dil == 1:
                dst_ref[rows, :] = val.astype(BF16)
                return
            tmp_ref[rows, :] = val
            per = (rows.stop - rows.start) // dil
            a0 = rows.start // dil
            for rho in range(dil):
                src = pl.ds(rows.start + rho, per, stride=dil)
                dst_ref[rho * n_a + a0:rho * n_a + a0 + per, :] = tmp_ref[src, :].astype(BF16)

        for r0 in range(0, seq, PREP_ROWS):
            rows = slice(r0, r0 + PREP_ROWS)
            cos, sin = cos_ref[rows, :], sin_ref[rows, :]
            class_major(qd_ref, rows, _rope(_rms(q_refs[gi][rows, :], gq_ref[...]), cos, sin) * SCALE)
            class_major(kd_ref, rows, _rope(_rms(k_refs[gi][rows, :], gk_ref[...]), cos, sin))
            class_major(vd_ref, rows, v_refs[gi][rows, :])

        for j in range(seq // ta):
            rho, at = divmod(j, tiles_per_class)
            rows = slice(j * ta, (j + 1) * ta)
            qj = qd_ref[rows, :]
            scores = [jnp.where(col <= row, _nt(qj, kd_ref[rows, :]), -jnp.inf)]
            values = [vd_ref[rows, :]]
            if at > 0:
                prev = slice((j - 1) * ta, j * ta)
                scores.append(jnp.where(col >= row, _nt(qj, kd_ref[prev, :]), -jnp.inf))
                values.append(vd_ref[prev, :])
            acc, m, l = _softmax_chunks(scores, values)
            tokens = pl.ds(dil * at * ta + rho, ta, stride=dil) if dil > 1 else rows
            og_refs[gi][tokens, :] = acc / l
            lse_refs[gi][tokens, :] = jnp.broadcast_to(m + jnp.log(l), (ta, HEAD_DIM))

    for r0 in range(0, seq, PREP_ROWS):
        rows = slice(r0, r0 + PREP_ROWS)
        lses = [ref[rows, :] for ref in lse_refs]
        top = functools.reduce(jnp.maximum, lses)
        ws = [jnp.exp(x - top) for x in lses]
        total = functools.reduce(lambda a, b: a + b, ws)
        out = functools.reduce(lambda a, b: a + b, [(w / total) * ref[rows, :] for w, ref in zip(ws, og_refs)])
        o_ref[rows, :] = out.astype(o_ref.dtype)


def dilated_attention(p, q_col, k_col, v_col, gq, gk, cos, sin, batch, seq):
    head = lambda c0: pl.BlockSpec((seq, HEAD_DIM), lambda b, h: (b, c0 + h))
    table = pl.BlockSpec((seq, HEAD_DIM), lambda b, h: (b, 0))
    row_spec = pl.BlockSpec((1, HEAD_DIM), lambda b, h: (0, 0))
    groups = range(DIL_GROUPS)
    return pl.pallas_call(
        functools.partial(_dil_kernel, seq=seq),
        grid=(batch, DIL_HEADS),
        in_specs=([head(q_col + gi * DIL_HEADS) for gi in groups]
                  + [head(k_col + gi * DIL_HEADS) for gi in groups]
                  + [head(v_col + gi * DIL_HEADS) for gi in groups]
                  + [table, table, row_spec, row_spec]),
        out_specs=pl.BlockSpec((seq, HEAD_DIM), lambda b, h: (b, h)),
        out_shape=jax.ShapeDtypeStruct((batch * seq, DIL_HEADS * HEAD_DIM), BF16),
        scratch_shapes=[pltpu.VMEM((seq, HEAD_DIM), F32)]
                       + [pltpu.VMEM((seq, HEAD_DIM), BF16)] * 3
                       + [pltpu.VMEM((seq, HEAD_DIM), F32)] * (2 * DIL_GROUPS),
        compiler_params=_params(2),
        name="dilated",
    )(*([p] * 9), cos, sin, _row1(gq), _row1(gk))


def _xattn_kernel(q_ref, kv_ref, gq_ref, gk_ref, o_ref):
    gq, gk = gq_ref[...], gk_ref[...]
    width = XATTN_HEADS * HEAD_DIM
    for h in range(XATTN_HEADS):
        hs = slice(h * HEAD_DIM, (h + 1) * HEAD_DIM)
        q = (_rms(q_ref[:, hs], gq) * SCALE).astype(BF16)
        k = _rms(kv_ref[:, hs], gk).astype(BF16)
        v = kv_ref[:, width + h * HEAD_DIM:width + (h + 1) * HEAD_DIM].astype(BF16)
        s = _nt(q, k)
        e = jnp.exp(s - jnp.max(s, axis=-1, keepdims=True))
        p = e / jnp.sum(e, axis=-1, keepdims=True)
        o_ref[:, hs] = jnp.dot(p.astype(BF16), v, preferred_element_type=F32).astype(o_ref.dtype)


def memory_cross_attention(q, kv, gq, gk, batch, seq, mem_len, tq=512):
    tq = min(tq, seq)
    nq = seq // tq
    width = XATTN_HEADS * HEAD_DIM
    return pl.pallas_call(
        _xattn_kernel,
        grid=(batch, nq),
        in_specs=[pl.BlockSpec((tq, width), lambda b, i: (b * nq + i, 0)),
                  pl.BlockSpec((mem_len, 2 * width), lambda b, i: (b, 0)),
                  pl.BlockSpec((1, HEAD_DIM), lambda b, i: (0, 0)),
                  pl.BlockSpec((1, HEAD_DIM), lambda b, i: (0, 0))],
        out_specs=pl.BlockSpec((tq, width), lambda b, i: (b * nq + i, 0)),
        out_shape=jax.ShapeDtypeStruct((batch * seq, width), BF16),
        compiler_params=_params(2),
        name="xattn",
    )(q, kv, _row1(gq), _row1(gk))


def _even_mixer(x2, h, cos, sin, batch, seq, e, w_in, w_out, moba_gq, moba_gk, nsa_gq, nsa_gk_cmp,
                nsa_gk_slc, nsa_gk_win, pe_k, pe_v, phi_k, phi_v):
    hd = HEAD_DIM
    main = (3 * MOBA_HEADS + NSA_HEADS + 6 * NSA_GROUPS) * hd
    p = matmul(h, w_in, layer=e, tm=BIG_ROW_TILE)

    o_a = moba_attention(p, 0, 16, 32, moba_gq, moba_gk, cos, sin, batch, seq)
    kcmp, vcmp = nsa_compress(p, 64, 68, pe_k, pe_v, phi_k, phi_v, nsa_gk_cmp, cos, sin, batch, seq)
    o_b = nsa_attention(p, 48, 72, 76, 80, 84, main // hd, kcmp, vcmp, nsa_gq, nsa_gk_slc, nsa_gk_win,
                        cos, sin, batch, seq)

    return matmul([o_a, o_b], w_out, layer=e, residual=x2)


def _odd_mixer(x2, h, cos, sin, batch, seq, o, w_in, w_out, dil_gq, dil_gk):
    hd = HEAD_DIM
    p = matmul(h, w_in, layer=o, tm=BIG_ROW_TILE)
    o_c = sb_attention(p, 0, SB_HEADS, 2 * SB_HEADS, batch, seq)
    nd = DIL_GROUPS * DIL_HEADS
    o_d = dilated_attention(p, 3 * SB_HEADS, 3 * SB_HEADS + nd, 3 * SB_HEADS + 2 * nd, dil_gq, dil_gk,
                            cos, sin, batch, seq)
    return matmul([o_c, o_d], w_out, layer=o, residual=x2)


def kernel(x, mem, positions, mix_norm, even_w_in, even_w_out, moba_gq, moba_gk, nsa_gq, nsa_gk_cmp, nsa_gk_slc, nsa_gk_win, nsa_pe_k, nsa_pe_v, nsa_phi_k, nsa_phi_v, odd_w_in, odd_w_out, dil_gq, dil_gk, xattn_norm, mem_norm, xattn_wq, xattn_wkv, xattn_wo, xattn_gq, xattn_gk, ffn_norm, ffn_wg, ffn_wu, ffn_wd):
    batch, seq, d = x.shape
    mem_len = mem.shape[1]
    depth = mix_norm.shape[0]
    x2 = x.reshape(batch * seq, d)
    mem2 = mem.reshape(batch * mem_len, d)
    cos, sin = rope_tables(positions)

    for layer in range(depth):
        h = rmsnorm_rows(x2, mix_norm[layer])
        if layer % 2 == 0:
            e = layer // 2
            x2 = _even_mixer(x2, h, cos, sin, batch, seq, e, even_w_in, even_w_out, moba_gq[e],
                             moba_gk[e], nsa_gq[e], nsa_gk_cmp[e], nsa_gk_slc[e], nsa_gk_win[e],
                             nsa_pe_k[e], nsa_pe_v[e], nsa_phi_k[e], nsa_phi_v[e])
        else:
            o = layer // 2
            x2 = _odd_mixer(x2, h, cos, sin, batch, seq, o, odd_w_in, odd_w_out, dil_gq[o], dil_gk[o])

        h = rmsnorm_rows(x2, xattn_norm[layer])
        mem_n = rmsnorm_rows(mem2, mem_norm[layer])
        q = matmul(h, xattn_wq, layer=layer)
        kv = matmul(mem_n, xattn_wkv, layer=layer)
        o_x = memory_cross_attention(q, kv, xattn_gq[layer], xattn_gk[layer], batch, seq, mem_len)
        x2 = matmul(o_x, xattn_wo, layer=layer, residual=x2)

        h = rmsnorm_rows(x2, ffn_norm[layer])
        hidden = swiglu_gate_up(h, ffn_wg, ffn_wu, layer)
        x2 = matmul(hidden, ffn_wd, layer=layer, residual=x2, tm=BIG_ROW_TILE)

    return x2.reshape(batch, seq, d)
```

```python
import functools

import jax
import jax.numpy as jnp
from jax import lax
from jax.experimental import pallas as pl
from jax.experimental.pallas import tpu as pltpu

F32 = jnp.float32
BF16 = jnp.bfloat16
HIGHEST = lax.Precision.HIGHEST

HEAD_DIM = 128
HALF = HEAD_DIM // 2
ROPE_THETA = 10000.0
NORM_EPS = 1e-6
TINY = 1e-30
SCALE = HEAD_DIM ** -0.5

MOBA_HEADS = 16
MOBA_BLOCK = 256
MOBA_TOPK = 3
NSA_HEADS = 16
NSA_GROUPS = 4
NSA_HG = NSA_HEADS // NSA_GROUPS
NSA_CMP_LEN = 32
NSA_CMP_STRIDE = 16
NSA_SEL_BLOCK = 64
NSA_SEL_SHIFT = NSA_SEL_BLOCK.bit_length() - 1
NSA_SEL_TOPN = 16
NSA_WINDOW = 512
DIL_CONFIGS = ((128, 1), (512, 4), (2048, 16))
DIL_GROUPS = len(DIL_CONFIGS)
DIL_HEADS = 8
DIL_SPAN = 128
SB_HEADS = 24
XATTN_HEADS = 4

VMEM_LIMIT_BYTES = 56 * 1024 * 1024
NT_DIMS = (((1,), (1,)), ((), ()))
PREP_ROWS = 256
MATMUL_ROW_TILE = 1024
BIG_ROW_TILE = 2048


def _params(n_grid):
    return pltpu.CompilerParams(dimension_semantics=("arbitrary",) * n_grid,
                                vmem_limit_bytes=VMEM_LIMIT_BYTES)


def _nt(a, b, precision=None):
    return lax.dot_general(a, b, NT_DIMS, precision=precision, preferred_element_type=F32)


def _rms(x, g):
    return x * lax.rsqrt(jnp.mean(x * x, axis=-1, keepdims=True) + NORM_EPS) * g


def _rope(x, cos, sin_signed):
    return x * cos + pltpu.roll(x, HALF, 1) * sin_signed


def _softmax_chunks(scores, values):
    m = jnp.max(functools.reduce(jnp.maximum, scores), axis=-1, keepdims=True)
    probs = [jnp.exp(s - m) for s in scores]
    l = jnp.sum(functools.reduce(lambda a, b: a + b, probs), axis=-1, keepdims=True)
    acc = None
    for p, v in zip(probs, values):
        part = jnp.dot(p.astype(BF16), v, preferred_element_type=F32)
        acc = part if acc is None else acc + part
    return acc, m, l


def _top_rank(vals):
    idx = lax.broadcasted_iota(jnp.int32, vals.shape, 0)
    rank = jnp.zeros(vals.shape, jnp.int32)
    for m in range(vals.shape[0]):
        c = vals[m:m + 1, :]
        ahead = (c > vals) | ((c == vals) & (idx > m))
        rank = rank + ahead.astype(jnp.int32)
    return rank


def _to_columns(x_t):
    rows = x_t.shape[1]
    eye = (lax.broadcasted_iota(jnp.int32, (rows, rows), 0)
           == lax.broadcasted_iota(jnp.int32, (rows, rows), 1)).astype(BF16)
    return _nt(eye, x_t.astype(BF16))


def _rmsnorm_kernel(x_ref, g_ref, o_ref):
    o_ref[...] = _rms(x_ref[...], g_ref[...]).astype(o_ref.dtype)


def rmsnorm_rows(x, g, tm=256):
    m, d = x.shape
    tm = min(tm, m)
    return pl.pallas_call(
        _rmsnorm_kernel,
        grid=(m // tm,),
        in_specs=[pl.BlockSpec((tm, d), lambda i: (i, 0)), pl.BlockSpec((1, d), lambda i: (0, 0))],
        out_specs=pl.BlockSpec((tm, d), lambda i: (i, 0)),
        out_shape=jax.ShapeDtypeStruct((m, d), BF16),
        compiler_params=_params(1),
        name="rmsnorm",
    )(x, g.reshape(1, d))


def _matmul_kernel(*refs, n_lhs, has_res):
    a_refs, w_refs = refs[:n_lhs], refs[n_lhs:2 * n_lhs]
    out = None
    for a_ref, w_ref in zip(a_refs, w_refs):
        part = jnp.dot(a_ref[...], w_ref[...].astype(BF16), preferred_element_type=F32)
        out = part if out is None else out + part
    if has_res:
        out = out + refs[2 * n_lhs][...]
    refs[-1][...] = out.astype(refs[-1].dtype)


def _contraction_tile(kdim, limit=6144):
    if kdim <= limit:
        return kdim
    return max(t for t in range(HEAD_DIM, limit + 1, HEAD_DIM) if kdim % t == 0)


def _matmul_call(pieces, w, layer, n, residual, out_dtype, tm):
    m = pieces[0][0].shape[0]
    tn = 512 if sum(k for _, _, k, _ in pieces) <= 4096 else 256
    tm, tn = min(tm, m), min(tn, n)
    assert m % tm == 0
    lhs_mode = dict(pipeline_mode=pl.Buffered(1)) if tm > MATMUL_ROW_TILE else {}
    a_specs, w_specs = [], []
    for _, col0, k, row0 in pieces:
        assert col0 % k == 0 and row0 % k == 0
        a_specs.append(pl.BlockSpec((tm, k), lambda i, j, c=col0 // k: (i, c), **lhs_mode))
        if layer is None:
            w_specs.append(pl.BlockSpec((k, tn), lambda i, j, r=row0 // k: (r, j)))
        else:
            w_specs.append(pl.BlockSpec((None, k, tn), lambda i, j, r=row0 // k: (layer, r, j)))
    args = [a for a, _, _, _ in pieces] + [w] * len(pieces)
    in_specs = a_specs + w_specs
    if residual is not None:
        in_specs.append(pl.BlockSpec((tm, tn), lambda i, j: (i, j)))
        args.append(residual)
    return pl.pallas_call(
        functools.partial(_matmul_kernel, n_lhs=len(pieces), has_res=residual is not None),
        grid=(m // tm, pl.cdiv(n, tn)),
        in_specs=in_specs,
        out_specs=pl.BlockSpec((tm, tn), lambda i, j: (i, j)),
        out_shape=jax.ShapeDtypeStruct((m, n), out_dtype),
        compiler_params=_params(2),
        name="matmul",
    )(*args)


def matmul(lhs, w, layer=None, residual=None, out_dtype=F32, tm=MATMUL_ROW_TILE):
    arrays = lhs if isinstance(lhs, (list, tuple)) else [lhs]
    n = w.shape[-1]
    if len(arrays) > 1:
        pieces, row0 = [], 0
        for a in arrays:
            pieces.append((a, 0, a.shape[1], row0))
            row0 += a.shape[1]
        return _matmul_call(pieces, w, layer, n, residual, out_dtype, tm)
    a = arrays[0]
    kdim = a.shape[1]
    tk = _contraction_tile(kdim)
    out = residual
    for c in range(kdim // tk):
        last = c == kdim // tk - 1
        out = _matmul_call([(a, c * tk, tk, c * tk)], w, layer, n, out, out_dtype if last else F32, tm)
    return out


def _gate_up_kernel(a_ref, wg_ref, wu_ref, o_ref):
    a = a_ref[...]
    g = jnp.dot(a, wg_ref[...].astype(BF16), preferred_element_type=F32)
    u = jnp.dot(a, wu_ref[...].astype(BF16), preferred_element_type=F32)
    o_ref[...] = (g * jax.nn.sigmoid(g) * u).astype(o_ref.dtype)


def swiglu_gate_up(a, wg, wu, layer, tm=BIG_ROW_TILE, tn=256):
    m, kdim = a.shape
    n = wg.shape[-1]
    tm, tn = min(tm, m), min(tn, n)
    w_spec = pl.BlockSpec((None, kdim, tn), lambda i, j: (layer, 0, j))
    lhs_mode = dict(pipeline_mode=pl.Buffered(1)) if tm > MATMUL_ROW_TILE else {}
    return pl.pallas_call(
        _gate_up_kernel,
        grid=(m // tm, pl.cdiv(n, tn)),
        in_specs=[pl.BlockSpec((tm, kdim), lambda i, j: (i, 0), **lhs_mode), w_spec, w_spec],
        out_specs=pl.BlockSpec((tm, tn), lambda i, j: (i, j)),
        out_shape=jax.ShapeDtypeStruct((m, n), BF16),
        compiler_params=_params(2),
        name="swiglu_gate_up",
    )(a, wg, wu)


def _rope_table_kernel(pos_ref, invf_ref, cos_ref, sin_ref):
    ang = pos_ref[...] * invf_ref[...]
    lane = lax.broadcasted_iota(jnp.int32, ang.shape, 1)
    s = jnp.sin(ang)
    cos_ref[...] = jnp.cos(ang)
    sin_ref[...] = jnp.where(lane < HALF, -s, s)


def rope_tables(positions, tr=256):
    n = positions.size
    tr = min(tr, n)
    inv_freq = ROPE_THETA ** (-jnp.arange(HALF, dtype=F32) / HALF)
    invf = jnp.concatenate([inv_freq, inv_freq]).reshape(1, HEAD_DIM)
    pos = jnp.broadcast_to(positions.astype(F32).reshape(n, 1), (n, HEAD_DIM))
    spec = pl.BlockSpec((tr, HEAD_DIM), lambda i: (i, 0))
    return pl.pallas_call(
        _rope_table_kernel,
        grid=(n // tr,),
        in_specs=[spec, pl.BlockSpec((1, HEAD_DIM), lambda i: (0, 0))],
        out_specs=[spec, spec],
        out_shape=[jax.ShapeDtypeStruct((n, HEAD_DIM), F32)] * 2,
        compiler_params=_params(1),
        name="rope_tables",
    )(pos, invf)


def _prep_keys(k_ref, g, cos_ref, sin_ref, out_ref, seq):
    for r0 in range(0, seq, PREP_ROWS):
        rows = slice(r0, min(r0 + PREP_ROWS, seq))
        out_ref[rows, :] = _rope(_rms(k_ref[rows, :], g), cos_ref[rows, :], sin_ref[rows, :]).astype(BF16)


def _row1(x):
    return x.reshape(1, HEAD_DIM)


_ROW_SPEC3 = pl.BlockSpec((1, HEAD_DIM), lambda b, h, i: (0, 0))


def _moba_kernel(q_ref, k_ref, v_ref, cos_ref, sin_ref, gq_ref, gk_ref, o_ref, kb_ref, vb_ref, *, nblk):
    blk = MOBA_BLOCK
    means = []
    for n in range(nblk):
        rows = slice(n * blk, (n + 1) * blk)
        k = _rope(_rms(k_ref[rows, :], gk_ref[...]), cos_ref[rows, :], sin_ref[rows, :])
        kb_ref[rows, :] = k.astype(BF16)
        means.append(jnp.mean(k, axis=0, keepdims=True))
    vb_ref[...] = v_ref[...].astype(BF16)
    row = lax.broadcasted_iota(jnp.int32, (blk, blk), 0)
    col = lax.broadcasted_iota(jnp.int32, (blk, blk), 1)

    for i in range(nblk):
        own = slice(i * blk, (i + 1) * blk)
        q = _rope(_rms(q_ref[own, :], gq_ref[...]), cos_ref[own, :], sin_ref[own, :])
        qb = (q * SCALE).astype(BF16)
        scores = [jnp.where(col <= row, _nt(qb, kb_ref[own, :]), -jnp.inf)]
        values = [vb_ref[own, :]]
        if i > MOBA_TOPK:
            gate = _nt(jnp.concatenate(means[:i], axis=0), q, precision=HIGHEST)
            sel = _to_columns((_top_rank(gate) < MOBA_TOPK).astype(F32))
        for n in range(i):
            rows = slice(n * blk, (n + 1) * blk)
            s = _nt(qb, kb_ref[rows, :])
            if i > MOBA_TOPK:
                s = jnp.where(sel[:, n:n + 1] > 0.5, s, -jnp.inf)
            scores.append(s)
            values.append(vb_ref[rows, :])
        acc, _, l = _softmax_chunks(scores, values)
        o_ref[own, :] = (acc / l).astype(o_ref.dtype)


def moba_attention(p, q_col, k_col, v_col, gq, gk, cos, sin, batch, seq):
    nblk = seq // MOBA_BLOCK
    head = lambda c0: pl.BlockSpec((seq, HEAD_DIM), lambda b, h: (b, c0 + h))
    table = pl.BlockSpec((seq, HEAD_DIM), lambda b, h: (b, 0))
    row_spec = pl.BlockSpec((1, HEAD_DIM), lambda b, h: (0, 0))
    return pl.pallas_call(
        functools.partial(_moba_kernel, nblk=nblk),
        grid=(batch, MOBA_HEADS),
        in_specs=[head(q_col), head(k_col), head(v_col), table, table, row_spec, row_spec],
        out_specs=pl.BlockSpec((seq, HEAD_DIM), lambda b, h: (b, h)),
        out_shape=jax.ShapeDtypeStruct((batch * seq, MOBA_HEADS * HEAD_DIM), BF16),
        scratch_shapes=[pltpu.VMEM((seq, HEAD_DIM), BF16), pltpu.VMEM((seq, HEAD_DIM), BF16)],
        compiler_params=_params(2),
        name="moba",
    )(p, p, p, cos, sin, _row1(gq), _row1(gk))


def _nsa_compress_kernel(kc_ref, vc_ref, pek_ref, pev_ref, phik_ref, phiv_ref, g_ref, cos_ref, sin_ref,
                         ko_ref, vo_ref, *, nslot):
    half = NSA_CMP_LEN // 2

    def compress(x_ref, pe_ref, phi_ref):
        first = jnp.zeros((nslot, HEAD_DIM), F32)
        second = jnp.zeros((nslot, HEAD_DIM), F32)
        for l in range(half):
            xl = x_ref[pl.ds(l, nslot, stride=NSA_CMP_STRIDE), :]
            first += jnp.dot((xl + pe_ref[l:l + 1, :]).astype(BF16), phi_ref[l].astype(BF16),
                             preferred_element_type=F32)
            second += jnp.dot((xl + pe_ref[half + l:half + l + 1, :]).astype(BF16),
                              phi_ref[half + l].astype(BF16), preferred_element_type=F32)
        return pltpu.roll(first, 1, 0) + second

    ends = pl.ds(NSA_CMP_STRIDE - 1, nslot, stride=NSA_CMP_STRIDE)
    kc = compress(kc_ref, pek_ref, phik_ref)
    kc = _rope(_rms(kc, g_ref[...]), cos_ref[ends, :], sin_ref[ends, :])
    ko_ref[0, 0] = kc.astype(BF16)
    vo_ref[0, 0] = compress(vc_ref, pev_ref, phiv_ref).astype(BF16)


def nsa_compress(p, kc_col, vc_col, pe_k, pe_v, phi_k, phi_v, g_kc, cos, sin, batch, seq):
    nslot = seq // NSA_CMP_STRIDE
    full = lambda shape: pl.BlockSpec(shape, lambda b, g: (0,) * len(shape))
    out_spec = pl.BlockSpec((1, 1, nslot, HEAD_DIM), lambda b, g: (b, g, 0, 0))
    out_shape = jax.ShapeDtypeStruct((batch, NSA_GROUPS, nslot, HEAD_DIM), BF16)
    return pl.pallas_call(
        functools.partial(_nsa_compress_kernel, nslot=nslot),
        grid=(batch, NSA_GROUPS),
        in_specs=[pl.BlockSpec((seq, HEAD_DIM), lambda b, g: (b, kc_col + g)),
                  pl.BlockSpec((seq, HEAD_DIM), lambda b, g: (b, vc_col + g)),
                  full((NSA_CMP_LEN, HEAD_DIM)), full((NSA_CMP_LEN, HEAD_DIM)),
                  full((NSA_CMP_LEN, HEAD_DIM, HEAD_DIM)), full((NSA_CMP_LEN, HEAD_DIM, HEAD_DIM)),
                  full((1, HEAD_DIM)),
                  pl.BlockSpec((seq, HEAD_DIM), lambda b, g: (b, 0)),
                  pl.BlockSpec((seq, HEAD_DIM), lambda b, g: (b, 0))],
        out_specs=[out_spec, out_spec],
        out_shape=[out_shape, out_shape],
        compiler_params=_params(2),
        name="nsa_compress",
    )(p, p, pe_k, pe_v, phi_k, phi_v, _row1(g_kc), cos, sin)


NSA_QUERY_TILE = 256


def _nsa_kernel(q_ref, gl_ref, kc_ref, vc_ref, ks_ref, vs_ref, kw_ref, vw_ref, cos_ref, sin_ref,
                gq_ref, gks_ref, gkw_ref, o_ref, ksb_ref, vsb_ref, kwb_ref, vwb_ref, *, tq, seq):
    hg = NSA_HG
    nslot = seq // NSA_CMP_STRIDE
    topn = min(NSA_SEL_TOPN, seq // NSA_SEL_BLOCK)

    @pl.when(pl.program_id(2) == 0)
    def _():
        _prep_keys(ks_ref, gks_ref[...], cos_ref, sin_ref, ksb_ref, seq)
        _prep_keys(kw_ref, gkw_ref[...], cos_ref, sin_ref, kwb_ref, seq)
        vsb_ref[...] = vs_ref[...].astype(BF16)
        vwb_ref[...] = vw_ref[...].astype(BF16)

    def tile_heads(x):
        return jnp.concatenate([x] * hg, axis=0)

    gate_shift = (HEAD_DIM - 3 * hg * pl.program_id(1)) % HEAD_DIM
    row = lax.broadcasted_iota(jnp.int32, (tq, tq), 0)
    col = lax.broadcasted_iota(jnp.int32, (tq, tq), 1)
    causal = tile_heads(jnp.where(col <= row, 0.0, -jnp.inf))
    window_edge = tile_heads(jnp.where(col > row, 0.0, -jnp.inf))
    per_sel = NSA_SEL_BLOCK // NSA_CMP_STRIDE
    span = NSA_CMP_LEN // NSA_CMP_STRIDE
    nback = NSA_WINDOW // tq

    def query_tile(i):
        q0 = i * tq
        own = slice(q0, q0 + tq)
        cos, sin, gq = cos_ref[own, :], sin_ref[own, :], gq_ref[...]
        qs = []
        for h in range(hg):
            qh = _rope(_rms(q_ref[:, h * HEAD_DIM:(h + 1) * HEAD_DIM], gq), cos, sin)
            qs.append((qh * SCALE).astype(BF16))
        qst = jnp.concatenate(qs, axis=0)

        slot = lax.broadcasted_iota(jnp.int32, (tq, nslot), 1)
        tpos = q0 + lax.broadcasted_iota(jnp.int32, (tq, nslot), 0)
        valid = (slot >= 1) & (slot * NSA_CMP_STRIDE + (NSA_CMP_STRIDE - 1) <= tpos)
        s_c = _nt(qst, kc_ref[0, 0]) + tile_heads(jnp.where(valid, 0.0, -jnp.inf))
        m_c = jnp.max(s_c, axis=-1, keepdims=True)
        m_c = jnp.where(m_c == -jnp.inf, 0.0, m_c)
        e_c = jnp.exp(s_c - m_c)
        p_c = e_c / jnp.maximum(jnp.sum(e_c, axis=-1, keepdims=True), TINY)
        o_c = jnp.dot(p_c.astype(BF16), vc_ref[0, 0], preferred_element_type=F32)

        ncand = (q0 + tq) // NSA_SEL_BLOCK
        sel = None
        if ncand > topn:
            p_sum = p_c[0:tq]
            for h in range(1, hg):
                p_sum = p_sum + p_c[h * tq:(h + 1) * tq]
            b_idx = lax.broadcasted_iota(jnp.int32, (ncand, nslot), 0)
            j_idx = lax.broadcasted_iota(jnp.int32, (ncand, nslot), 1)
            overlap = ((j_idx >= 1) & (j_idx - 1 > per_sel * b_idx - span)
                       & (j_idx - 1 < per_sel * (b_idx + 1)))
            imp = _nt(overlap.astype(F32), p_sum, precision=HIGHEST)
            blk = lax.broadcasted_iota(jnp.int32, (ncand, tq), 0)
            cur = (q0 + lax.broadcasted_iota(jnp.int32, (ncand, tq), 1)) >> NSA_SEL_SHIFT
            forced = (blk == cur) | (blk == 0)
            imp = jnp.where(forced, jnp.inf, jnp.where(blk <= cur, imp, -jnp.inf))
            sel = _to_columns((_top_rank(imp) < topn).astype(F32)).astype(BF16)

        scores, values = [], []
        for d in range(min(nback, i) + 1):
            keys = slice(q0 - d * tq, q0 - (d - 1) * tq)
            s = _nt(qst, kwb_ref[keys, :])
            if d == 0:
                s = s + causal
            elif d == nback:
                s = s + window_edge
            scores.append(s)
            values.append(vwb_ref[keys, :])
        acc_w, _, l_w = _softmax_chunks(scores, values)
        o_w = acc_w / l_w

        scores, values = [], []
        for c in range(i + 1):
            keys = slice(c * tq, (c + 1) * tq)
            s = _nt(qst, ksb_ref[keys, :])
            if sel is not None:
                e_row = lax.broadcasted_iota(jnp.int32, (ncand, tq), 0)
                e_col = lax.broadcasted_iota(jnp.int32, (ncand, tq), 1)
                expand = (((c * tq + e_col) >> NSA_SEL_SHIFT) == e_row).astype(BF16)
                keep = jnp.dot(sel, expand, preferred_element_type=F32) > 0.5
                if c == i:
                    keep = keep & (col <= row)
                s = s + tile_heads(jnp.where(keep, 0.0, -jnp.inf))
            elif c == i:
                s = s + causal
            scores.append(s)
            values.append(vsb_ref[keys, :])
        acc_s, _, l_s = _softmax_chunks(scores, values)
        o_s = acc_s / l_s

        gates = jax.nn.sigmoid(pltpu.roll(gl_ref[...], gate_shift, 1))
        for h in range(hg):
            hs = slice(h * tq, (h + 1) * tq)
            out = (gates[:, 3 * h:3 * h + 1] * o_c[hs] + gates[:, 3 * h + 1:3 * h + 2] * o_s[hs]
                   + gates[:, 3 * h + 2:3 * h + 3] * o_w[hs])
            o_ref[:, h * HEAD_DIM:(h + 1) * HEAD_DIM] = out.astype(o_ref.dtype)

    for i in range(seq // tq):
        pl.when(pl.program_id(2) == i)(functools.partial(query_tile, i))


def nsa_attention(p, q_col, ks_col, vs_col, kw_col, vw_col, gl_col, kcmp, vcmp, gq, gks, gkw, cos, sin,
                  batch, seq):
    tq = min(NSA_QUERY_TILE, seq)
    nq = seq // tq
    gw = NSA_HG * HEAD_DIM
    nslot = seq // NSA_CMP_STRIDE
    head = lambda c0: pl.BlockSpec((seq, HEAD_DIM), lambda b, g, i: (b, c0 + g))
    table = pl.BlockSpec((seq, HEAD_DIM), lambda b, g, i: (b, 0))
    row_spec = pl.BlockSpec((1, HEAD_DIM), lambda b, g, i: (0, 0))
    cmp_spec = pl.BlockSpec((1, 1, nslot, HEAD_DIM), lambda b, g, i: (b, g, 0, 0))
    return pl.pallas_call(
        functools.partial(_nsa_kernel, tq=tq, seq=seq),
        grid=(batch, NSA_GROUPS, nq),
        in_specs=[pl.BlockSpec((tq, gw), lambda b, g, i: (b * nq + i, q_col // NSA_HG + g)),
                  pl.BlockSpec((tq, HEAD_DIM), lambda b, g, i: (b * nq + i, gl_col)),
                  cmp_spec, cmp_spec,
                  head(ks_col), head(vs_col), head(kw_col), head(vw_col), table, table,
                  row_spec, row_spec, row_spec],
        out_specs=pl.BlockSpec((tq, gw), lambda b, g, i: (b * nq + i, g)),
        out_shape=jax.ShapeDtypeStruct((batch * seq, NSA_HEADS * HEAD_DIM), BF16),
        scratch_shapes=[pltpu.VMEM((seq, HEAD_DIM), BF16)] * 4,
        compiler_params=_params(3),
        name="nsa",
    )(p, p, kcmp, vcmp, p, p, p, p, cos, sin, _row1(gq), _row1(gks), _row1(gkw))


def _sb_kernel(q_ref, k_ref, v_ref, o_ref, kb_ref, vb_ref, run_ref, acc_ref, *, tq, tk):
    qi = pl.program_id(2)
    per_q = tq // tk

    @pl.when(qi == 0)
    def _():
        kb_ref[...] = k_ref[...].astype(BF16)
        vb_ref[...] = v_ref[...].astype(BF16)

    later = (lax.broadcasted_iota(jnp.int32, (tk, tk), 0)
             > lax.broadcasted_iota(jnp.int32, (tk, tk), 1)).astype(BF16)
    row = lax.broadcasted_iota(jnp.int32, (tq, tk), 0)
    col = lax.broadcasted_iota(jnp.int32, (tq, tk), 1)
    qb = (q_ref[...] * SCALE).astype(BF16)

    def tiles(first, key0s):
        keys = [pl.ds(pl.multiple_of((first - j) * tk, tk), tk) for j in range(len(key0s))]
        zs = [_nt(qb, kb_ref[kk, :]) for kk in keys]
        log_betas, log_rests, stricts = [], [], []
        for z, key0 in zip(zs, key0s):
            log_beta = jnp.minimum(z, 0.0) - jnp.log(1.0 + jnp.exp(-jnp.abs(z)))
            log_rest = log_beta - z
            strict = None
            if key0 is not None:
                strict = key0 + col < row
                log_rest = jnp.where(strict, log_rest, 0.0)
            log_betas.append(log_beta)
            log_rests.append(log_rest)
            stricts.append(strict)
        insides = []
        for log_rest in log_rests:
            hi = log_rest.astype(BF16)
            lo = (log_rest - hi.astype(F32)).astype(BF16)
            both = jnp.dot(jnp.concatenate([hi, lo], axis=0), later, preferred_element_type=F32)
            insides.append(both[:tq] + both[tq:])
        run = run_ref[...]
        acc = acc_ref[...]
        for kk, log_beta, log_rest, strict, inside in zip(keys, log_betas, log_rests, stricts, insides):
            a = jnp.exp(log_beta + inside + jnp.concatenate([run] * (tk // HEAD_DIM), axis=1))
            if strict is not None:
                a = jnp.where(strict, a, 0.0)
            acc = acc + jnp.dot(a.astype(BF16), vb_ref[kk, :], preferred_element_type=F32)
            run = run + jnp.sum(log_rest, axis=-1, keepdims=True)
        run_ref[...] = run
        acc_ref[...] = acc

    run_ref[...] = jnp.zeros_like(run_ref)
    acc_ref[...] = jnp.zeros_like(acc_ref)
    tiles((qi + 1) * per_q - 1, [(per_q - 1 - j) * tk for j in range(per_q)])

    def step(i, carry):
        tiles((qi - i) * per_q - 1, [None] * per_q)
        return carry

    lax.fori_loop(0, qi, step, 0)
    o_ref[...] = acc_ref[...].astype(o_ref.dtype)


def sb_attention(p, q_col, k_col, v_col, batch, seq, tq=512, tk=256):
    tq = min(tq, seq)
    tk = min(tk, tq)
    nq = seq // tq
    head = lambda c0: pl.BlockSpec((seq, HEAD_DIM), lambda b, h, i: (b, c0 + h))
    return pl.pallas_call(
        functools.partial(_sb_kernel, tq=tq, tk=tk),
        grid=(batch, SB_HEADS, nq),
        in_specs=[pl.BlockSpec((tq, HEAD_DIM), lambda b, h, i: (b * nq + i, q_col + h)),
                  head(k_col), head(v_col)],
        out_specs=pl.BlockSpec((tq, HEAD_DIM), lambda b, h, i: (b * nq + i, h)),
        out_shape=jax.ShapeDtypeStruct((batch * seq, SB_HEADS * HEAD_DIM), BF16),
        scratch_shapes=[pltpu.VMEM((seq, HEAD_DIM), BF16), pltpu.VMEM((seq, HEAD_DIM), BF16),
                        pltpu.VMEM((tq, HEAD_DIM), F32), pltpu.VMEM((tq, HEAD_DIM), F32)],
        compiler_params=_params(3),
        name="stick_breaking",
    )(p, p, p)


def _dil_kernel(q0_ref, q1_ref, q2_ref, k0_ref, k1_ref, k2_ref, v0_ref, v1_ref, v2_ref,
                cos_ref, sin_ref, gq_ref, gk_ref, o_ref,
                tmp_ref, qd_ref, kd_ref, vd_ref, og0_ref, og1_ref, og2_ref, lse0_ref, lse1_ref, lse2_ref,
                *, seq):
    q_refs, k_refs, v_refs = (q0_ref, q1_ref, q2_ref), (k0_ref, k1_ref, k2_ref), (v0_ref, v1_ref, v2_ref)
    og_refs, lse_refs = (og0_ref, og1_ref, og2_ref), (lse0_ref, lse1_ref, lse2_ref)
    ta = DIL_SPAN
    row = lax.broadcasted_iota(jnp.int32, (ta, ta), 0)
    col = lax.broadcasted_iota(jnp.int32, (ta, ta), 1)

    for gi, (window, dil) in enumerate(DIL_CONFIGS):
        assert window == dil * DIL_SPAN and seq % (dil * ta) == 0
        n_a = seq // dil
        tiles_per_class = n_a // ta

        def class_major(dst_ref, rows, val, dil=dil, n_a=n_a):
            if dil == 1:
                dst_ref[rows, :] = val.astype(BF16)
                return
            tmp_ref[rows, :] = val
            per = (rows.stop - rows.start) // dil
            a0 = rows.start // dil
            for rho in range(dil):
                src = pl.ds(rows.start + rho, per, stride=dil)
                dst_ref[rho * n_a + a0:rho * n_a + a0 + per, :] = tmp_ref[src, :].astype(BF16)

        for r0 in range(0, seq, PREP_ROWS):
            rows = slice(r0, r0 + PREP_ROWS)
            cos, sin = cos_ref[rows, :], sin_ref[rows, :]
            class_major(qd_ref, rows, _rope(_rms(q_refs[gi][rows, :], gq_ref[...]), cos, sin) * SCALE)
            class_major(kd_ref, rows, _rope(_rms(k_refs[gi][rows, :], gk_ref[...]), cos, sin))
            class_major(vd_ref, rows, v_refs[gi][rows, :])

        for j in range(seq // ta):
            rho, at = divmod(j, tiles_per_class)
            rows = slice(j * ta, (j + 1) * ta)
            qj = qd_ref[rows, :]
            scores = [jnp.where(col <= row, _nt(qj, kd_ref[rows, :]), -jnp.inf)]
            values = [vd_ref[rows, :]]
            if at > 0:
                prev = slice((j - 1) * ta, j * ta)
                scores.append(jnp.where(col >= row, _nt(qj, kd_ref[prev, :]), -jnp.inf))
                values.append(vd_ref[prev, :])
            acc, m, l = _softmax_chunks(scores, values)
            tokens = pl.ds(dil * at * ta + rho, ta, stride=dil) if dil > 1 else rows
            og_refs[gi][tokens, :] = acc / l
            lse_refs[gi][tokens, :] = jnp.broadcast_to(m + jnp.log(l), (ta, HEAD_DIM))

    for r0 in range(0, seq, PREP_ROWS):
        rows = slice(r0, r0 + PREP_ROWS)
        lses = [ref[rows, :] for ref in lse_refs]
        top = functools.reduce(jnp.maximum, lses)
        ws = [jnp.exp(x - top) for x in lses]
        total = functools.reduce(lambda a, b: a + b, ws)
        out = functools.reduce(lambda a, b: a + b, [(w / total) * ref[rows, :] for w, ref in zip(ws, og_refs)])
        o_ref[rows, :] = out.astype(o_ref.dtype)


def dilated_attention(p, q_col, k_col, v_col, gq, gk, cos, sin, batch, seq):
    head = lambda c0: pl.BlockSpec((seq, HEAD_DIM), lambda b, h: (b, c0 + h))
    table = pl.BlockSpec((seq, HEAD_DIM), lambda b, h: (b, 0))
    row_spec = pl.BlockSpec((1, HEAD_DIM), lambda b, h: (0, 0))
    groups = range(DIL_GROUPS)
    return pl.pallas_call(
        functools.partial(_dil_kernel, seq=seq),
        grid=(batch, DIL_HEADS),
        in_specs=([head(q_col + gi * DIL_HEADS) for gi in groups]
                  + [head(k_col + gi * DIL_HEADS) for gi in groups]
                  + [head(v_col + gi * DIL_HEADS) for gi in groups]
                  + [table, table, row_spec, row_spec]),
        out_specs=pl.BlockSpec((seq, HEAD_DIM), lambda b, h: (b, h)),
        out_shape=jax.ShapeDtypeStruct((batch * seq, DIL_HEADS * HEAD_DIM), BF16),
        scratch_shapes=[pltpu.VMEM((seq, HEAD_DIM), F32)]
                       + [pltpu.VMEM((seq, HEAD_DIM), BF16)] * 3
                       + [pltpu.VMEM((seq, HEAD_DIM), F32)] * (2 * DIL_GROUPS),
        compiler_params=_params(2),
        name="dilated",
    )(*([p] * 9), cos, sin, _row1(gq), _row1(gk))


def _xattn_kernel(q_ref, kv_ref, gq_ref, gk_ref, o_ref):
    gq, gk = gq_ref[...], gk_ref[...]
    width = XATTN_HEADS * HEAD_DIM
    for h in range(XATTN_HEADS):
        hs = slice(h * HEAD_DIM, (h + 1) * HEAD_DIM)
        q = (_rms(q_ref[:, hs], gq) * SCALE).astype(BF16)
        k = _rms(kv_ref[:, hs], gk).astype(BF16)
        v = kv_ref[:, width + h * HEAD_DIM:width + (h + 1) * HEAD_DIM].astype(BF16)
        s = _nt(q, k)
        e = jnp.exp(s - jnp.max(s, axis=-1, keepdims=True))
        p = e / jnp.sum(e, axis=-1, keepdims=True)
        o_ref[:, hs] = jnp.dot(p.astype(BF16), v, preferred_element_type=F32).astype(o_ref.dtype)


def memory_cross_attention(q, kv, gq, gk, batch, seq, mem_len, tq=512):
    tq = min(tq, seq)
    nq = seq // tq
    width = XATTN_HEADS * HEAD_DIM
    return pl.pallas_call(
        _xattn_kernel,
        grid=(batch, nq),
        in_specs=[pl.BlockSpec((tq, width), lambda b, i: (b * nq + i, 0)),
                  pl.BlockSpec((mem_len, 2 * width), lambda b, i: (b, 0)),
                  pl.BlockSpec((1, HEAD_DIM), lambda b, i: (0, 0)),
                  pl.BlockSpec((1, HEAD_DIM), lambda b, i: (0, 0))],
        out_specs=pl.BlockSpec((tq, width), lambda b, i: (b * nq + i, 0)),
        out_shape=jax.ShapeDtypeStruct((batch * seq, width), BF16),
        compiler_params=_params(2),
        name="xattn",
    )(q, kv, _row1(gq), _row1(gk))


def _even_mixer(x2, h, cos, sin, batch, seq, e, w_in, w_out, moba_gq, moba_gk, nsa_gq, nsa_gk_cmp,
                nsa_gk_slc, nsa_gk_win, pe_k, pe_v, phi_k, phi_v):
    hd = HEAD_DIM
    main = (3 * MOBA_HEADS + NSA_HEADS + 6 * NSA_GROUPS) * hd
    p = matmul(h, w_in, layer=e, tm=BIG_ROW_TILE)

    o_a = moba_attention(p, 0, 16, 32, moba_gq, moba_gk, cos, sin, batch, seq)
    kcmp, vcmp = nsa_compress(p, 64, 68, pe_k, pe_v, phi_k, phi_v, nsa_gk_cmp, cos, sin, batch, seq)
    o_b = nsa_attention(p, 48, 72, 76, 80, 84, main // hd, kcmp, vcmp, nsa_gq, nsa_gk_slc, nsa_gk_win,
                        cos, sin, batch, seq)

    return matmul([o_a, o_b], w_out, layer=e, residual=x2)


def _odd_mixer(x2, h, cos, sin, batch, seq, o, w_in, w_out, dil_gq, dil_gk):
    hd = HEAD_DIM
    p = matmul(h, w_in, layer=o, tm=BIG_ROW_TILE)
    o_c = sb_attention(p, 0, SB_HEADS, 2 * SB_HEADS, batch, seq)
    nd = DIL_GROUPS * DIL_HEADS
    o_d = dilated_attention(p, 3 * SB_HEADS, 3 * SB_HEADS + nd, 3 * SB_HEADS + 2 * nd, dil_gq, dil_gk,
                            cos, sin, batch, seq)
    return matmul([o_c, o_d], w_out, layer=o, residual=x2)


def kernel(x, mem, positions, mix_norm, even_w_in, even_w_out, moba_gq, moba_gk, nsa_gq, nsa_gk_cmp, nsa_gk_slc, nsa_gk_win, nsa_pe_k, nsa_pe_v, nsa_phi_k, nsa_phi_v, odd_w_in, odd_w_out, dil_gq, dil_gk, xattn_norm, mem_norm, xattn_wq, xattn_wkv, xattn_wo, xattn_gq, xattn_gk, ffn_norm, ffn_wg, ffn_wu, ffn_wd):
    batch, seq, d = x.shape
    mem_len = mem.shape[1]
    depth = mix_norm.shape[0]
    x2 = x.reshape(batch * seq, d)
    mem2 = mem.reshape(batch * mem_len, d)
    cos, sin = rope_tables(positions)

    for layer in range(depth):
        h = rmsnorm_rows(x2, mix_norm[layer])
        if layer % 2 == 0:
            e = layer // 2
            x2 = _even_mixer(x2, h, cos, sin, batch, seq, e, even_w_in, even_w_out, moba_gq[e],
                             moba_gk[e], nsa_gq[e], nsa_gk_cmp[e], nsa_gk_slc[e], nsa_gk_win[e],
                             nsa_pe_k[e], nsa_pe_v[e], nsa_phi_k[e], nsa_phi_v[e])
        else:
            o = layer // 2
            x2 = _odd_mixer(x2, h, cos, sin, batch, seq, o, odd_w_in, odd_w_out, dil_gq[o], dil_gk[o])

        h = rmsnorm_rows(x2, xattn_norm[layer])
        mem_n = rmsnorm_rows(mem2, mem_norm[layer])
        q = matmul(h, xattn_wq, layer=layer)
        kv = matmul(mem_n, xattn_wkv, layer=layer)
        o_x = memory_cross_attention(q, kv, xattn_gq[layer], xattn_gk[layer], batch, seq, mem_len)
        x2 = matmul(o_x, xattn_wo, layer=layer, residual=x2)

        h = rmsnorm_rows(x2, ffn_norm[layer])
        hidden = swiglu_gate_up(h, ffn_wg, ffn_wu, layer)
        x2 = matmul(hidden, ffn_wd, layer=layer, residual=x2, tm=BIG_ROW_TILE)

    return x2.reshape(batch, seq, d)
```

```python
import functools

import jax
import jax.numpy as jnp
from jax import lax
from jax.experimental import pallas as pl
from jax.experimental.pallas import tpu as pltpu

F32 = jnp.float32
BF16 = jnp.bfloat16
HIGHEST = lax.Precision.HIGHEST

HEAD_DIM = 128
HALF = HEAD_DIM // 2
ROPE_THETA = 10000.0
NORM_EPS = 1e-6
TINY = 1e-30
SCALE = HEAD_DIM ** -0.5

MOBA_HEADS = 16
MOBA_BLOCK = 256
MOBA_TOPK = 3
NSA_HEADS = 16
NSA_GROUPS = 4
NSA_HG = NSA_HEADS // NSA_GROUPS
NSA_CMP_LEN = 32
NSA_CMP_STRIDE = 16
NSA_SEL_BLOCK = 64
NSA_SEL_SHIFT = NSA_SEL_BLOCK.bit_length() - 1
NSA_SEL_TOPN = 16
NSA_WINDOW = 512
DIL_CONFIGS = ((128, 1), (512, 4), (2048, 16))
DIL_GROUPS = len(DIL_CONFIGS)
DIL_HEADS = 8
DIL_SPAN = 128
SB_HEADS = 24
XATTN_HEADS = 4

VMEM_LIMIT_BYTES = 56 * 1024 * 1024
NT_DIMS = (((1,), (1,)), ((), ()))
PREP_ROWS = 256
MATMUL_ROW_TILE = 1024
BIG_ROW_TILE = 2048


def _params(n_grid):
    return pltpu.CompilerParams(dimension_semantics=("arbitrary",) * n_grid,
                                vmem_limit_bytes=VMEM_LIMIT_BYTES)


def _nt(a, b, precision=None):
    return lax.dot_general(a, b, NT_DIMS, precision=precision, preferred_element_type=F32)


def _rms(x, g):
    return x * lax.rsqrt(jnp.mean(x * x, axis=-1, keepdims=True) + NORM_EPS) * g


def _rope(x, cos, sin_signed):
    return x * cos + pltpu.roll(x, HALF, 1) * sin_signed


def _softmax_chunks(scores, values):
    m = jnp.max(functools.reduce(jnp.maximum, scores), axis=-1, keepdims=True)
    probs = [jnp.exp(s - m) for s in scores]
    l = jnp.sum(functools.reduce(lambda a, b: a + b, probs), axis=-1, keepdims=True)
    acc = None
    for p, v in zip(probs, values):
        part = jnp.dot(p.astype(BF16), v, preferred_element_type=F32)
        acc = part if acc is None else acc + part
    return acc, m, l


def _top_rank(vals):
    idx = lax.broadcasted_iota(jnp.int32, vals.shape, 0)
    rank = jnp.zeros(vals.shape, jnp.int32)
    for m in range(vals.shape[0]):
        c = vals[m:m + 1, :]
        ahead = (c > vals) | ((c == vals) & (idx > m))
        rank = rank + ahead.astype(jnp.int32)
    return rank


def _to_columns(x_t):
    rows = x_t.shape[1]
    eye = (lax.broadcasted_iota(jnp.int32, (rows, rows), 0)
           == lax.broadcasted_iota(jnp.int32, (rows, rows), 1)).astype(BF16)
    return _nt(eye, x_t.astype(BF16))


def _rmsnorm_kernel(x_ref, g_ref, o_ref):
    o_ref[...] = _rms(x_ref[...], g_ref[...]).astype(o_ref.dtype)


def rmsnorm_rows(x, g, tm=256):
    m, d = x.shape
    tm = min(tm, m)
    return pl.pallas_call(
        _rmsnorm_kernel,
        grid=(m // tm,),
        in_specs=[pl.BlockSpec((tm, d), lambda i: (i, 0)), pl.BlockSpec((1, d), lambda i: (0, 0))],
        out_specs=pl.BlockSpec((tm, d), lambda i: (i, 0)),
        out_shape=jax.ShapeDtypeStruct((m, d), BF16),
        compiler_params=_params(1),
        name="rmsnorm",
    )(x, g.reshape(1, d))


def _matmul_kernel(*refs, n_lhs, has_res):
    a_refs, w_refs = refs[:n_lhs], refs[n_lhs:2 * n_lhs]
    out = None
    for a_ref, w_ref in zip(a_refs, w_refs):
        part = jnp.dot(a_ref[...], w_ref[...].astype(BF16), preferred_element_type=F32)
        out = part if out is None else out + part
    if has_res:
        out = out + refs[2 * n_lhs][...]
    refs[-1][...] = out.astype(refs[-1].dtype)


def _contraction_tile(kdim, limit=6144):
    if kdim <= limit:
        return kdim
    return max(t for t in range(HEAD_DIM, limit + 1, HEAD_DIM) if kdim % t == 0)


def _matmul_call(pieces, w, layer, n, residual, out_dtype, tm):
    m = pieces[0][0].shape[0]
    tn = 512 if sum(k for _, _, k, _ in pieces) <= 4096 else 256
    tm, tn = min(tm, m), min(tn, n)
    assert m % tm == 0
    lhs_mode = dict(pipeline_mode=pl.Buffered(1)) if tm > MATMUL_ROW_TILE else {}
    a_specs, w_specs = [], []
    for _, col0, k, row0 in pieces:
        assert col0 % k == 0 and row0 % k == 0
        a_specs.append(pl.BlockSpec((tm, k), lambda i, j, c=col0 // k: (i, c), **lhs_mode))
        if layer is None:
            w_specs.append(pl.BlockSpec((k, tn), lambda i, j, r=row0 // k: (r, j)))
        else:
            w_specs.append(pl.BlockSpec((None, k, tn), lambda i, j, r=row0 // k: (layer, r, j)))
    args = [a for a, _, _, _ in pieces] + [w] * len(pieces)
    in_specs = a_specs + w_specs
    if residual is not None:
        in_specs.append(pl.BlockSpec((tm, tn), lambda i, j: (i, j)))
        args.append(residual)
    return pl.pallas_call(
        functools.partial(_matmul_kernel, n_lhs=len(pieces), has_res=residual is not None),
        grid=(m // tm, pl.cdiv(n, tn)),
        in_specs=in_specs,
        out_specs=pl.BlockSpec((tm, tn), lambda i, j: (i, j)),
        out_shape=jax.ShapeDtypeStruct((m, n), out_dtype),
        compiler_params=_params(2),
        name="matmul",
    )(*args)


def matmul(lhs, w, layer=None, residual=None, out_dtype=F32, tm=MATMUL_ROW_TILE):
    arrays = lhs if isinstance(lhs, (list, tuple)) else [lhs]
    n = w.shape[-1]
    if len(arrays) > 1:
        pieces, row0 = [], 0
        for a in arrays:
            pieces.append((a, 0, a.shape[1], row0))
            row0 += a.shape[1]
        return _matmul_call(pieces, w, layer, n, residual, out_dtype, tm)
    a = arrays[0]
    kdim = a.shape[1]
    tk = _contraction_tile(kdim)
    out = residual
    for c in range(kdim // tk):
        last = c == kdim // tk - 1
        out = _matmul_call([(a, c * tk, tk, c * tk)], w, layer, n, out, out_dtype if last else F32, tm)
    return out


def _matmul_norm_kernel(a_ref, w_ref, r_ref, g_ref, x_ref, h_ref):
    x = jnp.dot(a_ref[...], w_ref[...].astype(BF16), preferred_element_type=F32) + r_ref[...]
    x_ref[...] = x
    h_ref[...] = _rms(x, g_ref[...]).astype(h_ref.dtype)


def matmul_residual_norm(a, w, layer, residual, gain, tm=256):
    m, kdim = a.shape
    n = w.shape[-1]
    tm = min(tm, m)
    rows = lambda width: pl.BlockSpec((tm, width), lambda i: (i, 0))
    return pl.pallas_call(
        _matmul_norm_kernel,
        grid=(m // tm,),
        in_specs=[rows(kdim), pl.BlockSpec((None, kdim, n), lambda i: (layer, 0, 0)), rows(n),
                  pl.BlockSpec((1, n), lambda i: (0, 0))],
        out_specs=[rows(n), rows(n)],
        out_shape=[jax.ShapeDtypeStruct((m, n), F32), jax.ShapeDtypeStruct((m, n), BF16)],
        compiler_params=_params(1),
        name="matmul_norm",
    )(a, w, residual, gain.reshape(1, n))


def _gate_up_kernel(a_ref, wg_ref, wu_ref, o_ref):
    a = a_ref[...]
    g = jnp.dot(a, wg_ref[...].astype(BF16), preferred_element_type=F32)
    u = jnp.dot(a, wu_ref[...].astype(BF16), preferred_element_type=F32)
    o_ref[...] = (g * jax.nn.sigmoid(g) * u).astype(o_ref.dtype)


def swiglu_gate_up(a, wg, wu, layer, tm=BIG_ROW_TILE, tn=256):
    m, kdim = a.shape
    n = wg.shape[-1]
    tm, tn = min(tm, m), min(tn, n)
    w_spec = pl.BlockSpec((None, kdim, tn), lambda i, j: (layer, 0, j))
    lhs_mode = dict(pipeline_mode=pl.Buffered(1)) if tm > MATMUL_ROW_TILE else {}
    return pl.pallas_call(
        _gate_up_kernel,
        grid=(m // tm, pl.cdiv(n, tn)),
        in_specs=[pl.BlockSpec((tm, kdim), lambda i, j: (i, 0), **lhs_mode), w_spec, w_spec],
        out_specs=pl.BlockSpec((tm, tn), lambda i, j: (i, j)),
        out_shape=jax.ShapeDtypeStruct((m, n), BF16),
        compiler_params=_params(2),
        name="swiglu_gate_up",
    )(a, wg, wu)


def _rope_table_kernel(pos_ref, invf_ref, cos_ref, sin_ref):
    ang = pos_ref[...] * invf_ref[...]
    lane = lax.broadcasted_iota(jnp.int32, ang.shape, 1)
    s = jnp.sin(ang)
    cos_ref[...] = jnp.cos(ang)
    sin_ref[...] = jnp.where(lane < HALF, -s, s)


def rope_tables(positions, tr=256):
    n = positions.size
    tr = min(tr, n)
    inv_freq = ROPE_THETA ** (-jnp.arange(HALF, dtype=F32) / HALF)
    invf = jnp.concatenate([inv_freq, inv_freq]).reshape(1, HEAD_DIM)
    pos = jnp.broadcast_to(positions.astype(F32).reshape(n, 1), (n, HEAD_DIM))
    spec = pl.BlockSpec((tr, HEAD_DIM), lambda i: (i, 0))
    return pl.pallas_call(
        _rope_table_kernel,
        grid=(n // tr,),
        in_specs=[spec, pl.BlockSpec((1, HEAD_DIM), lambda i: (0, 0))],
        out_specs=[spec, spec],
        out_shape=[jax.ShapeDtypeStruct((n, HEAD_DIM), F32)] * 2,
        compiler_params=_params(1),
        name="rope_tables",
    )(pos, invf)


def _prep_keys(k_ref, g, cos_ref, sin_ref, out_ref, seq):
    for r0 in range(0, seq, PREP_ROWS):
        rows = slice(r0, min(r0 + PREP_ROWS, seq))
        out_ref[rows, :] = _rope(_rms(k_ref[rows, :], g), cos_ref[rows, :], sin_ref[rows, :]).astype(BF16)


def _row1(x):
    return x.reshape(1, HEAD_DIM)


def _moba_kernel(q_ref, k_ref, v_ref, cos_ref, sin_ref, gq_ref, gk_ref, o_ref, kb_ref, vb_ref, *, nblk):
    blk = MOBA_BLOCK
    means = []
    for n in range(nblk):
        rows = slice(n * blk, (n + 1) * blk)
        k = _rope(_rms(k_ref[rows, :], gk_ref[...]), cos_ref[rows, :], sin_ref[rows, :])
        kb_ref[rows, :] = k.astype(BF16)
        means.append(jnp.mean(k, axis=0, keepdims=True))
    vb_ref[...] = v_ref[...].astype(BF16)
    row = lax.broadcasted_iota(jnp.int32, (blk, blk), 0)
    col = lax.broadcasted_iota(jnp.int32, (blk, blk), 1)

    for i in range(nblk):
        own = slice(i * blk, (i + 1) * blk)
        q = _rope(_rms(q_ref[own, :], gq_ref[...]), cos_ref[own, :], sin_ref[own, :])
        qb = (q * SCALE).astype(BF16)
        scores = [jnp.where(col <= row, _nt(qb, kb_ref[own, :]), -jnp.inf)]
        values = [vb_ref[own, :]]
        if i > MOBA_TOPK:
            gate = _nt(jnp.concatenate(means[:i], axis=0), q, precision=HIGHEST)
            sel = _to_columns((_top_rank(gate) < MOBA_TOPK).astype(F32))
        for n in range(i):
            rows = slice(n * blk, (n + 1) * blk)
            s = _nt(qb, kb_ref[rows, :])
            if i > MOBA_TOPK:
                s = jnp.where(sel[:, n:n + 1] > 0.5, s, -jnp.inf)
            scores.append(s)
            values.append(vb_ref[rows, :])
        acc, _, l = _softmax_chunks(scores, values)
        o_ref[own, :] = (acc / l).astype(o_ref.dtype)


def moba_attention(p, q_col, k_col, v_col, gq, gk, cos, sin, batch, seq):
    nblk = seq // MOBA_BLOCK
    head = lambda c0: pl.BlockSpec((seq, HEAD_DIM), lambda b, h: (b, c0 + h))
    table = pl.BlockSpec((seq, HEAD_DIM), lambda b, h: (b, 0))
    row_spec = pl.BlockSpec((1, HEAD_DIM), lambda b, h: (0, 0))
    return pl.pallas_call(
        functools.partial(_moba_kernel, nblk=nblk),
        grid=(batch, MOBA_HEADS),
        in_specs=[head(q_col), head(k_col), head(v_col), table, table, row_spec, row_spec],
        out_specs=pl.BlockSpec((seq, HEAD_DIM), lambda b, h: (b, h)),
        out_shape=jax.ShapeDtypeStruct((batch * seq, MOBA_HEADS * HEAD_DIM), BF16),
        scratch_shapes=[pltpu.VMEM((seq, HEAD_DIM), BF16), pltpu.VMEM((seq, HEAD_DIM), BF16)],
        compiler_params=_params(2),
        name="moba",
    )(p, p, p, cos, sin, _row1(gq), _row1(gk))


def _nsa_compress_kernel(kc_ref, vc_ref, pek_ref, pev_ref, phik_ref, phiv_ref, g_ref, cos_ref, sin_ref,
                         ko_ref, vo_ref, *, nslot):
    half = NSA_CMP_LEN // 2

    def compress(x_ref, pe_ref, phi_ref):
        first = jnp.zeros((nslot, HEAD_DIM), F32)
        second = jnp.zeros((nslot, HEAD_DIM), F32)
        for l in range(half):
            xl = x_ref[pl.ds(l, nslot, stride=NSA_CMP_STRIDE), :]
            first += jnp.dot((xl + pe_ref[l:l + 1, :]).astype(BF16), phi_ref[l].astype(BF16),
                             preferred_element_type=F32)
            second += jnp.dot((xl + pe_ref[half + l:half + l + 1, :]).astype(BF16),
                              phi_ref[half + l].astype(BF16), preferred_element_type=F32)
        return pltpu.roll(first, 1, 0) + second

    ends = pl.ds(NSA_CMP_STRIDE - 1, nslot, stride=NSA_CMP_STRIDE)
    kc = compress(kc_ref, pek_ref, phik_ref)
    kc = _rope(_rms(kc, g_ref[...]), cos_ref[ends, :], sin_ref[ends, :])
    ko_ref[0, 0] = kc.astype(BF16)
    vo_ref[0, 0] = compress(vc_ref, pev_ref, phiv_ref).astype(BF16)


def nsa_compress(p, kc_col, vc_col, pe_k, pe_v, phi_k, phi_v, g_kc, cos, sin, batch, seq):
    nslot = seq // NSA_CMP_STRIDE
    full = lambda shape: pl.BlockSpec(shape, lambda b, g: (0,) * len(shape))
    out_spec = pl.BlockSpec((1, 1, nslot, HEAD_DIM), lambda b, g: (b, g, 0, 0))
    out_shape = jax.ShapeDtypeStruct((batch, NSA_GROUPS, nslot, HEAD_DIM), BF16)
    return pl.pallas_call(
        functools.partial(_nsa_compress_kernel, nslot=nslot),
        grid=(batch, NSA_GROUPS),
        in_specs=[pl.BlockSpec((seq, HEAD_DIM), lambda b, g: (b, kc_col + g)),
                  pl.BlockSpec((seq, HEAD_DIM), lambda b, g: (b, vc_col + g)),
                  full((NSA_CMP_LEN, HEAD_DIM)), full((NSA_CMP_LEN, HEAD_DIM)),
                  full((NSA_CMP_LEN, HEAD_DIM, HEAD_DIM)), full((NSA_CMP_LEN, HEAD_DIM, HEAD_DIM)),
                  full((1, HEAD_DIM)),
                  pl.BlockSpec((seq, HEAD_DIM), lambda b, g: (b, 0)),
                  pl.BlockSpec((seq, HEAD_DIM), lambda b, g: (b, 0))],
        out_specs=[out_spec, out_spec],
        out_shape=[out_shape, out_shape],
        compiler_params=_params(2),
        name="nsa_compress",
    )(p, p, pe_k, pe_v, phi_k, phi_v, _row1(g_kc), cos, sin)


NSA_QUERY_TILE = 256


def _nsa_kernel(q_ref, gl_ref, kc_ref, vc_ref, ks_ref, vs_ref, kw_ref, vw_ref, cos_ref, sin_ref,
                gq_ref, gks_ref, gkw_ref, o_ref, ksb_ref, vsb_ref, kwb_ref, vwb_ref,
                s_ref, top_ref, sum_ref, acc_ref, *, tq, seq):
    hg = NSA_HG
    nslot = seq // NSA_CMP_STRIDE
    nsel = seq // NSA_SEL_BLOCK
    topn = min(NSA_SEL_TOPN, nsel)
    qi = pl.program_id(2)
    q0 = qi * tq

    @pl.when(qi == 0)
    def _():
        _prep_keys(ks_ref, gks_ref[...], cos_ref, sin_ref, ksb_ref, seq)
        _prep_keys(kw_ref, gkw_ref[...], cos_ref, sin_ref, kwb_ref, seq)
        vsb_ref[...] = vs_ref[...].astype(BF16)
        vwb_ref[...] = vw_ref[...].astype(BF16)

    def tile_heads(x):
        return jnp.concatenate([x] * hg, axis=0)

    gate_shift = (HEAD_DIM - 3 * hg * pl.program_id(1)) % HEAD_DIM
    row = lax.broadcasted_iota(jnp.int32, (tq, tq), 0)
    col = lax.broadcasted_iota(jnp.int32, (tq, tq), 1)
    per_sel = NSA_SEL_BLOCK // NSA_CMP_STRIDE
    span = NSA_CMP_LEN // NSA_CMP_STRIDE
    nback = NSA_WINDOW // tq

    own = pl.ds(pl.multiple_of(q0, tq), tq)
    cos, sin, gq = cos_ref[own, :], sin_ref[own, :], gq_ref[...]
    qs = []
    for h in range(hg):
        qh = _rope(_rms(q_ref[:, h * HEAD_DIM:(h + 1) * HEAD_DIM], gq), cos, sin)
        qs.append((qh * SCALE).astype(BF16))
    qst = jnp.concatenate(qs, axis=0)

    slot = lax.broadcasted_iota(jnp.int32, (tq, nslot), 1)
    tpos = q0 + lax.broadcasted_iota(jnp.int32, (tq, nslot), 0)
    valid = (slot >= 1) & (slot * NSA_CMP_STRIDE + (NSA_CMP_STRIDE - 1) <= tpos)
    s_c = _nt(qst, kc_ref[0, 0]) + tile_heads(jnp.where(valid, 0.0, -jnp.inf))
    m_c = jnp.max(s_c, axis=-1, keepdims=True)
    m_c = jnp.where(m_c == -jnp.inf, 0.0, m_c)
    e_c = jnp.exp(s_c - m_c)
    p_c = e_c / jnp.maximum(jnp.sum(e_c, axis=-1, keepdims=True), TINY)
    o_c = jnp.dot(p_c.astype(BF16), vc_ref[0, 0], preferred_element_type=F32)

    p_sum = p_c[0:tq]
    for h in range(1, hg):
        p_sum = p_sum + p_c[h * tq:(h + 1) * tq]
    b_idx = lax.broadcasted_iota(jnp.int32, (nsel, nslot), 0)
    j_idx = lax.broadcasted_iota(jnp.int32, (nsel, nslot), 1)
    overlap = ((j_idx >= 1) & (j_idx - 1 > per_sel * b_idx - span) & (j_idx - 1 < per_sel * (b_idx + 1)))
    imp = _nt(overlap.astype(F32), p_sum, precision=HIGHEST)
    blk = lax.broadcasted_iota(jnp.int32, (nsel, tq), 0)
    cur = (q0 + lax.broadcasted_iota(jnp.int32, (nsel, tq), 1)) >> NSA_SEL_SHIFT
    forced = (blk == cur) | (blk == 0)
    imp = jnp.where(forced, jnp.inf, jnp.where(blk <= cur, imp, -jnp.inf))
    sel = _to_columns((_top_rank(imp) < topn).astype(F32)).astype(BF16)

    scores, values = [], []
    for d in range(nback + 1):
        keys = pl.ds(pl.multiple_of(jnp.maximum(qi - d, 0) * tq, tq), tq)
        s = _nt(qst, kwb_ref[keys, :])
        exists = jnp.where(qi >= d, 0.0, -jnp.inf)
        if d == 0:
            s = s + tile_heads(jnp.where(col <= row, 0.0, -jnp.inf))
        elif d == nback:
            s = s + tile_heads(jnp.where(col > row, exists, -jnp.inf))
        else:
            s = s + exists
        scores.append(s)
        values.append(vwb_ref[keys, :])
    acc_w, _, l_w = _softmax_chunks(scores, values)
    o_w = acc_w / l_w

    e_row = lax.broadcasted_iota(jnp.int32, (nsel, tq), 0)
    e_col = lax.broadcasted_iota(jnp.int32, (nsel, tq), 1)
    top_ref[...] = jnp.full(top_ref.shape, -jnp.inf, F32)

    def score_chunk(c, carry):
        keys = pl.ds(pl.multiple_of(c * tq, tq), tq)
        expand = (((c * tq + e_col) >> NSA_SEL_SHIFT) == e_row).astype(BF16)
        keep = (jnp.dot(sel, expand, preferred_element_type=F32) > 0.5) & (c * tq + col <= q0 + row)
        s = _nt(qst, ksb_ref[keys, :]) + tile_heads(jnp.where(keep, 0.0, -jnp.inf))
        s_ref[c] = s
        top_ref[...] = jnp.maximum(top_ref[...], s)
        return carry

    lax.fori_loop(0, qi + 1, score_chunk, 0)
    m_s = jnp.max(top_ref[...], axis=-1, keepdims=True)
    sum_ref[...] = jnp.zeros(sum_ref.shape, F32)
    acc_ref[...] = jnp.zeros(acc_ref.shape, F32)

    def weigh_chunk(c, carry):
        keys = pl.ds(pl.multiple_of(c * tq, tq), tq)
        p = jnp.exp(s_ref[c] - m_s)
        sum_ref[...] += p
        acc_ref[...] += jnp.dot(p.astype(BF16), vsb_ref[keys, :], preferred_element_type=F32)
        return carry

    lax.fori_loop(0, qi + 1, weigh_chunk, 0)
    o_s = acc_ref[...] / jnp.sum(sum_ref[...], axis=-1, keepdims=True)

    gates = jax.nn.sigmoid(pltpu.roll(gl_ref[...], gate_shift, 1))
    for h in range(hg):
        hs = slice(h * tq, (h + 1) * tq)
        out = (gates[:, 3 * h:3 * h + 1] * o_c[hs] + gates[:, 3 * h + 1:3 * h + 2] * o_s[hs]
               + gates[:, 3 * h + 2:3 * h + 3] * o_w[hs])
        o_ref[:, h * HEAD_DIM:(h + 1) * HEAD_DIM] = out.astype(o_ref.dtype)


def nsa_attention(p, q_col, ks_col, vs_col, kw_col, vw_col, gl_col, kcmp, vcmp, gq, gks, gkw, cos, sin,
                  batch, seq):
    tq = min(NSA_QUERY_TILE, seq)
    nq = seq // tq
    gw = NSA_HG * HEAD_DIM
    nslot = seq // NSA_CMP_STRIDE
    head = lambda c0: pl.BlockSpec((seq, HEAD_DIM), lambda b, g, i: (b, c0 + g))
    table = pl.BlockSpec((seq, HEAD_DIM), lambda b, g, i: (b, 0))
    row_spec = pl.BlockSpec((1, HEAD_DIM), lambda b, g, i: (0, 0))
    cmp_spec = pl.BlockSpec((1, 1, nslot, HEAD_DIM), lambda b, g, i: (b, g, 0, 0))
    return pl.pallas_call(
        functools.partial(_nsa_kernel, tq=tq, seq=seq),
        grid=(batch, NSA_GROUPS, nq),
        in_specs=[pl.BlockSpec((tq, gw), lambda b, g, i: (b * nq + i, q_col // NSA_HG + g)),
                  pl.BlockSpec((tq, HEAD_DIM), lambda b, g, i: (b * nq + i, gl_col)),
                  cmp_spec, cmp_spec,
                  head(ks_col), head(vs_col), head(kw_col), head(vw_col), table, table,
                  row_spec, row_spec, row_spec],
        out_specs=pl.BlockSpec((tq, gw), lambda b, g, i: (b * nq + i, g)),
        out_shape=jax.ShapeDtypeStruct((batch * seq, NSA_HEADS * HEAD_DIM), BF16),
        scratch_shapes=[pltpu.VMEM((seq, HEAD_DIM), BF16)] * 4
                       + [pltpu.VMEM((nq, NSA_HG * tq, tq), F32),
                          pltpu.VMEM((NSA_HG * tq, tq), F32), pltpu.VMEM((NSA_HG * tq, tq), F32),
                          pltpu.VMEM((NSA_HG * tq, HEAD_DIM), F32)],
        compiler_params=_params(3),
        name="nsa",
    )(p, p, kcmp, vcmp, p, p, p, p, cos, sin, _row1(gq), _row1(gks), _row1(gkw))


def _sb_kernel(q_ref, k_ref, v_ref, o_ref, kb_ref, vb_ref, run_ref, acc_ref, *, tq, tk):
    qi = pl.program_id(2)
    per_q = tq // tk

    @pl.when(qi == 0)
    def _():
        kb_ref[...] = k_ref[...].astype(BF16)
        vb_ref[...] = v_ref[...].astype(BF16)

    later = (lax.broadcasted_iota(jnp.int32, (tk, tk), 0)
             > lax.broadcasted_iota(jnp.int32, (tk, tk), 1)).astype(BF16)
    row = lax.broadcasted_iota(jnp.int32, (tq, tk), 0)
    col = lax.broadcasted_iota(jnp.int32, (tq, tk), 1)
    qb = (q_ref[...] * SCALE).astype(BF16)

    def tiles(first, key0s):
        keys = [pl.ds(pl.multiple_of((first - j) * tk, tk), tk) for j in range(len(key0s))]
        zs = [_nt(qb, kb_ref[kk, :]) for kk in keys]
        log_betas, log_rests, stricts = [], [], []
        for z, key0 in zip(zs, key0s):
            log_beta = jnp.minimum(z, 0.0) - jnp.log(1.0 + jnp.exp(-jnp.abs(z)))
            log_rest = log_beta - z
            strict = None
            if key0 is not None:
                strict = key0 + col < row
                log_rest = jnp.where(strict, log_rest, 0.0)
            log_betas.append(log_beta)
            log_rests.append(log_rest)
            stricts.append(strict)
        insides = []
        for log_rest in log_rests:
            hi = log_rest.astype(BF16)
            lo = (log_rest - hi.astype(F32)).astype(BF16)
            both = jnp.dot(jnp.concatenate([hi, lo], axis=0), later, preferred_element_type=F32)
            insides.append(both[:tq] + both[tq:])
        run = run_ref[...]
        acc = acc_ref[...]
        for kk, log_beta, log_rest, strict, inside in zip(keys, log_betas, log_rests, stricts, insides):
            a = jnp.exp(log_beta + inside + jnp.concatenate([run] * (tk // HEAD_DIM), axis=1))
            if strict is not None:
                a = jnp.where(strict, a, 0.0)
            acc = acc + jnp.dot(a.astype(BF16), vb_ref[kk, :], preferred_element_type=F32)
            run = run + jnp.sum(log_rest, axis=-1, keepdims=True)
        run_ref[...] = run
        acc_ref[...] = acc

    run_ref[...] = jnp.zeros_like(run_ref)
    acc_ref[...] = jnp.zeros_like(acc_ref)
    tiles((qi + 1) * per_q - 1, [(per_q - 1 - j) * tk for j in range(per_q)])

    def step(i, carry):
        tiles((qi - i) * per_q - 1, [None] * per_q)
        return carry

    lax.fori_loop(0, qi, step, 0)
    o_ref[...] = acc_ref[...].astype(o_ref.dtype)


def sb_attention(p, q_col, k_col, v_col, batch, seq, tq=512, tk=256):
    tq = min(tq, seq)
    tk = min(tk, tq)
    nq = seq // tq
    head = lambda c0: pl.BlockSpec((seq, HEAD_DIM), lambda b, h, i: (b, c0 + h))
    return pl.pallas_call(
        functools.partial(_sb_kernel, tq=tq, tk=tk),
        grid=(batch, SB_HEADS, nq),
        in_specs=[pl.BlockSpec((tq, HEAD_DIM), lambda b, h, i: (b * nq + i, q_col + h)),
                  head(k_col), head(v_col)],
        out_specs=pl.BlockSpec((tq, HEAD_DIM), lambda b, h, i: (b * nq + i, h)),
        out_shape=jax.ShapeDtypeStruct((batch * seq, SB_HEADS * HEAD_DIM), BF16),
        scratch_shapes=[pltpu.VMEM((seq, HEAD_DIM), BF16), pltpu.VMEM((seq, HEAD_DIM), BF16),
                        pltpu.VMEM((tq, HEAD_DIM), F32), pltpu.VMEM((tq, HEAD_DIM), F32)],
        compiler_params=_params(3),
        name="stick_breaking",
    )(p, p, p)


def _dil_kernel(q0_ref, q1_ref, q2_ref, k0_ref, k1_ref, k2_ref, v0_ref, v1_ref, v2_ref,
                cos_ref, sin_ref, gq_ref, gk_ref, o_ref,
                tmp_ref, qd_ref, kd_ref, vd_ref, og0_ref, og1_ref, og2_ref, lse0_ref, lse1_ref, lse2_ref,
                *, seq):
    q_refs, k_refs, v_refs = (q0_ref, q1_ref, q2_ref), (k0_ref, k1_ref, k2_ref), (v0_ref, v1_ref, v2_ref)
    og_refs, lse_refs = (og0_ref, og1_ref, og2_ref), (lse0_ref, lse1_ref, lse2_ref)
    ta = DIL_SPAN
    row = lax.broadcasted_iota(jnp.int32, (ta, ta), 0)
    col = lax.broadcasted_iota(jnp.int32, (ta, ta), 1)

    for gi, (window, dil) in enumerate(DIL_CONFIGS):
        assert window == dil * DIL_SPAN and seq % (dil * ta) == 0
        n_a = seq // dil
        tiles_per_class = n_a // ta

        def class_major(dst_ref, rows, val, dil=dil, n_a=n_a):
            if dil == 1:
                dst_ref[rows, :] = val.astype(BF16)
                return
            tmp_ref[rows, :] = val
            per = (rows.stop - rows.start) // dil
            a0 = rows.start // dil
            for rho in range(dil):
                src = pl.ds(rows.start + rho, per, stride=dil)
                dst_ref[rho * n_a + a0:rho * n_a + a0 + per, :] = tmp_ref[src, :].astype(BF16)

        for r0 in range(0, seq, PREP_ROWS):
            rows = slice(r0, r0 + PREP_ROWS)
            cos, sin = cos_ref[rows, :], sin_ref[rows, :]
            class_major(qd_ref, rows, _rope(_rms(q_refs[gi][rows, :], gq_ref[...]), cos, sin) * SCALE)
            class_major(kd_ref, rows, _rope(_rms(k_refs[gi][rows, :], gk_ref[...]), cos, sin))
            class_major(vd_ref, rows, v_refs[gi][rows, :])

        for j in range(seq // ta):
            rho, at = divmod(j, tiles_per_class)
            rows = slice(j * ta, (j + 1) * ta)
            qj = qd_ref[rows, :]
            scores = [jnp.where(col <= row, _nt(qj, kd_ref[rows, :]), -jnp.inf)]
            values = [vd_ref[rows, :]]
            if at > 0:
                prev = slice((j - 1) * ta, j * ta)
                scores.append(jnp.where(col >= row, _nt(qj, kd_ref[prev, :]), -jnp.inf))
                values.append(vd_ref[prev, :])
            acc, m, l = _softmax_chunks(scores, values)
            tokens = pl.ds(dil * at * ta + rho, ta, stride=dil) if dil > 1 else rows
            og_refs[gi][tokens, :] = acc / l
            lse_refs[gi][tokens, :] = jnp.broadcast_to(m + jnp.log(l), (ta, HEAD_DIM))

    for r0 in range(0, seq, PREP_ROWS):
        rows = slice(r0, r0 + PREP_ROWS)
        lses = [ref[rows, :] for ref in lse_refs]
        top = functools.reduce(jnp.maximum, lses)
        ws = [jnp.exp(x - top) for x in lses]
        total = functools.reduce(lambda a, b: a + b, ws)
        out = functools.reduce(lambda a, b: a + b, [(w / total) * ref[rows, :] for w, ref in zip(ws, og_refs)])
        o_ref[rows, :] = out.astype(o_ref.dtype)


def dilated_attention(p, q_col, k_col, v_col, gq, gk, cos, sin, batch, seq):
    head = lambda c0: pl.BlockSpec((seq, HEAD_DIM), lambda b, h: (b, c0 + h))
    table = pl.BlockSpec((seq, HEAD_DIM), lambda b, h: (b, 0))
    row_spec = pl.BlockSpec((1, HEAD_DIM), lambda b, h: (0, 0))
    groups = range(DIL_GROUPS)
    return pl.pallas_call(
        functools.partial(_dil_kernel, seq=seq),
        grid=(batch, DIL_HEADS),
        in_specs=([head(q_col + gi * DIL_HEADS) for gi in groups]
                  + [head(k_col + gi * DIL_HEADS) for gi in groups]
                  + [head(v_col + gi * DIL_HEADS) for gi in groups]
                  + [table, table, row_spec, row_spec]),
        out_specs=pl.BlockSpec((seq, HEAD_DIM), lambda b, h: (b, h)),
        out_shape=jax.ShapeDtypeStruct((batch * seq, DIL_HEADS * HEAD_DIM), BF16),
        scratch_shapes=[pltpu.VMEM((seq, HEAD_DIM), F32)]
                       + [pltpu.VMEM((seq, HEAD_DIM), BF16)] * 3
                       + [pltpu.VMEM((seq, HEAD_DIM), F32)] * (2 * DIL_GROUPS),
        compiler_params=_params(2),
        name="dilated",
    )(*([p] * 9), cos, sin, _row1(gq), _row1(gk))


def _xattn_kernel(q_ref, kv_ref, gq_ref, gk_ref, o_ref):
    gq, gk = gq_ref[...], gk_ref[...]
    width = XATTN_HEADS * HEAD_DIM
    for h in range(XATTN_HEADS):
        hs = slice(h * HEAD_DIM, (h + 1) * HEAD_DIM)
        q = (_rms(q_ref[:, hs], gq) * SCALE).astype(BF16)
        k = _rms(kv_ref[:, hs], gk).astype(BF16)
        v = kv_ref[:, width + h * HEAD_DIM:width + (h + 1) * HEAD_DIM].astype(BF16)
        s = _nt(q, k)
        e = jnp.exp(s - jnp.max(s, axis=-1, keepdims=True))
        p = e / jnp.sum(e, axis=-1, keepdims=True)
        o_ref[:, hs] = jnp.dot(p.astype(BF16), v, preferred_element_type=F32).astype(o_ref.dtype)


def memory_cross_attention(q, kv, gq, gk, batch, seq, mem_len, tq=512):
    tq = min(tq, seq)
    nq = seq // tq
    width = XATTN_HEADS * HEAD_DIM
    return pl.pallas_call(
        _xattn_kernel,
        grid=(batch, nq),
        in_specs=[pl.BlockSpec((tq, width), lambda b, i: (b * nq + i, 0)),
                  pl.BlockSpec((mem_len, 2 * width), lambda b, i: (b, 0)),
                  pl.BlockSpec((1, HEAD_DIM), lambda b, i: (0, 0)),
                  pl.BlockSpec((1, HEAD_DIM), lambda b, i: (0, 0))],
        out_specs=pl.BlockSpec((tq, width), lambda b, i: (b * nq + i, 0)),
        out_shape=jax.ShapeDtypeStruct((batch * seq, width), BF16),
        compiler_params=_params(2),
        name="xattn",
    )(q, kv, _row1(gq), _row1(gk))


def _even_mixer(x2, h, cos, sin, batch, seq, e, w_in, w_out, moba_gq, moba_gk, nsa_gq, nsa_gk_cmp,
                nsa_gk_slc, nsa_gk_win, pe_k, pe_v, phi_k, phi_v):
    hd = HEAD_DIM
    main = (3 * MOBA_HEADS + NSA_HEADS + 6 * NSA_GROUPS) * hd
    pad = -w_in.shape[-1] % hd
    p = matmul(h, jnp.pad(w_in[e].astype(BF16), ((0, 0), (0, pad))), tm=BIG_ROW_TILE)

    o_a = moba_attention(p, 0, 16, 32, moba_gq, moba_gk, cos, sin, batch, seq)
    kcmp, vcmp = nsa_compress(p, 64, 68, pe_k, pe_v, phi_k, phi_v, nsa_gk_cmp, cos, sin, batch, seq)
    o_b = nsa_attention(p, 48, 72, 76, 80, 84, main // hd, kcmp, vcmp, nsa_gq, nsa_gk_slc, nsa_gk_win,
                        cos, sin, batch, seq)

    return matmul([o_a, o_b], w_out, layer=e, residual=x2)


def _odd_mixer(x2, h, cos, sin, batch, seq, o, w_in, w_out, dil_gq, dil_gk):
    hd = HEAD_DIM
    p = matmul(h, w_in, layer=o, tm=BIG_ROW_TILE)
    o_c = sb_attention(p, 0, SB_HEADS, 2 * SB_HEADS, batch, seq)
    nd = DIL_GROUPS * DIL_HEADS
    o_d = dilated_attention(p, 3 * SB_HEADS, 3 * SB_HEADS + nd, 3 * SB_HEADS + 2 * nd, dil_gq, dil_gk,
                            cos, sin, batch, seq)
    return matmul([o_c, o_d], w_out, layer=o, residual=x2)


def kernel(x, mem, positions, mix_norm, even_w_in, even_w_out, moba_gq, moba_gk, nsa_gq, nsa_gk_cmp, nsa_gk_slc, nsa_gk_win, nsa_pe_k, nsa_pe_v, nsa_phi_k, nsa_phi_v, odd_w_in, odd_w_out, dil_gq, dil_gk, xattn_norm, mem_norm, xattn_wq, xattn_wkv, xattn_wo, xattn_gq, xattn_gk, ffn_norm, ffn_wg, ffn_wu, ffn_wd):
    batch, seq, d = x.shape
    mem_len = mem.shape[1]
    depth = mix_norm.shape[0]
    x2 = x.reshape(batch * seq, d)
    mem2 = mem.reshape(batch * mem_len, d)
    cos, sin = rope_tables(positions)

    for layer in range(depth):
        h = rmsnorm_rows(x2, mix_norm[layer])
        if layer % 2 == 0:
            e = layer // 2
            x2 = _even_mixer(x2, h, cos, sin, batch, seq, e, even_w_in, even_w_out, moba_gq[e],
                             moba_gk[e], nsa_gq[e], nsa_gk_cmp[e], nsa_gk_slc[e], nsa_gk_win[e],
                             nsa_pe_k[e], nsa_pe_v[e], nsa_phi_k[e], nsa_phi_v[e])
        else:
            o = layer // 2
            x2 = _odd_mixer(x2, h, cos, sin, batch, seq, o, odd_w_in, odd_w_out, dil_gq[o], dil_gk[o])

        h = rmsnorm_rows(x2, xattn_norm[layer])
        mem_n = rmsnorm_rows(mem2, mem_norm[layer])
        q = matmul(h, xattn_wq, layer=layer)
        kv = matmul(mem_n, xattn_wkv, layer=layer)
        o_x = memory_cross_attention(q, kv, xattn_gq[layer], xattn_gk[layer], batch, seq, mem_len)
        x2, h = matmul_residual_norm(o_x, xattn_wo, layer, x2, ffn_norm[layer])
        hidden = swiglu_gate_up(h, ffn_wg, ffn_wu, layer)
        x2 = matmul(hidden, ffn_wd, layer=layer, residual=x2, tm=BIG_ROW_TILE)

    return x2.reshape(batch, seq, d)
```

```python
import functools

import jax
import jax.numpy as jnp
from jax import lax
from jax.experimental import pallas as pl
from jax.experimental.pallas import tpu as pltpu

F32 = jnp.float32
BF16 = jnp.bfloat16
HIGHEST = lax.Precision.HIGHEST

HEAD_DIM = 128
HALF = HEAD_DIM // 2
ROPE_THETA = 10000.0
NORM_EPS = 1e-6
TINY = 1e-30
SCALE = HEAD_DIM ** -0.5

MOBA_HEADS = 16
MOBA_BLOCK = 256
MOBA_TOPK = 3
NSA_HEADS = 16
NSA_GROUPS = 4
NSA_HG = NSA_HEADS // NSA_GROUPS
NSA_CMP_LEN = 32
NSA_CMP_STRIDE = 16
NSA_SEL_BLOCK = 64
NSA_SEL_SHIFT = NSA_SEL_BLOCK.bit_length() - 1
NSA_SEL_TOPN = 16
NSA_WINDOW = 512
DIL_CONFIGS = ((128, 1), (512, 4), (2048, 16))
DIL_GROUPS = len(DIL_CONFIGS)
DIL_HEADS = 8
DIL_SPAN = 128
SB_HEADS = 24
XATTN_HEADS = 4

VMEM_LIMIT_BYTES = 56 * 1024 * 1024
NT_DIMS = (((1,), (1,)), ((), ()))
PREP_ROWS = 256
MATMUL_ROW_TILE = 1024
BIG_ROW_TILE = 2048


def _params(n_grid):
    return pltpu.CompilerParams(dimension_semantics=("arbitrary",) * n_grid,
                                vmem_limit_bytes=VMEM_LIMIT_BYTES)


def _nt(a, b, precision=None):
    return lax.dot_general(a, b, NT_DIMS, precision=precision, preferred_element_type=F32)


def _rms(x, g):
    return x * lax.rsqrt(jnp.mean(x * x, axis=-1, keepdims=True) + NORM_EPS) * g


def _rope(x, cos, sin_signed):
    return x * cos + pltpu.roll(x, HALF, 1) * sin_signed


def _softmax_chunks(scores, values):
    m = jnp.max(functools.reduce(jnp.maximum, scores), axis=-1, keepdims=True)
    probs = [jnp.exp(s - m) for s in scores]
    l = jnp.sum(functools.reduce(lambda a, b: a + b, probs), axis=-1, keepdims=True)
    acc = None
    for p, v in zip(probs, values):
        part = jnp.dot(p.astype(BF16), v, preferred_element_type=F32)
        acc = part if acc is None else acc + part
    return acc, m, l


def _top_rank(vals):
    idx = lax.broadcasted_iota(jnp.int32, vals.shape, 0)
    rank = jnp.zeros(vals.shape, jnp.int32)
    for m in range(vals.shape[0]):
        c = vals[m:m + 1, :]
        ahead = (c > vals) | ((c == vals) & (idx > m))
        rank = rank + ahead.astype(jnp.int32)
    return rank


def _to_columns(x_t):
    rows = x_t.shape[1]
    eye = (lax.broadcasted_iota(jnp.int32, (rows, rows), 0)
           == lax.broadcasted_iota(jnp.int32, (rows, rows), 1)).astype(BF16)
    return _nt(eye, x_t.astype(BF16))


def _rmsnorm_kernel(x_ref, g_ref, o_ref):
    o_ref[...] = _rms(x_ref[...], g_ref[...]).astype(o_ref.dtype)


def rmsnorm_rows(x, g, tm=256):
    m, d = x.shape
    tm = min(tm, m)
    return pl.pallas_call(
        _rmsnorm_kernel,
        grid=(m // tm,),
        in_specs=[pl.BlockSpec((tm, d), lambda i: (i, 0)), pl.BlockSpec((1, d), lambda i: (0, 0))],
        out_specs=pl.BlockSpec((tm, d), lambda i: (i, 0)),
        out_shape=jax.ShapeDtypeStruct((m, d), BF16),
        compiler_params=_params(1),
        name="rmsnorm",
    )(x, g.reshape(1, d))


def _matmul_kernel(*refs, n_lhs, has_res):
    a_refs, w_refs = refs[:n_lhs], refs[n_lhs:2 * n_lhs]
    out = None
    for a_ref, w_ref in zip(a_refs, w_refs):
        part = jnp.dot(a_ref[...], w_ref[...].astype(BF16), preferred_element_type=F32)
        out = part if out is None else out + part
    if has_res:
        out = out + refs[2 * n_lhs][...]
    refs[-1][...] = out.astype(refs[-1].dtype)


def _contraction_tile(kdim, limit=6144):
    if kdim <= limit:
        return kdim
    return max(t for t in range(HEAD_DIM, limit + 1, HEAD_DIM) if kdim % t == 0)


def _matmul_call(pieces, w, layer, n, residual, out_dtype, tm):
    m = pieces[0][0].shape[0]
    tn = 512 if sum(k for _, _, k, _ in pieces) <= 4096 else 256
    tm, tn = min(tm, m), min(tn, n)
    assert m % tm == 0
    lhs_mode = dict(pipeline_mode=pl.Buffered(1)) if tm > MATMUL_ROW_TILE else {}
    a_specs, w_specs = [], []
    for _, col0, k, row0 in pieces:
        assert col0 % k == 0 and row0 % k == 0
        a_specs.append(pl.BlockSpec((tm, k), lambda i, j, c=col0 // k: (i, c), **lhs_mode))
        if layer is None:
            w_specs.append(pl.BlockSpec((k, tn), lambda i, j, r=row0 // k: (r, j)))
        else:
            w_specs.append(pl.BlockSpec((None, k, tn), lambda i, j, r=row0 // k: (layer, r, j)))
    args = [a for a, _, _, _ in pieces] + [w] * len(pieces)
    in_specs = a_specs + w_specs
    if residual is not None:
        in_specs.append(pl.BlockSpec((tm, tn), lambda i, j: (i, j)))
        args.append(residual)
    return pl.pallas_call(
        functools.partial(_matmul_kernel, n_lhs=len(pieces), has_res=residual is not None),
        grid=(m // tm, pl.cdiv(n, tn)),
        in_specs=in_specs,
        out_specs=pl.BlockSpec((tm, tn), lambda i, j: (i, j)),
        out_shape=jax.ShapeDtypeStruct((m, n), out_dtype),
        compiler_params=_params(2),
        name="matmul",
    )(*args)


def matmul(lhs, w, layer=None, residual=None, out_dtype=F32, tm=MATMUL_ROW_TILE):
    arrays = lhs if isinstance(lhs, (list, tuple)) else [lhs]
    n = w.shape[-1]
    if len(arrays) > 1:
        pieces, row0 = [], 0
        for a in arrays:
            pieces.append((a, 0, a.shape[1], row0))
            row0 += a.shape[1]
        return _matmul_call(pieces, w, layer, n, residual, out_dtype, tm)
    a = arrays[0]
    kdim = a.shape[1]
    tk = _contraction_tile(kdim)
    out = residual
    for c in range(kdim // tk):
        last = c == kdim // tk - 1
        out = _matmul_call([(a, c * tk, tk, c * tk)], w, layer, n, out, out_dtype if last else F32, tm)
    return out


def _matmul_norm_kernel(a_ref, w_ref, r_ref, g_ref, x_ref, h_ref):
    x = jnp.dot(a_ref[...], w_ref[...].astype(BF16), preferred_element_type=F32) + r_ref[...]
    x_ref[...] = x
    h_ref[...] = _rms(x, g_ref[...]).astype(h_ref.dtype)


def matmul_residual_norm(a, w, layer, residual, gain, tm=256):
    m, kdim = a.shape
    n = w.shape[-1]
    tm = min(tm, m)
    rows = lambda width: pl.BlockSpec((tm, width), lambda i: (i, 0))
    return pl.pallas_call(
        _matmul_norm_kernel,
        grid=(m // tm,),
        in_specs=[rows(kdim), pl.BlockSpec((None, kdim, n), lambda i: (layer, 0, 0)), rows(n),
                  pl.BlockSpec((1, n), lambda i: (0, 0))],
        out_specs=[rows(n), rows(n)],
        out_shape=[jax.ShapeDtypeStruct((m, n), F32), jax.ShapeDtypeStruct((m, n), BF16)],
        compiler_params=_params(1),
        name="matmul_norm",
    )(a, w, residual, gain.reshape(1, n))


def _gate_up_kernel(a_ref, wg_ref, wu_ref, o_ref):
    a = a_ref[...]
    g = jnp.dot(a, wg_ref[...].astype(BF16), preferred_element_type=F32)
    u = jnp.dot(a, wu_ref[...].astype(BF16), preferred_element_type=F32)
    o_ref[...] = (g * jax.nn.sigmoid(g) * u).astype(o_ref.dtype)


def swiglu_gate_up(a, wg, wu, layer, tm=BIG_ROW_TILE, tn=256):
    m, kdim = a.shape
    n = wg.shape[-1]
    tm, tn = min(tm, m), min(tn, n)
    w_spec = pl.BlockSpec((None, kdim, tn), lambda i, j: (layer, 0, j))
    lhs_mode = dict(pipeline_mode=pl.Buffered(1)) if tm > MATMUL_ROW_TILE else {}
    return pl.pallas_call(
        _gate_up_kernel,
        grid=(m // tm, pl.cdiv(n, tn)),
        in_specs=[pl.BlockSpec((tm, kdim), lambda i, j: (i, 0), **lhs_mode), w_spec, w_spec],
        out_specs=pl.BlockSpec((tm, tn), lambda i, j: (i, j)),
        out_shape=jax.ShapeDtypeStruct((m, n), BF16),
        compiler_params=_params(2),
        name="swiglu_gate_up",
    )(a, wg, wu)


def _rope_table_kernel(pos_ref, invf_ref, cos_ref, sin_ref):
    ang = pos_ref[...] * invf_ref[...]
    lane = lax.broadcasted_iota(jnp.int32, ang.shape, 1)
    s = jnp.sin(ang)
    cos_ref[...] = jnp.cos(ang)
    sin_ref[...] = jnp.where(lane < HALF, -s, s)


def rope_tables(positions, tr=256):
    n = positions.size
    tr = min(tr, n)
    inv_freq = ROPE_THETA ** (-jnp.arange(HALF, dtype=F32) / HALF)
    invf = jnp.concatenate([inv_freq, inv_freq]).reshape(1, HEAD_DIM)
    pos = jnp.broadcast_to(positions.astype(F32).reshape(n, 1), (n, HEAD_DIM))
    spec = pl.BlockSpec((tr, HEAD_DIM), lambda i: (i, 0))
    return pl.pallas_call(
        _rope_table_kernel,
        grid=(n // tr,),
        in_specs=[spec, pl.BlockSpec((1, HEAD_DIM), lambda i: (0, 0))],
        out_specs=[spec, spec],
        out_shape=[jax.ShapeDtypeStruct((n, HEAD_DIM), F32)] * 2,
        compiler_params=_params(1),
        name="rope_tables",
    )(pos, invf)


def _prep_keys(k_ref, g, cos_ref, sin_ref, out_ref, seq):
    for r0 in range(0, seq, PREP_ROWS):
        rows = slice(r0, min(r0 + PREP_ROWS, seq))
        out_ref[rows, :] = _rope(_rms(k_ref[rows, :], g), cos_ref[rows, :], sin_ref[rows, :]).astype(BF16)


def _row1(x):
    return x.reshape(1, HEAD_DIM)


def _moba_kernel(q_ref, k_ref, v_ref, cos_ref, sin_ref, gq_ref, gk_ref, o_ref, kb_ref, vb_ref, *, nblk):
    blk = MOBA_BLOCK
    means = []
    for n in range(nblk):
        rows = slice(n * blk, (n + 1) * blk)
        k = _rope(_rms(k_ref[rows, :], gk_ref[...]), cos_ref[rows, :], sin_ref[rows, :])
        kb_ref[rows, :] = k.astype(BF16)
        means.append(jnp.mean(k, axis=0, keepdims=True))
    vb_ref[...] = v_ref[...].astype(BF16)
    row = lax.broadcasted_iota(jnp.int32, (blk, blk), 0)
    col = lax.broadcasted_iota(jnp.int32, (blk, blk), 1)

    for i in range(nblk):
        own = slice(i * blk, (i + 1) * blk)
        q = _rope(_rms(q_ref[own, :], gq_ref[...]), cos_ref[own, :], sin_ref[own, :])
        qb = (q * SCALE).astype(BF16)
        scores = [jnp.where(col <= row, _nt(qb, kb_ref[own, :]), -jnp.inf)]
        values = [vb_ref[own, :]]
        if i > MOBA_TOPK:
            gate = _nt(jnp.concatenate(means[:i], axis=0), q, precision=HIGHEST)
            sel = _to_columns((_top_rank(gate) < MOBA_TOPK).astype(F32))
        for n in range(i):
            rows = slice(n * blk, (n + 1) * blk)
            s = _nt(qb, kb_ref[rows, :])
            if i > MOBA_TOPK:
                s = jnp.where(sel[:, n:n + 1] > 0.5, s, -jnp.inf)
            scores.append(s)
            values.append(vb_ref[rows, :])
        acc, _, l = _softmax_chunks(scores, values)
        o_ref[own, :] = (acc / l).astype(o_ref.dtype)


def moba_attention(p, q_col, k_col, v_col, gq, gk, cos, sin, batch, seq):
    nblk = seq // MOBA_BLOCK
    head = lambda c0: pl.BlockSpec((seq, HEAD_DIM), lambda b, h: (b, c0 + h))
    table = pl.BlockSpec((seq, HEAD_DIM), lambda b, h: (b, 0))
    row_spec = pl.BlockSpec((1, HEAD_DIM), lambda b, h: (0, 0))
    return pl.pallas_call(
        functools.partial(_moba_kernel, nblk=nblk),
        grid=(batch, MOBA_HEADS),
        in_specs=[head(q_col), head(k_col), head(v_col), table, table, row_spec, row_spec],
        out_specs=pl.BlockSpec((seq, HEAD_DIM), lambda b, h: (b, h)),
        out_shape=jax.ShapeDtypeStruct((batch * seq, MOBA_HEADS * HEAD_DIM), BF16),
        scratch_shapes=[pltpu.VMEM((seq, HEAD_DIM), BF16), pltpu.VMEM((seq, HEAD_DIM), BF16)],
        compiler_params=_params(2),
        name="moba",
    )(p, p, p, cos, sin, _row1(gq), _row1(gk))


def _nsa_compress_kernel(kc_ref, vc_ref, pek_ref, pev_ref, phik_ref, phiv_ref, g_ref, cos_ref, sin_ref,
                         ko_ref, vo_ref, *, nslot):
    half = NSA_CMP_LEN // 2

    def compress(x_ref, pe_ref, phi_ref):
        first = jnp.zeros((nslot, HEAD_DIM), F32)
        second = jnp.zeros((nslot, HEAD_DIM), F32)
        for l in range(half):
            xl = x_ref[pl.ds(l, nslot, stride=NSA_CMP_STRIDE), :]
            first += jnp.dot((xl + pe_ref[l:l + 1, :]).astype(BF16), phi_ref[l].astype(BF16),
                             preferred_element_type=F32)
            second += jnp.dot((xl + pe_ref[half + l:half + l + 1, :]).astype(BF16),
                              phi_ref[half + l].astype(BF16), preferred_element_type=F32)
        return pltpu.roll(first, 1, 0) + second

    ends = pl.ds(NSA_CMP_STRIDE - 1, nslot, stride=NSA_CMP_STRIDE)
    kc = compress(kc_ref, pek_ref, phik_ref)
    kc = _rope(_rms(kc, g_ref[...]), cos_ref[ends, :], sin_ref[ends, :])
    ko_ref[0, 0] = kc.astype(BF16)
    vo_ref[0, 0] = compress(vc_ref, pev_ref, phiv_ref).astype(BF16)


def nsa_compress(p, kc_col, vc_col, pe_k, pe_v, phi_k, phi_v, g_kc, cos, sin, batch, seq):
    nslot = seq // NSA_CMP_STRIDE
    full = lambda shape: pl.BlockSpec(shape, lambda b, g: (0,) * len(shape))
    out_spec = pl.BlockSpec((1, 1, nslot, HEAD_DIM), lambda b, g: (b, g, 0, 0))
    out_shape = jax.ShapeDtypeStruct((batch, NSA_GROUPS, nslot, HEAD_DIM), BF16)
    return pl.pallas_call(
        functools.partial(_nsa_compress_kernel, nslot=nslot),
        grid=(batch, NSA_GROUPS),
        in_specs=[pl.BlockSpec((seq, HEAD_DIM), lambda b, g: (b, kc_col + g)),
                  pl.BlockSpec((seq, HEAD_DIM), lambda b, g: (b, vc_col + g)),
                  full((NSA_CMP_LEN, HEAD_DIM)), full((NSA_CMP_LEN, HEAD_DIM)),
                  full((NSA_CMP_LEN, HEAD_DIM, HEAD_DIM)), full((NSA_CMP_LEN, HEAD_DIM, HEAD_DIM)),
                  full((1, HEAD_DIM)),
                  pl.BlockSpec((seq, HEAD_DIM), lambda b, g: (b, 0)),
                  pl.BlockSpec((seq, HEAD_DIM), lambda b, g: (b, 0))],
        out_specs=[out_spec, out_spec],
        out_shape=[out_shape, out_shape],
        compiler_params=_params(2),
        name="nsa_compress",
    )(p, p, pe_k, pe_v, phi_k, phi_v, _row1(g_kc), cos, sin)


NSA_QUERY_TILE = 256


def _nsa_kernel(q_ref, gl_ref, kc_ref, vc_ref, ks_ref, vs_ref, kw_ref, vw_ref, cos_ref, sin_ref,
                gq_ref, gks_ref, gkw_ref, o_ref, ksb_ref, vsb_ref, kwb_ref, vwb_ref,
                s_ref, top_ref, sum_ref, acc_ref, *, tq, seq):
    hg = NSA_HG
    nslot = seq // NSA_CMP_STRIDE
    nsel = seq // NSA_SEL_BLOCK
    topn = min(NSA_SEL_TOPN, nsel)
    qi = pl.program_id(2)
    q0 = qi * tq

    @pl.when(qi == 0)
    def _():
        _prep_keys(ks_ref, gks_ref[...], cos_ref, sin_ref, ksb_ref, seq)
        _prep_keys(kw_ref, gkw_ref[...], cos_ref, sin_ref, kwb_ref, seq)
        vsb_ref[...] = vs_ref[...].astype(BF16)
        vwb_ref[...] = vw_ref[...].astype(BF16)

    def tile_heads(x):
        return jnp.concatenate([x] * hg, axis=0)

    gate_shift = (HEAD_DIM - 3 * hg * pl.program_id(1)) % HEAD_DIM
    row = lax.broadcasted_iota(jnp.int32, (tq, tq), 0)
    col = lax.broadcasted_iota(jnp.int32, (tq, tq), 1)
    per_sel = NSA_SEL_BLOCK // NSA_CMP_STRIDE
    span = NSA_CMP_LEN // NSA_CMP_STRIDE
    nback = NSA_WINDOW // tq

    own = pl.ds(pl.multiple_of(q0, tq), tq)
    cos, sin, gq = cos_ref[own, :], sin_ref[own, :], gq_ref[...]
    qs = []
    for h in range(hg):
        qh = _rope(_rms(q_ref[:, h * HEAD_DIM:(h + 1) * HEAD_DIM], gq), cos, sin)
        qs.append((qh * SCALE).astype(BF16))
    qst = jnp.concatenate(qs, axis=0)

    slot = lax.broadcasted_iota(jnp.int32, (tq, nslot), 1)
    tpos = q0 + lax.broadcasted_iota(jnp.int32, (tq, nslot), 0)
    valid = (slot >= 1) & (slot * NSA_CMP_STRIDE + (NSA_CMP_STRIDE - 1) <= tpos)
    s_c = _nt(qst, kc_ref[0, 0]) + tile_heads(jnp.where(valid, 0.0, -jnp.inf))
    m_c = jnp.max(s_c, axis=-1, keepdims=True)
    m_c = jnp.where(m_c == -jnp.inf, 0.0, m_c)
    e_c = jnp.exp(s_c - m_c)
    p_c = e_c / jnp.maximum(jnp.sum(e_c, axis=-1, keepdims=True), TINY)
    o_c = jnp.dot(p_c.astype(BF16), vc_ref[0, 0], preferred_element_type=F32)

    p_sum = p_c[0:tq]
    for h in range(1, hg):
        p_sum = p_sum + p_c[h * tq:(h + 1) * tq]
    b_idx = lax.broadcasted_iota(jnp.int32, (nsel, nslot), 0)
    j_idx = lax.broadcasted_iota(jnp.int32, (nsel, nslot), 1)
    overlap = ((j_idx >= 1) & (j_idx - 1 > per_sel * b_idx - span) & (j_idx - 1 < per_sel * (b_idx + 1)))
    imp = _nt(overlap.astype(F32), p_sum, precision=HIGHEST)
    blk = lax.broadcasted_iota(jnp.int32, (nsel, tq), 0)
    cur = (q0 + lax.broadcasted_iota(jnp.int32, (nsel, tq), 1)) >> NSA_SEL_SHIFT
    forced = (blk == cur) | (blk == 0)
    imp = jnp.where(forced, jnp.inf, jnp.where(blk <= cur, imp, -jnp.inf))
    sel = _to_columns((_top_rank(imp) < topn).astype(F32)).astype(BF16)

    scores, values = [], []
    for d in range(nback + 1):
        keys = pl.ds(pl.multiple_of(jnp.maximum(qi - d, 0) * tq, tq), tq)
        s = _nt(qst, kwb_ref[keys, :])
        exists = jnp.where(qi >= d, 0.0, -jnp.inf)
        if d == 0:
            s = s + tile_heads(jnp.where(col <= row, 0.0, -jnp.inf))
        elif d == nback:
            s = s + tile_heads(jnp.where(col > row, exists, -jnp.inf))
        else:
            s = s + exists
        scores.append(s)
        values.append(vwb_ref[keys, :])
    acc_w, _, l_w = _softmax_chunks(scores, values)
    o_w = acc_w / l_w

    e_row = lax.broadcasted_iota(jnp.int32, (nsel, tq), 0)
    e_col = lax.broadcasted_iota(jnp.int32, (nsel, tq), 1)
    top_ref[...] = jnp.full(top_ref.shape, -jnp.inf, F32)

    def score_chunk(c, carry):
        keys = pl.ds(pl.multiple_of(c * tq, tq), tq)
        expand = (((c * tq + e_col) >> NSA_SEL_SHIFT) == e_row).astype(BF16)
        keep = (jnp.dot(sel, expand, preferred_element_type=F32) > 0.5) & (c * tq + col <= q0 + row)
        s = _nt(qst, ksb_ref[keys, :]) + tile_heads(jnp.where(keep, 0.0, -jnp.inf))
        s_ref[c] = s
        top_ref[...] = jnp.maximum(top_ref[...], s)
        return carry

    lax.fori_loop(0, qi + 1, score_chunk, 0)
    m_s = jnp.max(top_ref[...], axis=-1, keepdims=True)
    sum_ref[...] = jnp.zeros(sum_ref.shape, F32)
    acc_ref[...] = jnp.zeros(acc_ref.shape, F32)

    def weigh_chunk(c, carry):
        keys = pl.ds(pl.multiple_of(c * tq, tq), tq)
        p = jnp.exp(s_ref[c] - m_s)
        sum_ref[...] += p
        acc_ref[...] += jnp.dot(p.astype(BF16), vsb_ref[keys, :], preferred_element_type=F32)
        return carry

    lax.fori_loop(0, qi + 1, weigh_chunk, 0)
    o_s = acc_ref[...] / jnp.sum(sum_ref[...], axis=-1, keepdims=True)

    gates = jax.nn.sigmoid(pltpu.roll(gl_ref[...], gate_shift, 1))
    for h in range(hg):
        hs = slice(h * tq, (h + 1) * tq)
        out = (gates[:, 3 * h:3 * h + 1] * o_c[hs] + gates[:, 3 * h + 1:3 * h + 2] * o_s[hs]
               + gates[:, 3 * h + 2:3 * h + 3] * o_w[hs])
        o_ref[:, h * HEAD_DIM:(h + 1) * HEAD_DIM] = out.astype(o_ref.dtype)


def nsa_attention(p, q_col, ks_col, vs_col, kw_col, vw_col, gl_col, kcmp, vcmp, gq, gks, gkw, cos, sin,
                  batch, seq):
    tq = min(NSA_QUERY_TILE, seq)
    nq = seq // tq
    gw = NSA_HG * HEAD_DIM
    nslot = seq // NSA_CMP_STRIDE
    head = lambda c0: pl.BlockSpec((seq, HEAD_DIM), lambda b, g, i: (b, c0 + g))
    table = pl.BlockSpec((seq, HEAD_DIM), lambda b, g, i: (b, 0))
    row_spec = pl.BlockSpec((1, HEAD_DIM), lambda b, g, i: (0, 0))
    cmp_spec = pl.BlockSpec((1, 1, nslot, HEAD_DIM), lambda b, g, i: (b, g, 0, 0))
    return pl.pallas_call(
        functools.partial(_nsa_kernel, tq=tq, seq=seq),
        grid=(batch, NSA_GROUPS, nq),
        in_specs=[pl.BlockSpec((tq, gw), lambda b, g, i: (b * nq + i, q_col // NSA_HG + g)),
                  pl.BlockSpec((tq, HEAD_DIM), lambda b, g, i: (b * nq + i, gl_col)),
                  cmp_spec, cmp_spec,
                  head(ks_col), head(vs_col), head(kw_col), head(vw_col), table, table,
                  row_spec, row_spec, row_spec],
        out_specs=pl.BlockSpec((tq, gw), lambda b, g, i: (b * nq + i, g)),
        out_shape=jax.ShapeDtypeStruct((batch * seq, NSA_HEADS * HEAD_DIM), BF16),
        scratch_shapes=[pltpu.VMEM((seq, HEAD_DIM), BF16)] * 4
                       + [pltpu.VMEM((nq, NSA_HG * tq, tq), F32),
                          pltpu.VMEM((NSA_HG * tq, tq), F32), pltpu.VMEM((NSA_HG * tq, tq), F32),
                          pltpu.VMEM((NSA_HG * tq, HEAD_DIM), F32)],
        compiler_params=_params(3),
        name="nsa",
    )(p, p, kcmp, vcmp, p, p, p, p, cos, sin, _row1(gq), _row1(gks), _row1(gkw))


def _sb_kernel(q_ref, k_ref, v_ref, o_ref, kb_ref, vb_ref, *, tq, tk, seq):
    per_q = tq // tk
    kb_ref[...] = k_ref[...].astype(BF16)
    vb_ref[...] = v_ref[...].astype(BF16)

    later = (lax.broadcasted_iota(jnp.int32, (tk, tk), 0)
             > lax.broadcasted_iota(jnp.int32, (tk, tk), 1)).astype(BF16)
    row = lax.broadcasted_iota(jnp.int32, (tq, tk), 0)
    col = lax.broadcasted_iota(jnp.int32, (tq, tk), 1)

    def tiles(qb, first, key0s, run, acc):
        keys = [slice((first - j) * tk, (first - j + 1) * tk) for j in range(len(key0s))]
        zs = [_nt(qb, kb_ref[kk, :]) for kk in keys]
        log_betas, log_rests, stricts = [], [], []
        for z, key0 in zip(zs, key0s):
            log_beta = jnp.minimum(z, 0.0) - jnp.log(1.0 + jnp.exp(-jnp.abs(z)))
            log_rest = log_beta - z
            strict = None
            if key0 is not None:
                strict = key0 + col < row
                log_rest = jnp.where(strict, log_rest, 0.0)
            log_betas.append(log_beta)
            log_rests.append(log_rest)
            stricts.append(strict)
        insides = []
        for log_rest in log_rests:
            hi = log_rest.astype(BF16)
            lo = (log_rest - hi.astype(F32)).astype(BF16)
            both = jnp.dot(jnp.concatenate([hi, lo], axis=0), later, preferred_element_type=F32)
            insides.append(both[:tq] + both[tq:])
        for kk, log_beta, log_rest, strict, inside in zip(keys, log_betas, log_rests, stricts, insides):
            a = jnp.exp(log_beta + inside + jnp.concatenate([run] * (tk // HEAD_DIM), axis=1))
            if strict is not None:
                a = jnp.where(strict, a, 0.0)
            acc = acc + jnp.dot(a.astype(BF16), vb_ref[kk, :], preferred_element_type=F32)
            run = run + jnp.sum(log_rest, axis=-1, keepdims=True)
        return run, acc

    for qt in range(seq // tq):
        own = slice(qt * tq, (qt + 1) * tq)
        qb = (q_ref[own, :] * SCALE).astype(BF16)
        run = jnp.zeros((tq, HEAD_DIM), F32)
        acc = jnp.zeros((tq, HEAD_DIM), F32)
        run, acc = tiles(qb, (qt + 1) * per_q - 1, [(per_q - 1 - j) * tk for j in range(per_q)], run, acc)
        for i in range(qt):
            run, acc = tiles(qb, (qt - i) * per_q - 1, [None] * per_q, run, acc)
        o_ref[own, :] = acc.astype(o_ref.dtype)


def sb_attention(p, q_col, k_col, v_col, batch, seq, tq=512, tk=256):
    tq = min(tq, seq)
    tk = min(tk, tq)
    head = lambda c0: pl.BlockSpec((seq, HEAD_DIM), lambda b, h: (b, c0 + h))
    return pl.pallas_call(
        functools.partial(_sb_kernel, tq=tq, tk=tk, seq=seq),
        grid=(batch, SB_HEADS),
        in_specs=[head(q_col), head(k_col), head(v_col)],
        out_specs=pl.BlockSpec((seq, HEAD_DIM), lambda b, h: (b, h)),
        out_shape=jax.ShapeDtypeStruct((batch * seq, SB_HEADS * HEAD_DIM), BF16),
        scratch_shapes=[pltpu.VMEM((seq, HEAD_DIM), BF16), pltpu.VMEM((seq, HEAD_DIM), BF16)],
        compiler_params=_params(2),
        name="stick_breaking",
    )(p, p, p)


def _dil_kernel(q0_ref, q1_ref, q2_ref, k0_ref, k1_ref, k2_ref, v0_ref, v1_ref, v2_ref,
                cos_ref, sin_ref, gq_ref, gk_ref, o_ref,
                tmp_ref, qd_ref, kd_ref, vd_ref, og0_ref, og1_ref, og2_ref, lse0_ref, lse1_ref, lse2_ref,
                *, seq):
    q_refs, k_refs, v_refs = (q0_ref, q1_ref, q2_ref), (k0_ref, k1_ref, k2_ref), (v0_ref, v1_ref, v2_ref)
    og_refs, lse_refs = (og0_ref, og1_ref, og2_ref), (lse0_ref, lse1_ref, lse2_ref)
    ta = DIL_SPAN
    row = lax.broadcasted_iota(jnp.int32, (ta, ta), 0)
    col = lax.broadcasted_iota(jnp.int32, (ta, ta), 1)

    for gi, (window, dil) in enumerate(DIL_CONFIGS):
        assert window == dil * DIL_SPAN and seq % (dil * ta) == 0
        n_a = seq // dil
        tiles_per_class = n_a // ta

        def class_major(dst_ref, rows, val, dil=dil, n_a=n_a):
            if dil == 1:
                dst_ref[rows, :] = val.astype(BF16)
                return
            tmp_ref[rows, :] = val
            per = (rows.stop - rows.start) // dil
            a0 = rows.start // dil
            for rho in range(dil):
                src = pl.ds(rows.start + rho, per, stride=dil)
                dst_ref[rho * n_a + a0:rho * n_a + a0 + per, :] = tmp_ref[src, :].astype(BF16)

        for r0 in range(0, seq, PREP_ROWS):
            rows = slice(r0, r0 + PREP_ROWS)
            cos, sin = cos_ref[rows, :], sin_ref[rows, :]
            class_major(qd_ref, rows, _rope(_rms(q_refs[gi][rows, :], gq_ref[...]), cos, sin) * SCALE)
            class_major(kd_ref, rows, _rope(_rms(k_refs[gi][rows, :], gk_ref[...]), cos, sin))
            class_major(vd_ref, rows, v_refs[gi][rows, :])

        for j in range(seq // ta):
            rho, at = divmod(j, tiles_per_class)
            rows = slice(j * ta, (j + 1) * ta)
            qj = qd_ref[rows, :]
            scores = [jnp.where(col <= row, _nt(qj, kd_ref[rows, :]), -jnp.inf)]
            values = [vd_ref[rows, :]]
            if at > 0:
                prev = slice((j - 1) * ta, j * ta)
                scores.append(jnp.where(col >= row, _nt(qj, kd_ref[prev, :]), -jnp.inf))
                values.append(vd_ref[prev, :])
            acc, m, l = _softmax_chunks(scores, values)
            tokens = pl.ds(dil * at * ta + rho, ta, stride=dil) if dil > 1 else rows
            og_refs[gi][tokens, :] = acc / l
            lse_refs[gi][tokens, :] = jnp.broadcast_to(m + jnp.log(l), (ta, HEAD_DIM))

    for r0 in range(0, seq, PREP_ROWS):
        rows = slice(r0, r0 + PREP_ROWS)
        lses = [ref[rows, :] for ref in lse_refs]
        top = functools.reduce(jnp.maximum, lses)
        ws = [jnp.exp(x - top) for x in lses]
        total = functools.reduce(lambda a, b: a + b, ws)
        out = functools.reduce(lambda a, b: a + b, [(w / total) * ref[rows, :] for w, ref in zip(ws, og_refs)])
        o_ref[rows, :] = out.astype(o_ref.dtype)


def dilated_attention(p, q_col, k_col, v_col, gq, gk, cos, sin, batch, seq):
    head = lambda c0: pl.BlockSpec((seq, HEAD_DIM), lambda b, h: (b, c0 + h))
    table = pl.BlockSpec((seq, HEAD_DIM), lambda b, h: (b, 0))
    row_spec = pl.BlockSpec((1, HEAD_DIM), lambda b, h: (0, 0))
    groups = range(DIL_GROUPS)
    return pl.pallas_call(
        functools.partial(_dil_kernel, seq=seq),
        grid=(batch, DIL_HEADS),
        in_specs=([head(q_col + gi * DIL_HEADS) for gi in groups]
                  + [head(k_col + gi * DIL_HEADS) for gi in groups]
                  + [head(v_col + gi * DIL_HEADS) for gi in groups]
                  + [table, table, row_spec, row_spec]),
        out_specs=pl.BlockSpec((seq, HEAD_DIM), lambda b, h: (b, h)),
        out_shape=jax.ShapeDtypeStruct((batch * seq, DIL_HEADS * HEAD_DIM), BF16),
        scratch_shapes=[pltpu.VMEM((seq, HEAD_DIM), F32)]
                       + [pltpu.VMEM((seq, HEAD_DIM), BF16)] * 3
                       + [pltpu.VMEM((seq, HEAD_DIM), F32)] * (2 * DIL_GROUPS),
        compiler_params=_params(2),
        name="dilated",
    )(*([p] * 9), cos, sin, _row1(gq), _row1(gk))


def _xattn_kernel(q_ref, kv_ref, gq_ref, gk_ref, o_ref):
    gq, gk = gq_ref[...], gk_ref[...]
    width = XATTN_HEADS * HEAD_DIM
    for h in range(XATTN_HEADS):
        hs = slice(h * HEAD_DIM, (h + 1) * HEAD_DIM)
        q = (_rms(q_ref[:, hs], gq) * SCALE).astype(BF16)
        k = _rms(kv_ref[:, hs], gk).astype(BF16)
        v = kv_ref[:, width + h * HEAD_DIM:width + (h + 1) * HEAD_DIM].astype(BF16)
        s = _nt(q, k)
        e = jnp.exp(s - jnp.max(s, axis=-1, keepdims=True))
        p = e / jnp.sum(e, axis=-1, keepdims=True)
        o_ref[:, hs] = jnp.dot(p.astype(BF16), v, preferred_element_type=F32).astype(o_ref.dtype)


def memory_cross_attention(q, kv, gq, gk, batch, seq, mem_len, tq=512):
    tq = min(tq, seq)
    nq = seq // tq
    width = XATTN_HEADS * HEAD_DIM
    return pl.pallas_call(
        _xattn_kernel,
        grid=(batch, nq),
        in_specs=[pl.BlockSpec((tq, width), lambda b, i: (b * nq + i, 0)),
                  pl.BlockSpec((mem_len, 2 * width), lambda b, i: (b, 0)),
                  pl.BlockSpec((1, HEAD_DIM), lambda b, i: (0, 0)),
                  pl.BlockSpec((1, HEAD_DIM), lambda b, i: (0, 0))],
        out_specs=pl.BlockSpec((tq, width), lambda b, i: (b * nq + i, 0)),
        out_shape=jax.ShapeDtypeStruct((batch * seq, width), BF16),
        compiler_params=_params(2),
        name="xattn",
    )(q, kv, _row1(gq), _row1(gk))


def _even_mixer(x2, h, cos, sin, batch, seq, e, w_in, w_out, moba_gq, moba_gk, nsa_gq, nsa_gk_cmp,
                nsa_gk_slc, nsa_gk_win, pe_k, pe_v, phi_k, phi_v):
    hd = HEAD_DIM
    main = (3 * MOBA_HEADS + NSA_HEADS + 6 * NSA_GROUPS) * hd
    pad = -w_in.shape[-1] % hd
    p = matmul(h, jnp.pad(w_in[e].astype(BF16), ((0, 0), (0, pad))), tm=BIG_ROW_TILE)

    o_a = moba_attention(p, 0, 16, 32, moba_gq, moba_gk, cos, sin, batch, seq)
    kcmp, vcmp = nsa_compress(p, 64, 68, pe_k, pe_v, phi_k, phi_v, nsa_gk_cmp, cos, sin, batch, seq)
    o_b = nsa_attention(p, 48, 72, 76, 80, 84, main // hd, kcmp, vcmp, nsa_gq, nsa_gk_slc, nsa_gk_win,
                        cos, sin, batch, seq)

    return matmul([o_a, o_b], w_out, layer=e, residual=x2)


def _odd_mixer(x2, h, cos, sin, batch, seq, o, w_in, w_out, dil_gq, dil_gk):
    hd = HEAD_DIM
    p = matmul(h, w_in, layer=o, tm=BIG_ROW_TILE)
    o_c = sb_attention(p, 0, SB_HEADS, 2 * SB_HEADS, batch, seq)
    nd = DIL_GROUPS * DIL_HEADS
    o_d = dilated_attention(p, 3 * SB_HEADS, 3 * SB_HEADS + nd, 3 * SB_HEADS + 2 * nd, dil_gq, dil_gk,
                            cos, sin, batch, seq)
    return matmul([o_c, o_d], w_out, layer=o, residual=x2)


def kernel(x, mem, positions, mix_norm, even_w_in, even_w_out, moba_gq, moba_gk, nsa_gq, nsa_gk_cmp, nsa_gk_slc, nsa_gk_win, nsa_pe_k, nsa_pe_v, nsa_phi_k, nsa_phi_v, odd_w_in, odd_w_out, dil_gq, dil_gk, xattn_norm, mem_norm, xattn_wq, xattn_wkv, xattn_wo, xattn_gq, xattn_gk, ffn_norm, ffn_wg, ffn_wu, ffn_wd):
    batch, seq, d = x.shape
    mem_len = mem.shape[1]
    depth = mix_norm.shape[0]
    x2 = x.reshape(batch * seq, d)
    mem2 = mem.reshape(batch * mem_len, d)
    cos, sin = rope_tables(positions)

    for layer in range(depth):
        h = rmsnorm_rows(x2, mix_norm[layer])
        if layer % 2 == 0:
            e = layer // 2
            x2 = _even_mixer(x2, h, cos, sin, batch, seq, e, even_w_in, even_w_out, moba_gq[e],
                             moba_gk[e], nsa_gq[e], nsa_gk_cmp[e], nsa_gk_slc[e], nsa_gk_win[e],
                             nsa_pe_k[e], nsa_pe_v[e], nsa_phi_k[e], nsa_phi_v[e])
        else:
            o = layer // 2
            x2 = _odd_mixer(x2, h, cos, sin, batch, seq, o, odd_w_in, odd_w_out, dil_gq[o], dil_gk[o])

        h = rmsnorm_rows(x2, xattn_norm[layer])
        mem_n = rmsnorm_rows(mem2, mem_norm[layer])
        q = matmul(h, xattn_wq, layer=layer)
        kv = matmul(mem_n, xattn_wkv, layer=layer)
        o_x = memory_cross_attention(q, kv, xattn_gq[layer], xattn_gk[layer], batch, seq, mem_len)
        x2, h = matmul_residual_norm(o_x, xattn_wo, layer, x2, ffn_norm[layer])
        hidden = swiglu_gate_up(h, ffn_wg, ffn_wu, layer)
        x2 = matmul(hidden, ffn_wd, layer=layer, residual=x2, tm=BIG_ROW_TILE)

    return x2.reshape(batch, seq, d)
```

```python
import functools

import jax
import jax.numpy as jnp
from jax import lax
from jax.experimental import pallas as pl
from jax.experimental.pallas import tpu as pltpu

F32 = jnp.float32
BF16 = jnp.bfloat16
HIGHEST = lax.Precision.HIGHEST

HEAD_DIM = 128
HALF = HEAD_DIM // 2
ROPE_THETA = 10000.0
NORM_EPS = 1e-6
TINY = 1e-30
SCALE = HEAD_DIM ** -0.5

MOBA_HEADS = 16
MOBA_BLOCK = 256
MOBA_TOPK = 3
MOBA_BLOCK_BATCH = 4
NSA_HEADS = 16
NSA_GROUPS = 4
NSA_HG = NSA_HEADS // NSA_GROUPS
NSA_CMP_LEN = 32
NSA_CMP_STRIDE = 16
NSA_SEL_BLOCK = 64
NSA_SEL_SHIFT = NSA_SEL_BLOCK.bit_length() - 1
NSA_SEL_TOPN = 16
NSA_WINDOW = 512
DIL_CONFIGS = ((128, 1), (512, 4), (2048, 16))
DIL_GROUPS = len(DIL_CONFIGS)
DIL_HEADS = 8
DIL_SPAN = 128
DIL_TILE_BATCH = 8
SB_HEADS = 24
XATTN_HEADS = 4

VMEM_LIMIT_BYTES = 56 * 1024 * 1024
NT_DIMS = (((1,), (1,)), ((), ()))
PREP_ROWS = 256
MATMUL_ROW_TILE = 1024
BIG_ROW_TILE = 2048


def _params(n_grid):
    return pltpu.CompilerParams(dimension_semantics=("arbitrary",) * n_grid,
                                vmem_limit_bytes=VMEM_LIMIT_BYTES)


def _nt(a, b, precision=None):
    return lax.dot_general(a, b, NT_DIMS, precision=precision, preferred_element_type=F32)


def _rms(x, g):
    return x * lax.rsqrt(jnp.mean(x * x, axis=-1, keepdims=True) + NORM_EPS) * g


def _rope(x, cos, sin_signed):
    return x * cos + pltpu.roll(x, HALF, 1) * sin_signed


def _softmax_chunks(scores, values):
    return _softmax_chunks_many([scores], [values])[0]


def _softmax_chunks_many(scores, values):
    ms = [jnp.max(functools.reduce(jnp.maximum, s), axis=-1, keepdims=True) for s in scores]
    probs = [[jnp.exp(c - m) for c in s] for s, m in zip(scores, ms)]
    ls = [jnp.sum(functools.reduce(lambda a, b: a + b, p), axis=-1, keepdims=True) for p in probs]
    accs = []
    for p, v in zip(probs, values):
        acc = None
        for pc, vc in zip(p, v):
            part = jnp.dot(pc.astype(BF16), vc, preferred_element_type=F32)
            acc = part if acc is None else acc + part
        accs.append(acc)
    return list(zip(accs, ms, ls))


def _top_rank(vals):
    idx = lax.broadcasted_iota(jnp.int32, vals.shape, 0)
    rank = jnp.zeros(vals.shape, jnp.int32)
    for m in range(vals.shape[0]):
        c = vals[m:m + 1, :]
        ahead = (c > vals) | ((c == vals) & (idx > m))
        rank = rank + ahead.astype(jnp.int32)
    return rank


def _to_columns(x_t):
    rows = x_t.shape[1]
    eye = (lax.broadcasted_iota(jnp.int32, (rows, rows), 0)
           == lax.broadcasted_iota(jnp.int32, (rows, rows), 1)).astype(BF16)
    return _nt(eye, x_t.astype(BF16))


def _rmsnorm_kernel(x_ref, g_ref, o_ref):
    o_ref[...] = _rms(x_ref[...], g_ref[...]).astype(o_ref.dtype)


def rmsnorm_rows(x, g, tm=256):
    m, d = x.shape
    tm = min(tm, m)
    return pl.pallas_call(
        _rmsnorm_kernel,
        grid=(m // tm,),
        in_specs=[pl.BlockSpec((tm, d), lambda i: (i, 0)), pl.BlockSpec((1, d), lambda i: (0, 0))],
        out_specs=pl.BlockSpec((tm, d), lambda i: (i, 0)),
        out_shape=jax.ShapeDtypeStruct((m, d), BF16),
        compiler_params=_params(1),
        name="rmsnorm",
    )(x, g.reshape(1, d))


def _matmul_kernel(*refs, n_lhs, has_res):
    a_refs, w_refs = refs[:n_lhs], refs[n_lhs:2 * n_lhs]
    out = None
    for a_ref, w_ref in zip(a_refs, w_refs):
        part = jnp.dot(a_ref[...], w_ref[...].astype(BF16), preferred_element_type=F32)
        out = part if out is None else out + part
    if has_res:
        out = out + refs[2 * n_lhs][...]
    refs[-1][...] = out.astype(refs[-1].dtype)


def _contraction_tile(kdim, limit=6144):
    if kdim <= limit:
        return kdim
    return max(t for t in range(HEAD_DIM, limit + 1, HEAD_DIM) if kdim % t == 0)


def _matmul_call(pieces, w, layer, n, residual, out_dtype, tm):
    m = pieces[0][0].shape[0]
    tn = 512 if sum(k for _, _, k, _ in pieces) <= 4096 else 256
    tm, tn = min(tm, m), min(tn, n)
    assert m % tm == 0
    lhs_mode = dict(pipeline_mode=pl.Buffered(1)) if tm > MATMUL_ROW_TILE else {}
    a_specs, w_specs = [], []
    for _, col0, k, row0 in pieces:
        assert col0 % k == 0 and row0 % k == 0
        a_specs.append(pl.BlockSpec((tm, k), lambda i, j, c=col0 // k: (i, c), **lhs_mode))
        if layer is None:
            w_specs.append(pl.BlockSpec((k, tn), lambda i, j, r=row0 // k: (r, j)))
        else:
            w_specs.append(pl.BlockSpec((None, k, tn), lambda i, j, r=row0 // k: (layer, r, j)))
    args = [a for a, _, _, _ in pieces] + [w] * len(pieces)
    in_specs = a_specs + w_specs
    if residual is not None:
        in_specs.append(pl.BlockSpec((tm, tn), lambda i, j: (i, j)))
        args.append(residual)
    return pl.pallas_call(
        functools.partial(_matmul_kernel, n_lhs=len(pieces), has_res=residual is not None),
        grid=(m // tm, pl.cdiv(n, tn)),
        in_specs=in_specs,
        out_specs=pl.BlockSpec((tm, tn), lambda i, j: (i, j)),
        out_shape=jax.ShapeDtypeStruct((m, n), out_dtype),
        compiler_params=_params(2),
        name="matmul",
    )(*args)


def matmul(lhs, w, layer=None, residual=None, out_dtype=F32, tm=MATMUL_ROW_TILE):
    arrays = lhs if isinstance(lhs, (list, tuple)) else [lhs]
    n = w.shape[-1]
    if len(arrays) > 1:
        pieces, row0 = [], 0
        for a in arrays:
            pieces.append((a, 0, a.shape[1], row0))
            row0 += a.shape[1]
        return _matmul_call(pieces, w, layer, n, residual, out_dtype, tm)
    a = arrays[0]
    kdim = a.shape[1]
    tk = _contraction_tile(kdim)
    out = residual
    for c in range(kdim // tk):
        last = c == kdim // tk - 1
        out = _matmul_call([(a, c * tk, tk, c * tk)], w, layer, n, out, out_dtype if last else F32, tm)
    return out


def _matmul_norm_kernel(a_ref, w_ref, r_ref, g_ref, x_ref, h_ref):
    x = jnp.dot(a_ref[...], w_ref[...].astype(BF16), preferred_element_type=F32) + r_ref[...]
    x_ref[...] = x
    h_ref[...] = _rms(x, g_ref[...]).astype(h_ref.dtype)


def matmul_residual_norm(a, w, layer, residual, gain, tm=256):
    m, kdim = a.shape
    n = w.shape[-1]
    tm = min(tm, m)
    rows = lambda width: pl.BlockSpec((tm, width), lambda i: (i, 0))
    return pl.pallas_call(
        _matmul_norm_kernel,
        grid=(m // tm,),
        in_specs=[rows(kdim), pl.BlockSpec((None, kdim, n), lambda i: (layer, 0, 0)), rows(n),
                  pl.BlockSpec((1, n), lambda i: (0, 0))],
        out_specs=[rows(n), rows(n)],
        out_shape=[jax.ShapeDtypeStruct((m, n), F32), jax.ShapeDtypeStruct((m, n), BF16)],
        compiler_params=_params(1),
        name="matmul_norm",
    )(a, w, residual, gain.reshape(1, n))


def _norm_matmul_kernel(x_ref, g_ref, w_ref, o_ref):
    h = _rms(x_ref[...], g_ref[...]).astype(BF16)
    o_ref[...] = jnp.dot(h, w_ref[...].astype(BF16), preferred_element_type=F32)


def norm_matmul(x, gain, w, layer, tm=512):
    m, d = x.shape
    n = w.shape[-1]
    tm = min(tm, m)
    return pl.pallas_call(
        _norm_matmul_kernel,
        grid=(m // tm,),
        in_specs=[pl.BlockSpec((tm, d), lambda i: (i, 0)), pl.BlockSpec((1, d), lambda i: (0, 0)),
                  pl.BlockSpec((None, d, n), lambda i: (layer, 0, 0), pipeline_mode=pl.Buffered(1))],
        out_specs=pl.BlockSpec((tm, n), lambda i: (i, 0)),
        out_shape=jax.ShapeDtypeStruct((m, n), F32),
        compiler_params=_params(1),
        name="norm_matmul",
    )(x, gain.reshape(1, d), w)


def _gate_up_kernel(a_ref, wg_ref, wu_ref, o_ref):
    a = a_ref[...]
    g = jnp.dot(a, wg_ref[...].astype(BF16), preferred_element_type=F32)
    u = jnp.dot(a, wu_ref[...].astype(BF16), preferred_element_type=F32)
    o_ref[...] = (g * jax.nn.sigmoid(g) * u).astype(o_ref.dtype)


def swiglu_gate_up(a, wg, wu, layer, tm=BIG_ROW_TILE, tn=256):
    m, kdim = a.shape
    n = wg.shape[-1]
    tm, tn = min(tm, m), min(tn, n)
    w_spec = pl.BlockSpec((None, kdim, tn), lambda i, j: (layer, 0, j))
    lhs_mode = dict(pipeline_mode=pl.Buffered(1)) if tm > MATMUL_ROW_TILE else {}
    return pl.pallas_call(
        _gate_up_kernel,
        grid=(m // tm, pl.cdiv(n, tn)),
        in_specs=[pl.BlockSpec((tm, kdim), lambda i, j: (i, 0), **lhs_mode), w_spec, w_spec],
        out_specs=pl.BlockSpec((tm, tn), lambda i, j: (i, j)),
        out_shape=jax.ShapeDtypeStruct((m, n), BF16),
        compiler_params=_params(2),
        name="swiglu_gate_up",
    )(a, wg, wu)


def _rope_table_kernel(pos_ref, invf_ref, cos_ref, sin_ref):
    ang = pos_ref[...] * invf_ref[...]
    lane = lax.broadcasted_iota(jnp.int32, ang.shape, 1)
    s = jnp.sin(ang)
    cos_ref[...] = jnp.cos(ang)
    sin_ref[...] = jnp.where(lane < HALF, -s, s)


def rope_tables(positions, tr=256):
    n = positions.size
    tr = min(tr, n)
    inv_freq = ROPE_THETA ** (-jnp.arange(HALF, dtype=F32) / HALF)
    invf = jnp.concatenate([inv_freq, inv_freq]).reshape(1, HEAD_DIM)
    pos = jnp.broadcast_to(positions.astype(F32).reshape(n, 1), (n, HEAD_DIM))
    spec = pl.BlockSpec((tr, HEAD_DIM), lambda i: (i, 0))
    return pl.pallas_call(
        _rope_table_kernel,
        grid=(n // tr,),
        in_specs=[spec, pl.BlockSpec((1, HEAD_DIM), lambda i: (0, 0))],
        out_specs=[spec, spec],
        out_shape=[jax.ShapeDtypeStruct((n, HEAD_DIM), F32)] * 2,
        compiler_params=_params(1),
        name="rope_tables",
    )(pos, invf)


def _prep_keys(k_ref, g, cos_ref, sin_ref, out_ref, seq):
    for r0 in range(0, seq, PREP_ROWS):
        rows = slice(r0, min(r0 + PREP_ROWS, seq))
        out_ref[rows, :] = _rope(_rms(k_ref[rows, :], g), cos_ref[rows, :], sin_ref[rows, :]).astype(BF16)


def _row1(x):
    return x.reshape(1, HEAD_DIM)


def _moba_kernel(q_ref, k_ref, v_ref, cos_ref, sin_ref, gq_ref, gk_ref, o_ref, kb_ref, vb_ref, *, nblk):
    blk = MOBA_BLOCK
    means = []
    for n in range(nblk):
        rows = slice(n * blk, (n + 1) * blk)
        k = _rope(_rms(k_ref[rows, :], gk_ref[...]), cos_ref[rows, :], sin_ref[rows, :])
        kb_ref[rows, :] = k.astype(BF16)
        means.append(jnp.mean(k, axis=0, keepdims=True))
    vb_ref[...] = v_ref[...].astype(BF16)
    row = lax.broadcasted_iota(jnp.int32, (blk, blk), 0)
    col = lax.broadcasted_iota(jnp.int32, (blk, blk), 1)

    qbs, sels = [], []
    for i in range(nblk):
        own = slice(i * blk, (i + 1) * blk)
        q = _rope(_rms(q_ref[own, :], gq_ref[...]), cos_ref[own, :], sin_ref[own, :])
        qbs.append((q * SCALE).astype(BF16))
        sel = None
        if i > MOBA_TOPK:
            gate = _nt(jnp.concatenate(means[:i], axis=0), q, precision=HIGHEST)
            sel = _to_columns((_top_rank(gate) < MOBA_TOPK).astype(F32))
        sels.append(sel)

    for i0 in range(0, nblk, MOBA_BLOCK_BATCH):
        batch = range(i0, min(i0 + MOBA_BLOCK_BATCH, nblk))
        scores, values = [], []
        for i in batch:
            own = slice(i * blk, (i + 1) * blk)
            s, v = [jnp.where(col <= row, _nt(qbs[i], kb_ref[own, :]), -jnp.inf)], [vb_ref[own, :]]
            for n in range(i):
                rows = slice(n * blk, (n + 1) * blk)
                sn = _nt(qbs[i], kb_ref[rows, :])
                if sels[i] is not None:
                    sn = jnp.where(sels[i][:, n:n + 1] > 0.5, sn, -jnp.inf)
                s.append(sn)
                v.append(vb_ref[rows, :])
            scores.append(s)
            values.append(v)
        for i, (acc, _, l) in zip(batch, _softmax_chunks_many(scores, values)):
            o_ref[i * blk:(i + 1) * blk, :] = (acc / l).astype(o_ref.dtype)


def moba_attention(p, q_col, k_col, v_col, gq, gk, cos, sin, batch, seq):
    nblk = seq // MOBA_BLOCK
    head = lambda c0: pl.BlockSpec((seq, HEAD_DIM), lambda b, h: (b, c0 + h))
    table = pl.BlockSpec((seq, HEAD_DIM), lambda b, h: (b, 0))
    row_spec = pl.BlockSpec((1, HEAD_DIM), lambda b, h: (0, 0))
    return pl.pallas_call(
        functools.partial(_moba_kernel, nblk=nblk),
        grid=(batch, MOBA_HEADS),
        in_specs=[head(q_col), head(k_col), head(v_col), table, table, row_spec, row_spec],
        out_specs=pl.BlockSpec((seq, HEAD_DIM), lambda b, h: (b, h)),
        out_shape=jax.ShapeDtypeStruct((batch * seq, MOBA_HEADS * HEAD_DIM), BF16),
        scratch_shapes=[pltpu.VMEM((seq, HEAD_DIM), BF16), pltpu.VMEM((seq, HEAD_DIM), BF16)],
        compiler_params=_params(2),
        name="moba",
    )(p, p, p, cos, sin, _row1(gq), _row1(gk))


def _nsa_compress_kernel(kc_ref, vc_ref, pek_ref, pev_ref, phik_ref, phiv_ref, g_ref, cos_ref, sin_ref,
                         ko_ref, vo_ref, *, nslot):
    half = NSA_CMP_LEN // 2

    def compress(x_ref, pe_ref, phi_ref):
        first = jnp.zeros((nslot, HEAD_DIM), F32)
        second = jnp.zeros((nslot, HEAD_DIM), F32)
        for l in range(half):
            xl = x_ref[pl.ds(l, nslot, stride=NSA_CMP_STRIDE), :]
            first += jnp.dot((xl + pe_ref[l:l + 1, :]).astype(BF16), phi_ref[l].astype(BF16),
                             preferred_element_type=F32)
            second += jnp.dot((xl + pe_ref[half + l:half + l + 1, :]).astype(BF16),
                              phi_ref[half + l].astype(BF16), preferred_element_type=F32)
        return pltpu.roll(first, 1, 0) + second

    ends = pl.ds(NSA_CMP_STRIDE - 1, nslot, stride=NSA_CMP_STRIDE)
    kc = compress(kc_ref, pek_ref, phik_ref)
    kc = _rope(_rms(kc, g_ref[...]), cos_ref[ends, :], sin_ref[ends, :])
    ko_ref[0, 0] = kc.astype(BF16)
    vo_ref[0, 0] = compress(vc_ref, pev_ref, phiv_ref).astype(BF16)


def nsa_compress(p, kc_col, vc_col, pe_k, pe_v, phi_k, phi_v, g_kc, cos, sin, batch, seq):
    nslot = seq // NSA_CMP_STRIDE
    full = lambda shape: pl.BlockSpec(shape, lambda b, g: (0,) * len(shape))
    out_spec = pl.BlockSpec((1, 1, nslot, HEAD_DIM), lambda b, g: (b, g, 0, 0))
    out_shape = jax.ShapeDtypeStruct((batch, NSA_GROUPS, nslot, HEAD_DIM), BF16)
    return pl.pallas_call(
        functools.partial(_nsa_compress_kernel, nslot=nslot),
        grid=(batch, NSA_GROUPS),
        in_specs=[pl.BlockSpec((seq, HEAD_DIM), lambda b, g: (b, kc_col + g)),
                  pl.BlockSpec((seq, HEAD_DIM), lambda b, g: (b, vc_col + g)),
                  full((NSA_CMP_LEN, HEAD_DIM)), full((NSA_CMP_LEN, HEAD_DIM)),
                  full((NSA_CMP_LEN, HEAD_DIM, HEAD_DIM)), full((NSA_CMP_LEN, HEAD_DIM, HEAD_DIM)),
                  full((1, HEAD_DIM)),
                  pl.BlockSpec((seq, HEAD_DIM), lambda b, g: (b, 0)),
                  pl.BlockSpec((seq, HEAD_DIM), lambda b, g: (b, 0))],
        out_specs=[out_spec, out_spec],
        out_shape=[out_shape, out_shape],
        compiler_params=_params(2),
        name="nsa_compress",
    )(p, p, pe_k, pe_v, phi_k, phi_v, _row1(g_kc), cos, sin)


NSA_QUERY_TILE = 256


def _nsa_kernel(q_ref, gl_ref, kc_ref, vc_ref, ks_ref, vs_ref, kw_ref, vw_ref, cos_ref, sin_ref,
                gq_ref, gks_ref, gkw_ref, o_ref, ksb_ref, vsb_ref, kwb_ref, vwb_ref,
                s_ref, top_ref, sum_ref, acc_ref, *, tq, seq):
    hg = NSA_HG
    nslot = seq // NSA_CMP_STRIDE
    nsel = seq // NSA_SEL_BLOCK
    topn = min(NSA_SEL_TOPN, nsel)
    qi = pl.program_id(2)
    q0 = qi * tq

    @pl.when(qi == 0)
    def _():
        _prep_keys(ks_ref, gks_ref[...], cos_ref, sin_ref, ksb_ref, seq)
        _prep_keys(kw_ref, gkw_ref[...], cos_ref, sin_ref, kwb_ref, seq)
        vsb_ref[...] = vs_ref[...].astype(BF16)
        vwb_ref[...] = vw_ref[...].astype(BF16)

    def tile_heads(x):
        return jnp.concatenate([x] * hg, axis=0)

    gate_shift = (HEAD_DIM - 3 * hg * pl.program_id(1)) % HEAD_DIM
    row = lax.broadcasted_iota(jnp.int32, (tq, tq), 0)
    col = lax.broadcasted_iota(jnp.int32, (tq, tq), 1)
    per_sel = NSA_SEL_BLOCK // NSA_CMP_STRIDE
    span = NSA_CMP_LEN // NSA_CMP_STRIDE
    nback = NSA_WINDOW // tq

    own = pl.ds(pl.multiple_of(q0, tq), tq)
    cos, sin, gq = cos_ref[own, :], sin_ref[own, :], gq_ref[...]
    qs = []
    for h in range(hg):
        qh = _rope(_rms(q_ref[:, h * HEAD_DIM:(h + 1) * HEAD_DIM], gq), cos, sin)
        qs.append((qh * SCALE).astype(BF16))
    qst = jnp.concatenate(qs, axis=0)

    scores_w, values_w = [], []
    for d in range(nback + 1):
        keys = pl.ds(pl.multiple_of(jnp.maximum(qi - d, 0) * tq, tq), tq)
        s = _nt(qst, kwb_ref[keys, :])
        exists = jnp.where(qi >= d, 0.0, -jnp.inf)
        if d == 0:
            s = s + tile_heads(jnp.where(col <= row, 0.0, -jnp.inf))
        elif d == nback:
            s = s + tile_heads(jnp.where(col > row, exists, -jnp.inf))
        else:
            s = s + exists
        scores_w.append(s)
        values_w.append(vwb_ref[keys, :])

    slot = lax.broadcasted_iota(jnp.int32, (tq, nslot), 1)
    tpos = q0 + lax.broadcasted_iota(jnp.int32, (tq, nslot), 0)
    valid = (slot >= 1) & (slot * NSA_CMP_STRIDE + (NSA_CMP_STRIDE - 1) <= tpos)
    s_c = _nt(qst, kc_ref[0, 0]) + tile_heads(jnp.where(valid, 0.0, -jnp.inf))
    m_c = jnp.max(s_c, axis=-1, keepdims=True)
    m_c = jnp.where(m_c == -jnp.inf, 0.0, m_c)
    e_c = jnp.exp(s_c - m_c)
    p_c = e_c / jnp.maximum(jnp.sum(e_c, axis=-1, keepdims=True), TINY)
    o_c = jnp.dot(p_c.astype(BF16), vc_ref[0, 0], preferred_element_type=F32)

    p_sum = p_c[0:tq]
    for h in range(1, hg):
        p_sum = p_sum + p_c[h * tq:(h + 1) * tq]
    b_idx = lax.broadcasted_iota(jnp.int32, (nsel, nslot), 0)
    j_idx = lax.broadcasted_iota(jnp.int32, (nsel, nslot), 1)
    overlap = ((j_idx >= 1) & (j_idx - 1 > per_sel * b_idx - span) & (j_idx - 1 < per_sel * (b_idx + 1)))
    imp = _nt(overlap.astype(F32), p_sum, precision=HIGHEST)
    blk = lax.broadcasted_iota(jnp.int32, (nsel, tq), 0)
    cur = (q0 + lax.broadcasted_iota(jnp.int32, (nsel, tq), 1)) >> NSA_SEL_SHIFT
    forced = (blk == cur) | (blk == 0)
    imp = jnp.where(forced, jnp.inf, jnp.where(blk <= cur, imp, -jnp.inf))
    acc_w, _, l_w = _softmax_chunks(scores_w, values_w)
    o_w = acc_w / l_w
    sel = _to_columns((_top_rank(imp) < topn).astype(F32)).astype(BF16)

    e_row = lax.broadcasted_iota(jnp.int32, (nsel, tq), 0)
    e_col = lax.broadcasted_iota(jnp.int32, (nsel, tq), 1)
    top_ref[...] = jnp.full(top_ref.shape, -jnp.inf, F32)

    def score_chunk(c, carry):
        keys = pl.ds(pl.multiple_of(c * tq, tq), tq)
        expand = (((c * tq + e_col) >> NSA_SEL_SHIFT) == e_row).astype(BF16)
        keep = (jnp.dot(sel, expand, preferred_element_type=F32) > 0.5) & (c * tq + col <= q0 + row)
        s = _nt(qst, ksb_ref[keys, :]) + tile_heads(jnp.where(keep, 0.0, -jnp.inf))
        s_ref[c] = s
        top_ref[...] = jnp.maximum(top_ref[...], s)
        return carry

    lax.fori_loop(0, qi + 1, score_chunk, 0)
    m_s = jnp.max(top_ref[...], axis=-1, keepdims=True)
    sum_ref[...] = jnp.zeros(sum_ref.shape, F32)
    acc_ref[...] = jnp.zeros(acc_ref.shape, F32)

    def weigh_chunk(c, carry):
        keys = pl.ds(pl.multiple_of(c * tq, tq), tq)
        p = jnp.exp(s_ref[c] - m_s)
        sum_ref[...] += p
        acc_ref[...] += jnp.dot(p.astype(BF16), vsb_ref[keys, :], preferred_element_type=F32)
        return carry

    lax.fori_loop(0, qi + 1, weigh_chunk, 0)
    o_s = acc_ref[...] / jnp.sum(sum_ref[...], axis=-1, keepdims=True)

    gates = jax.nn.sigmoid(pltpu.roll(gl_ref[...], gate_shift, 1))
    for h in range(hg):
        hs = slice(h * tq, (h + 1) * tq)
        out = (gates[:, 3 * h:3 * h + 1] * o_c[hs] + gates[:, 3 * h + 1:3 * h + 2] * o_s[hs]
               + gates[:, 3 * h + 2:3 * h + 3] * o_w[hs])
        o_ref[:, h * HEAD_DIM:(h + 1) * HEAD_DIM] = out.astype(o_ref.dtype)


def nsa_attention(p, q_col, ks_col, vs_col, kw_col, vw_col, gl_col, kcmp, vcmp, gq, gks, gkw, cos, sin,
                  batch, seq):
    tq = min(NSA_QUERY_TILE, seq)
    nq = seq // tq
    gw = NSA_HG * HEAD_DIM
    nslot = seq // NSA_CMP_STRIDE
    head = lambda c0: pl.BlockSpec((seq, HEAD_DIM), lambda b, g, i: (b, c0 + g))
    table = pl.BlockSpec((seq, HEAD_DIM), lambda b, g, i: (b, 0))
    row_spec = pl.BlockSpec((1, HEAD_DIM), lambda b, g, i: (0, 0))
    cmp_spec = pl.BlockSpec((1, 1, nslot, HEAD_DIM), lambda b, g, i: (b, g, 0, 0))
    return pl.pallas_call(
        functools.partial(_nsa_kernel, tq=tq, seq=seq),
        grid=(batch, NSA_GROUPS, nq),
        in_specs=[pl.BlockSpec((tq, gw), lambda b, g, i: (b * nq + i, q_col // NSA_HG + g)),
                  pl.BlockSpec((tq, HEAD_DIM), lambda b, g, i: (b * nq + i, gl_col)),
                  cmp_spec, cmp_spec,
                  head(ks_col), head(vs_col), head(kw_col), head(vw_col), table, table,
                  row_spec, row_spec, row_spec],
        out_specs=pl.BlockSpec((tq, gw), lambda b, g, i: (b * nq + i, g)),
        out_shape=jax.ShapeDtypeStruct((batch * seq, NSA_HEADS * HEAD_DIM), BF16),
        scratch_shapes=[pltpu.VMEM((seq, HEAD_DIM), BF16)] * 4
                       + [pltpu.VMEM((nq, NSA_HG * tq, tq), F32),
                          pltpu.VMEM((NSA_HG * tq, tq), F32), pltpu.VMEM((NSA_HG * tq, tq), F32),
                          pltpu.VMEM((NSA_HG * tq, HEAD_DIM), F32)],
        compiler_params=_params(3),
        name="nsa",
    )(p, p, kcmp, vcmp, p, p, p, p, cos, sin, _row1(gq), _row1(gks), _row1(gkw))


def _sb_kernel(q_ref, k_ref, v_ref, o_ref, kb_ref, vb_ref, *, tq, tk, seq):
    per_q = tq // tk
    kb_ref[...] = k_ref[...].astype(BF16)
    vb_ref[...] = v_ref[...].astype(BF16)

    later = (lax.broadcasted_iota(jnp.int32, (tk, tk), 0)
             > lax.broadcasted_iota(jnp.int32, (tk, tk), 1)).astype(BF16)
    row = lax.broadcasted_iota(jnp.int32, (tq, tk), 0)
    col = lax.broadcasted_iota(jnp.int32, (tq, tk), 1)

    def tiles(qb, first, key0s, run, acc):
        keys = [slice((first - j) * tk, (first - j + 1) * tk) for j in range(len(key0s))]
        zs = [_nt(qb, kb_ref[kk, :]) for kk in keys]
        log_betas, log_rests, stricts = [], [], []
        for z, key0 in zip(zs, key0s):
            log_beta = jnp.minimum(z, 0.0) - jnp.log(1.0 + jnp.exp(-jnp.abs(z)))
            log_rest = log_beta - z
            strict = None
            if key0 is not None:
                strict = key0 + col < row
                log_rest = jnp.where(strict, log_rest, 0.0)
            log_betas.append(log_beta)
            log_rests.append(log_rest)
            stricts.append(strict)
        insides = []
        for log_rest in log_rests:
            hi = log_rest.astype(BF16)
            lo = (log_rest - hi.astype(F32)).astype(BF16)
            both = jnp.dot(jnp.concatenate([hi, lo], axis=0), later, preferred_element_type=F32)
            insides.append(both[:tq] + both[tq:])
        for kk, log_beta, log_rest, strict, inside in zip(keys, log_betas, log_rests, stricts, insides):
            a = jnp.exp(log_beta + inside + jnp.concatenate([run] * (tk // HEAD_DIM), axis=1))
            if strict is not None:
                a = jnp.where(strict, a, 0.0)
            acc = acc + jnp.dot(a.astype(BF16), vb_ref[kk, :], preferred_element_type=F32)
            run = run + jnp.sum(log_rest, axis=-1, keepdims=True)
        return run, acc

    for qt in range(seq // tq):
        own = slice(qt * tq, (qt + 1) * tq)
        qb = (q_ref[own, :] * SCALE).astype(BF16)
        run = jnp.zeros((tq, HEAD_DIM), F32)
        acc = jnp.zeros((tq, HEAD_DIM), F32)
        run, acc = tiles(qb, (qt + 1) * per_q - 1, [(per_q - 1 - j) * tk for j in range(per_q)], run, acc)
        for i in range(qt):
            run, acc = tiles(qb, (qt - i) * per_q - 1, [None] * per_q, run, acc)
        o_ref[own, :] = acc.astype(o_ref.dtype)


def sb_attention(p, q_col, k_col, v_col, batch, seq, tq=512, tk=256):
    tq = min(tq, seq)
    tk = min(tk, tq)
    head = lambda c0: pl.BlockSpec((seq, HEAD_DIM), lambda b, h: (b, c0 + h))
    return pl.pallas_call(
        functools.partial(_sb_kernel, tq=tq, tk=tk, seq=seq),
        grid=(batch, SB_HEADS),
        in_specs=[head(q_col), head(k_col), head(v_col)],
        out_specs=pl.BlockSpec((seq, HEAD_DIM), lambda b, h: (b, h)),
        out_shape=jax.ShapeDtypeStruct((batch * seq, SB_HEADS * HEAD_DIM), BF16),
        scratch_shapes=[pltpu.VMEM((seq, HEAD_DIM), BF16), pltpu.VMEM((seq, HEAD_DIM), BF16)],
        compiler_params=_params(2),
        name="stick_breaking",
    )(p, p, p)


def _dil_kernel(q0_ref, q1_ref, q2_ref, k0_ref, k1_ref, k2_ref, v0_ref, v1_ref, v2_ref,
                cos_ref, sin_ref, gq_ref, gk_ref, o_ref,
                tmp_ref, qd_ref, kd_ref, vd_ref, og0_ref, og1_ref, og2_ref, lse0_ref, lse1_ref, lse2_ref,
                *, seq):
    q_refs, k_refs, v_refs = (q0_ref, q1_ref, q2_ref), (k0_ref, k1_ref, k2_ref), (v0_ref, v1_ref, v2_ref)
    og_refs, lse_refs = (og0_ref, og1_ref, og2_ref), (lse0_ref, lse1_ref, lse2_ref)
    ta = DIL_SPAN
    row = lax.broadcasted_iota(jnp.int32, (ta, ta), 0)
    col = lax.broadcasted_iota(jnp.int32, (ta, ta), 1)

    for gi, (window, dil) in enumerate(DIL_CONFIGS):
        assert window == dil * DIL_SPAN and seq % (dil * ta) == 0
        n_a = seq // dil
        tiles_per_class = n_a // ta

        def class_major(dst_ref, rows, val, dil=dil, n_a=n_a):
            if dil == 1:
                dst_ref[rows, :] = val.astype(BF16)
                return
            tmp_ref[rows, :] = val
            per = (rows.stop - rows.start) // dil
            a0 = rows.start // dil
            for rho in range(dil):
                src = pl.ds(rows.start + rho, per, stride=dil)
                dst_ref[rho * n_a + a0:rho * n_a + a0 + per, :] = tmp_ref[src, :].astype(BF16)

        for r0 in range(0, seq, PREP_ROWS):
            rows = slice(r0, r0 + PREP_ROWS)
            cos, sin = cos_ref[rows, :], sin_ref[rows, :]
            class_major(qd_ref, rows, _rope(_rms(q_refs[gi][rows, :], gq_ref[...]), cos, sin) * SCALE)
            class_major(kd_ref, rows, _rope(_rms(k_refs[gi][rows, :], gk_ref[...]), cos, sin))
            class_major(vd_ref, rows, v_refs[gi][rows, :])

        for j0 in range(0, seq // ta, DIL_TILE_BATCH):
            batch = range(j0, min(j0 + DIL_TILE_BATCH, seq // ta))
            scores, values = [], []
            for j in batch:
                rows = slice(j * ta, (j + 1) * ta)
                qj = qd_ref[rows, :]
                s, v = [jnp.where(col <= row, _nt(qj, kd_ref[rows, :]), -jnp.inf)], [vd_ref[rows, :]]
                if j % tiles_per_class > 0:
                    prev = slice((j - 1) * ta, j * ta)
                    s.append(jnp.where(col >= row, _nt(qj, kd_ref[prev, :]), -jnp.inf))
                    v.append(vd_ref[prev, :])
                scores.append(s)
                values.append(v)
            for j, (acc, m, l) in zip(batch, _softmax_chunks_many(scores, values)):
                rho, at = divmod(j, tiles_per_class)
                tokens = pl.ds(dil * at * ta + rho, ta, stride=dil) if dil > 1 else slice(j * ta, (j + 1) * ta)
                og_refs[gi][tokens, :] = acc / l
                lse_refs[gi][tokens, :] = jnp.broadcast_to(m + jnp.log(l), (ta, HEAD_DIM))

    for r0 in range(0, seq, PREP_ROWS):
        rows = slice(r0, r0 + PREP_ROWS)
        lses = [ref[rows, :] for ref in lse_refs]
        top = functools.reduce(jnp.maximum, lses)
        ws = [jnp.exp(x - top) for x in lses]
        total = functools.reduce(lambda a, b: a + b, ws)
        out = functools.reduce(lambda a, b: a + b, [(w / total) * ref[rows, :] for w, ref in zip(ws, og_refs)])
        o_ref[rows, :] = out.astype(o_ref.dtype)


def dilated_attention(p, q_col, k_col, v_col, gq, gk, cos, sin, batch, seq):
    head = lambda c0: pl.BlockSpec((seq, HEAD_DIM), lambda b, h: (b, c0 + h))
    table = pl.BlockSpec((seq, HEAD_DIM), lambda b, h: (b, 0))
    row_spec = pl.BlockSpec((1, HEAD_DIM), lambda b, h: (0, 0))
    groups = range(DIL_GROUPS)
    return pl.pallas_call(
        functools.partial(_dil_kernel, seq=seq),
        grid=(batch, DIL_HEADS),
        in_specs=([head(q_col + gi * DIL_HEADS) for gi in groups]
                  + [head(k_col + gi * DIL_HEADS) for gi in groups]
                  + [head(v_col + gi * DIL_HEADS) for gi in groups]
                  + [table, table, row_spec, row_spec]),
        out_specs=pl.BlockSpec((seq, HEAD_DIM), lambda b, h: (b, h)),
        out_shape=jax.ShapeDtypeStruct((batch * seq, DIL_HEADS * HEAD_DIM), BF16),
        scratch_shapes=[pltpu.VMEM((seq, HEAD_DIM), F32)]
                       + [pltpu.VMEM((seq, HEAD_DIM), BF16)] * 3
                       + [pltpu.VMEM((seq, HEAD_DIM), F32)] * (2 * DIL_GROUPS),
        compiler_params=_params(2),
        name="dilated",
    )(*([p] * 9), cos, sin, _row1(gq), _row1(gk))


def _xattn_kernel(q_ref, kv_ref, gq_ref, gk_ref, o_ref):
    gq, gk = gq_ref[...], gk_ref[...]
    width = XATTN_HEADS * HEAD_DIM
    for h in range(XATTN_HEADS):
        hs = slice(h * HEAD_DIM, (h + 1) * HEAD_DIM)
        q = (_rms(q_ref[:, hs], gq) * SCALE).astype(BF16)
        k = _rms(kv_ref[:, hs], gk).astype(BF16)
        v = kv_ref[:, width + h * HEAD_DIM:width + (h + 1) * HEAD_DIM].astype(BF16)
        s = _nt(q, k)
        e = jnp.exp(s - jnp.max(s, axis=-1, keepdims=True))
        p = e / jnp.sum(e, axis=-1, keepdims=True)
        o_ref[:, hs] = jnp.dot(p.astype(BF16), v, preferred_element_type=F32).astype(o_ref.dtype)


def memory_cross_attention(q, kv, gq, gk, batch, seq, mem_len, tq=512):
    tq = min(tq, seq)
    nq = seq // tq
    width = XATTN_HEADS * HEAD_DIM
    return pl.pallas_call(
        _xattn_kernel,
        grid=(batch, nq),
        in_specs=[pl.BlockSpec((tq, width), lambda b, i: (b * nq + i, 0)),
                  pl.BlockSpec((mem_len, 2 * width), lambda b, i: (b, 0)),
                  pl.BlockSpec((1, HEAD_DIM), lambda b, i: (0, 0)),
                  pl.BlockSpec((1, HEAD_DIM), lambda b, i: (0, 0))],
        out_specs=pl.BlockSpec((tq, width), lambda b, i: (b * nq + i, 0)),
        out_shape=jax.ShapeDtypeStruct((batch * seq, width), BF16),
        compiler_params=_params(2),
        name="xattn",
    )(q, kv, _row1(gq), _row1(gk))


def _even_mixer(x2, h, cos, sin, batch, seq, e, w_in, w_out, moba_gq, moba_gk, nsa_gq, nsa_gk_cmp,
                nsa_gk_slc, nsa_gk_win, pe_k, pe_v, phi_k, phi_v):
    hd = HEAD_DIM
    main = (3 * MOBA_HEADS + NSA_HEADS + 6 * NSA_GROUPS) * hd
    pad = -w_in.shape[-1] % hd
    p = matmul(h, jnp.pad(w_in[e].astype(BF16), ((0, 0), (0, pad))), tm=BIG_ROW_TILE)

    o_a = moba_attention(p, 0, 16, 32, moba_gq, moba_gk, cos, sin, batch, seq)
    kcmp, vcmp = nsa_compress(p, 64, 68, pe_k, pe_v, phi_k, phi_v, nsa_gk_cmp, cos, sin, batch, seq)
    o_b = nsa_attention(p, 48, 72, 76, 80, 84, main // hd, kcmp, vcmp, nsa_gq, nsa_gk_slc, nsa_gk_win,
                        cos, sin, batch, seq)

    return matmul([o_a, o_b], w_out, layer=e, residual=x2)


def _odd_mixer(x2, h, cos, sin, batch, seq, o, w_in, w_out, dil_gq, dil_gk):
    hd = HEAD_DIM
    p = matmul(h, w_in, layer=o, tm=BIG_ROW_TILE)
    o_c = sb_attention(p, 0, SB_HEADS, 2 * SB_HEADS, batch, seq)
    nd = DIL_GROUPS * DIL_HEADS
    o_d = dilated_attention(p, 3 * SB_HEADS, 3 * SB_HEADS + nd, 3 * SB_HEADS + 2 * nd, dil_gq, dil_gk,
                            cos, sin, batch, seq)
    return matmul([o_c, o_d], w_out, layer=o, residual=x2)


def kernel(x, mem, positions, mix_norm, even_w_in, even_w_out, moba_gq, moba_gk, nsa_gq, nsa_gk_cmp, nsa_gk_slc, nsa_gk_win, nsa_pe_k, nsa_pe_v, nsa_phi_k, nsa_phi_v, odd_w_in, odd_w_out, dil_gq, dil_gk, xattn_norm, mem_norm, xattn_wq, xattn_wkv, xattn_wo, xattn_gq, xattn_gk, ffn_norm, ffn_wg, ffn_wu, ffn_wd):
    batch, seq, d = x.shape
    mem_len = mem.shape[1]
    depth = mix_norm.shape[0]
    x2 = x.reshape(batch * seq, d)
    mem2 = mem.reshape(batch * mem_len, d)
    cos, sin = rope_tables(positions)

    for layer in range(depth):
        h = rmsnorm_rows(x2, mix_norm[layer])
        if layer % 2 == 0:
            e = layer // 2
            x2 = _even_mixer(x2, h, cos, sin, batch, seq, e, even_w_in, even_w_out, moba_gq[e],
                             moba_gk[e], nsa_gq[e], nsa_gk_cmp[e], nsa_gk_slc[e], nsa_gk_win[e],
                             nsa_pe_k[e], nsa_pe_v[e], nsa_phi_k[e], nsa_phi_v[e])
        else:
            o = layer // 2
            x2 = _odd_mixer(x2, h, cos, sin, batch, seq, o, odd_w_in, odd_w_out, dil_gq[o], dil_gk[o])

        q = norm_matmul(x2, xattn_norm[layer], xattn_wq, layer)
        kv = norm_matmul(mem2, mem_norm[layer], xattn_wkv, layer)
        o_x = memory_cross_attention(q, kv, xattn_gq[layer], xattn_gk[layer], batch, seq, mem_len)
        x2, h = matmul_residual_norm(o_x, xattn_wo, layer, x2, ffn_norm[layer])
        hidden = swiglu_gate_up(h, ffn_wg, ffn_wu, layer)
        x2 = matmul(hidden, ffn_wd, layer=layer, residual=x2, tm=BIG_ROW_TILE)

    return x2.reshape(batch, seq, d)
```

```python
import functools

import jax
import jax.numpy as jnp
from jax import lax
from jax.experimental import pallas as pl
from jax.experimental.pallas import tpu as pltpu

F32 = jnp.float32
BF16 = jnp.bfloat16
HIGHEST = lax.Precision.HIGHEST

HEAD_DIM = 128
HALF = HEAD_DIM // 2
ROPE_THETA = 10000.0
NORM_EPS = 1e-6
TINY = 1e-30
SCALE = HEAD_DIM ** -0.5

MOBA_HEADS = 16
MOBA_BLOCK = 256
MOBA_TOPK = 3
MOBA_BLOCK_BATCH = 4
NSA_HEADS = 16
NSA_GROUPS = 4
NSA_HG = NSA_HEADS // NSA_GROUPS
NSA_CMP_LEN = 32
NSA_CMP_STRIDE = 16
NSA_SEL_BLOCK = 64
NSA_SEL_SHIFT = NSA_SEL_BLOCK.bit_length() - 1
NSA_SEL_TOPN = 16
NSA_WINDOW = 512
DIL_CONFIGS = ((128, 1), (512, 4), (2048, 16))
DIL_GROUPS = len(DIL_CONFIGS)
DIL_HEADS = 8
DIL_SPAN = 128
DIL_TILE_BATCH = 8
SB_HEADS = 24
XATTN_HEADS = 4

VMEM_LIMIT_BYTES = 56 * 1024 * 1024
NT_DIMS = (((1,), (1,)), ((), ()))
PREP_ROWS = 256
MATMUL_ROW_TILE = 1024
BIG_ROW_TILE = 2048


def _params(n_grid):
    return pltpu.CompilerParams(dimension_semantics=("arbitrary",) * n_grid,
                                vmem_limit_bytes=VMEM_LIMIT_BYTES)


def _nt(a, b, precision=None):
    return lax.dot_general(a, b, NT_DIMS, precision=precision, preferred_element_type=F32)


def _rms(x, g):
    return x * lax.rsqrt(jnp.mean(x * x, axis=-1, keepdims=True) + NORM_EPS) * g


def _rope(x, cos, sin_signed):
    return x * cos + pltpu.roll(x, HALF, 1) * sin_signed


def _softmax_chunks_many(scores, values):
    ms = [jnp.max(functools.reduce(jnp.maximum, s), axis=-1, keepdims=True) for s in scores]
    probs = [[jnp.exp(c - m) for c in s] for s, m in zip(scores, ms)]
    ls = [jnp.sum(functools.reduce(lambda a, b: a + b, p), axis=-1, keepdims=True) for p in probs]
    accs = []
    for p, v in zip(probs, values):
        acc = None
        for pc, vc in zip(p, v):
            part = jnp.dot(pc.astype(BF16), vc, preferred_element_type=F32)
            acc = part if acc is None else acc + part
        accs.append(acc)
    return list(zip(accs, ms, ls))


def _top_rank_many(vals_list):
    idx = lax.broadcasted_iota(jnp.int32, vals_list[0].shape, 0)
    ranks = [jnp.zeros(v.shape, jnp.int32) for v in vals_list]
    for m in range(vals_list[0].shape[0]):
        for n, vals in enumerate(vals_list):
            c = vals[m:m + 1, :]
            ranks[n] = ranks[n] + ((c > vals) | ((c == vals) & (idx > m))).astype(jnp.int32)
    return ranks


def _top_rank(vals):
    return _top_rank_many([vals])[0]


def _to_columns(x_t):
    rows = x_t.shape[1]
    eye = (lax.broadcasted_iota(jnp.int32, (rows, rows), 0)
           == lax.broadcasted_iota(jnp.int32, (rows, rows), 1)).astype(BF16)
    return _nt(eye, x_t.astype(BF16))


def _rmsnorm_kernel(x_ref, g_ref, o_ref):
    o_ref[...] = _rms(x_ref[...], g_ref[...]).astype(o_ref.dtype)


def rmsnorm_rows(x, g, tm=256):
    m, d = x.shape
    tm = min(tm, m)
    return pl.pallas_call(
        _rmsnorm_kernel,
        grid=(m // tm,),
        in_specs=[pl.BlockSpec((tm, d), lambda i: (i, 0)), pl.BlockSpec((1, d), lambda i: (0, 0))],
        out_specs=pl.BlockSpec((tm, d), lambda i: (i, 0)),
        out_shape=jax.ShapeDtypeStruct((m, d), BF16),
        compiler_params=_params(1),
        name="rmsnorm",
    )(x, g.reshape(1, d))


def _matmul_kernel(*refs, n_lhs, has_res):
    a_refs, w_refs = refs[:n_lhs], refs[n_lhs:2 * n_lhs]
    out = None
    for a_ref, w_ref in zip(a_refs, w_refs):
        part = jnp.dot(a_ref[...], w_ref[...].astype(BF16), preferred_element_type=F32)
        out = part if out is None else out + part
    if has_res:
        out = out + refs[2 * n_lhs][...]
    refs[-1][...] = out.astype(refs[-1].dtype)


def _contraction_tile(kdim, limit=6144):
    if kdim <= limit:
        return kdim
    return max(t for t in range(HEAD_DIM, limit + 1, HEAD_DIM) if kdim % t == 0)


def _matmul_call(pieces, w, layer, n, residual, out_dtype, tm):
    m = pieces[0][0].shape[0]
    tn = 512 if sum(k for _, _, k, _ in pieces) <= 4096 else 256
    tm, tn = min(tm, m), min(tn, n)
    assert m % tm == 0
    lhs_mode = dict(pipeline_mode=pl.Buffered(1)) if tm > MATMUL_ROW_TILE else {}
    a_specs, w_specs = [], []
    for _, col0, k, row0 in pieces:
        assert col0 % k == 0 and row0 % k == 0
        a_specs.append(pl.BlockSpec((tm, k), lambda i, j, c=col0 // k: (i, c), **lhs_mode))
        if layer is None:
            w_specs.append(pl.BlockSpec((k, tn), lambda i, j, r=row0 // k: (r, j)))
        else:
            w_specs.append(pl.BlockSpec((None, k, tn), lambda i, j, r=row0 // k: (layer, r, j)))
    args = [a for a, _, _, _ in pieces] + [w] * len(pieces)
    in_specs = a_specs + w_specs
    if residual is not None:
        in_specs.append(pl.BlockSpec((tm, tn), lambda i, j: (i, j)))
        args.append(residual)
    return pl.pallas_call(
        functools.partial(_matmul_kernel, n_lhs=len(pieces), has_res=residual is not None),
        grid=(m // tm, pl.cdiv(n, tn)),
        in_specs=in_specs,
        out_specs=pl.BlockSpec((tm, tn), lambda i, j: (i, j)),
        out_shape=jax.ShapeDtypeStruct((m, n), out_dtype),
        compiler_params=_params(2),
        name="matmul",
    )(*args)


def matmul(lhs, w, layer=None, residual=None, out_dtype=F32, tm=MATMUL_ROW_TILE):
    arrays = lhs if isinstance(lhs, (list, tuple)) else [lhs]
    n = w.shape[-1]
    if len(arrays) > 1:
        pieces, row0 = [], 0
        for a in arrays:
            pieces.append((a, 0, a.shape[1], row0))
            row0 += a.shape[1]
        return _matmul_call(pieces, w, layer, n, residual, out_dtype, tm)
    a = arrays[0]
    kdim = a.shape[1]
    tk = _contraction_tile(kdim)
    out = residual
    for c in range(kdim // tk):
        last = c == kdim // tk - 1
        out = _matmul_call([(a, c * tk, tk, c * tk)], w, layer, n, out, out_dtype if last else F32, tm)
    return out


def _matmul_norm_kernel(a_ref, w_ref, r_ref, g_ref, x_ref, h_ref):
    x = jnp.dot(a_ref[...], w_ref[...].astype(BF16), preferred_element_type=F32) + r_ref[...]
    x_ref[...] = x
    h_ref[...] = _rms(x, g_ref[...]).astype(h_ref.dtype)


def matmul_residual_norm(a, w, layer, residual, gain, tm=256):
    m, kdim = a.shape
    n = w.shape[-1]
    tm = min(tm, m)
    rows = lambda width: pl.BlockSpec((tm, width), lambda i: (i, 0))
    return pl.pallas_call(
        _matmul_norm_kernel,
        grid=(m // tm,),
        in_specs=[rows(kdim), pl.BlockSpec((None, kdim, n), lambda i: (layer, 0, 0)), rows(n),
                  pl.BlockSpec((1, n), lambda i: (0, 0))],
        out_specs=[rows(n), rows(n)],
        out_shape=[jax.ShapeDtypeStruct((m, n), F32), jax.ShapeDtypeStruct((m, n), BF16)],
        compiler_params=_params(1),
        name="matmul_norm",
    )(a, w, residual, gain.reshape(1, n))


def _norm_matmul_kernel(x_ref, g_ref, w_ref, o_ref):
    h = _rms(x_ref[...], g_ref[...]).astype(BF16)
    o_ref[...] = jnp.dot(h, w_ref[...].astype(BF16), preferred_element_type=F32)


def norm_matmul(x, gain, w, layer, tm=512):
    m, d = x.shape
    n = w.shape[-1]
    tm = min(tm, m)
    return pl.pallas_call(
        _norm_matmul_kernel,
        grid=(m // tm,),
        in_specs=[pl.BlockSpec((tm, d), lambda i: (i, 0)), pl.BlockSpec((1, d), lambda i: (0, 0)),
                  pl.BlockSpec((None, d, n), lambda i: (layer, 0, 0), pipeline_mode=pl.Buffered(1))],
        out_specs=pl.BlockSpec((tm, n), lambda i: (i, 0)),
        out_shape=jax.ShapeDtypeStruct((m, n), F32),
        compiler_params=_params(1),
        name="norm_matmul",
    )(x, gain.reshape(1, d), w)


def _gate_up_kernel(a_ref, wg_ref, wu_ref, o_ref):
    a = a_ref[...]
    g = jnp.dot(a, wg_ref[...].astype(BF16), preferred_element_type=F32)
    u = jnp.dot(a, wu_ref[...].astype(BF16), preferred_element_type=F32)
    o_ref[...] = (g * jax.nn.sigmoid(g) * u).astype(o_ref.dtype)


def swiglu_gate_up(a, wg, wu, layer, tm=BIG_ROW_TILE, tn=256):
    m, kdim = a.shape
    n = wg.shape[-1]
    tm, tn = min(tm, m), min(tn, n)
    w_spec = pl.BlockSpec((None, kdim, tn), lambda i, j: (layer, 0, j))
    lhs_mode = dict(pipeline_mode=pl.Buffered(1)) if tm > MATMUL_ROW_TILE else {}
    return pl.pallas_call(
        _gate_up_kernel,
        grid=(m // tm, pl.cdiv(n, tn)),
        in_specs=[pl.BlockSpec((tm, kdim), lambda i, j: (i, 0), **lhs_mode), w_spec, w_spec],
        out_specs=pl.BlockSpec((tm, tn), lambda i, j: (i, j)),
        out_shape=jax.ShapeDtypeStruct((m, n), BF16),
        compiler_params=_params(2),
        name="swiglu_gate_up",
    )(a, wg, wu)


def _rope_table_kernel(pos_ref, invf_ref, cos_ref, sin_ref):
    ang = pos_ref[...] * invf_ref[...]
    lane = lax.broadcasted_iota(jnp.int32, ang.shape, 1)
    s = jnp.sin(ang)
    cos_ref[...] = jnp.cos(ang)
    sin_ref[...] = jnp.where(lane < HALF, -s, s)


def rope_tables(positions, tr=256):
    n = positions.size
    tr = min(tr, n)
    inv_freq = ROPE_THETA ** (-jnp.arange(HALF, dtype=F32) / HALF)
    invf = jnp.concatenate([inv_freq, inv_freq]).reshape(1, HEAD_DIM)
    pos = jnp.broadcast_to(positions.astype(F32).reshape(n, 1), (n, HEAD_DIM))
    spec = pl.BlockSpec((tr, HEAD_DIM), lambda i: (i, 0))
    return pl.pallas_call(
        _rope_table_kernel,
        grid=(n // tr,),
        in_specs=[spec, pl.BlockSpec((1, HEAD_DIM), lambda i: (0, 0))],
        out_specs=[spec, spec],
        out_shape=[jax.ShapeDtypeStruct((n, HEAD_DIM), F32)] * 2,
        compiler_params=_params(1),
        name="rope_tables",
    )(pos, invf)


def _prep_keys(k_ref, g, cos_ref, sin_ref, out_ref, seq):
    for r0 in range(0, seq, PREP_ROWS):
        rows = slice(r0, min(r0 + PREP_ROWS, seq))
        out_ref[rows, :] = _rope(_rms(k_ref[rows, :], g), cos_ref[rows, :], sin_ref[rows, :]).astype(BF16)


def _row1(x):
    return x.reshape(1, HEAD_DIM)


def _moba_kernel(q_ref, k_ref, v_ref, cos_ref, sin_ref, gq_ref, gk_ref, o_ref, kb_ref, vb_ref, *, nblk):
    blk = MOBA_BLOCK
    means = []
    for n in range(nblk):
        rows = slice(n * blk, (n + 1) * blk)
        k = _rope(_rms(k_ref[rows, :], gk_ref[...]), cos_ref[rows, :], sin_ref[rows, :])
        kb_ref[rows, :] = k.astype(BF16)
        means.append(jnp.mean(k, axis=0, keepdims=True))
    vb_ref[...] = v_ref[...].astype(BF16)
    row = lax.broadcasted_iota(jnp.int32, (blk, blk), 0)
    col = lax.broadcasted_iota(jnp.int32, (blk, blk), 1)

    qbs, sels = [], []
    for i in range(nblk):
        own = slice(i * blk, (i + 1) * blk)
        q = _rope(_rms(q_ref[own, :], gq_ref[...]), cos_ref[own, :], sin_ref[own, :])
        qbs.append((q * SCALE).astype(BF16))
        sel = None
        if i > MOBA_TOPK:
            gate = _nt(jnp.concatenate(means[:i], axis=0), q, precision=HIGHEST)
            sel = _to_columns((_top_rank(gate) < MOBA_TOPK).astype(F32))
        sels.append(sel)

    for i0 in range(0, nblk, MOBA_BLOCK_BATCH):
        batch = range(i0, min(i0 + MOBA_BLOCK_BATCH, nblk))
        scores, values = [], []
        for i in batch:
            own = slice(i * blk, (i + 1) * blk)
            s, v = [jnp.where(col <= row, _nt(qbs[i], kb_ref[own, :]), -jnp.inf)], [vb_ref[own, :]]
            for n in range(i):
                rows = slice(n * blk, (n + 1) * blk)
                sn = _nt(qbs[i], kb_ref[rows, :])
                if sels[i] is not None:
                    sn = jnp.where(sels[i][:, n:n + 1] > 0.5, sn, -jnp.inf)
                s.append(sn)
                v.append(vb_ref[rows, :])
            scores.append(s)
            values.append(v)
        for i, (acc, _, l) in zip(batch, _softmax_chunks_many(scores, values)):
            o_ref[i * blk:(i + 1) * blk, :] = (acc / l).astype(o_ref.dtype)


def moba_attention(p, q_col, k_col, v_col, gq, gk, cos, sin, batch, seq):
    nblk = seq // MOBA_BLOCK
    head = lambda c0: pl.BlockSpec((seq, HEAD_DIM), lambda b, h: (b, c0 + h))
    table = pl.BlockSpec((seq, HEAD_DIM), lambda b, h: (b, 0))
    row_spec = pl.BlockSpec((1, HEAD_DIM), lambda b, h: (0, 0))
    return pl.pallas_call(
        functools.partial(_moba_kernel, nblk=nblk),
        grid=(batch, MOBA_HEADS),
        in_specs=[head(q_col), head(k_col), head(v_col), table, table, row_spec, row_spec],
        out_specs=pl.BlockSpec((seq, HEAD_DIM), lambda b, h: (b, h)),
        out_shape=jax.ShapeDtypeStruct((batch * seq, MOBA_HEADS * HEAD_DIM), BF16),
        scratch_shapes=[pltpu.VMEM((seq, HEAD_DIM), BF16), pltpu.VMEM((seq, HEAD_DIM), BF16)],
        compiler_params=_params(2),
        name="moba",
    )(p, p, p, cos, sin, _row1(gq), _row1(gk))


def _nsa_compress_kernel(kc_ref, vc_ref, pek_ref, pev_ref, phik_ref, phiv_ref, g_ref, cos_ref, sin_ref,
                         ko_ref, vo_ref, *, nslot):
    half = NSA_CMP_LEN // 2

    def compress(x_ref, pe_ref, phi_ref):
        first = jnp.zeros((nslot, HEAD_DIM), F32)
        second = jnp.zeros((nslot, HEAD_DIM), F32)
        for l in range(half):
            xl = x_ref[pl.ds(l, nslot, stride=NSA_CMP_STRIDE), :]
            first += jnp.dot((xl + pe_ref[l:l + 1, :]).astype(BF16), phi_ref[l].astype(BF16),
                             preferred_element_type=F32)
            second += jnp.dot((xl + pe_ref[half + l:half + l + 1, :]).astype(BF16),
                              phi_ref[half + l].astype(BF16), preferred_element_type=F32)
        return pltpu.roll(first, 1, 0) + second

    ends = pl.ds(NSA_CMP_STRIDE - 1, nslot, stride=NSA_CMP_STRIDE)
    kc = compress(kc_ref, pek_ref, phik_ref)
    kc = _rope(_rms(kc, g_ref[...]), cos_ref[ends, :], sin_ref[ends, :])
    ko_ref[0, 0] = kc.astype(BF16)
    vo_ref[0, 0] = compress(vc_ref, pev_ref, phiv_ref).astype(BF16)


def nsa_compress(p, kc_col, vc_col, pe_k, pe_v, phi_k, phi_v, g_kc, cos, sin, batch, seq):
    nslot = seq // NSA_CMP_STRIDE
    full = lambda shape: pl.BlockSpec(shape, lambda b, g: (0,) * len(shape))
    out_spec = pl.BlockSpec((1, 1, nslot, HEAD_DIM), lambda b, g: (b, g, 0, 0))
    out_shape = jax.ShapeDtypeStruct((batch, NSA_GROUPS, nslot, HEAD_DIM), BF16)
    return pl.pallas_call(
        functools.partial(_nsa_compress_kernel, nslot=nslot),
        grid=(batch, NSA_GROUPS),
        in_specs=[pl.BlockSpec((seq, HEAD_DIM), lambda b, g: (b, kc_col + g)),
                  pl.BlockSpec((seq, HEAD_DIM), lambda b, g: (b, vc_col + g)),
                  full((NSA_CMP_LEN, HEAD_DIM)), full((NSA_CMP_LEN, HEAD_DIM)),
                  full((NSA_CMP_LEN, HEAD_DIM, HEAD_DIM)), full((NSA_CMP_LEN, HEAD_DIM, HEAD_DIM)),
                  full((1, HEAD_DIM)),
                  pl.BlockSpec((seq, HEAD_DIM), lambda b, g: (b, 0)),
                  pl.BlockSpec((seq, HEAD_DIM), lambda b, g: (b, 0))],
        out_specs=[out_spec, out_spec],
        out_shape=[out_shape, out_shape],
        compiler_params=_params(2),
        name="nsa_compress",
    )(p, p, pe_k, pe_v, phi_k, phi_v, _row1(g_kc), cos, sin)


NSA_QUERY_TILE = 256


NSA_GROUP_BATCH = 2


def _nsa_batch_kernel(q_ref, gl_ref, kc_ref, vc_ref, ks_ref, vs_ref, kw_ref, vw_ref, cos_ref, sin_ref,
                      gq_ref, gks_ref, gkw_ref, o_ref, ksb_ref, vsb_ref, kwb_ref, vwb_ref,
                      s_ref, top_ref, sum_ref, acc_ref, *, tq, seq):
    hg = NSA_HG
    groups = range(NSA_GROUP_BATCH)
    nslot = seq // NSA_CMP_STRIDE
    nsel = seq // NSA_SEL_BLOCK
    topn = min(NSA_SEL_TOPN, nsel)
    qi = pl.program_id(2)
    q0 = qi * tq
    lanes = [slice(g * HEAD_DIM, (g + 1) * HEAD_DIM) for g in groups]

    @pl.when(qi == 0)
    def _():
        for g in groups:
            _prep_keys(ks_ref.at[:, lanes[g]], gks_ref[...], cos_ref, sin_ref, ksb_ref.at[:, lanes[g]], seq)
            _prep_keys(kw_ref.at[:, lanes[g]], gkw_ref[...], cos_ref, sin_ref, kwb_ref.at[:, lanes[g]], seq)
        vsb_ref[...] = vs_ref[...].astype(BF16)
        vwb_ref[...] = vw_ref[...].astype(BF16)

    def tile_heads(x):
        return jnp.concatenate([x] * hg, axis=0)

    row = lax.broadcasted_iota(jnp.int32, (tq, tq), 0)
    col = lax.broadcasted_iota(jnp.int32, (tq, tq), 1)
    per_sel = NSA_SEL_BLOCK // NSA_CMP_STRIDE
    span = NSA_CMP_LEN // NSA_CMP_STRIDE
    nback = NSA_WINDOW // tq

    own = pl.ds(pl.multiple_of(q0, tq), tq)
    cos, sin, gq = cos_ref[own, :], sin_ref[own, :], gq_ref[...]
    qst = []
    for g in groups:
        qs = []
        for h in range(hg):
            c0 = (g * hg + h) * HEAD_DIM
            qs.append((_rope(_rms(q_ref[:, c0:c0 + HEAD_DIM], gq), cos, sin) * SCALE).astype(BF16))
        qst.append(jnp.concatenate(qs, axis=0))

    scores_w, values_w = [[] for _ in groups], [[] for _ in groups]
    for d in range(nback + 1):
        keys = pl.ds(pl.multiple_of(jnp.maximum(qi - d, 0) * tq, tq), tq)
        exists = jnp.where(qi >= d, 0.0, -jnp.inf)
        if d == 0:
            bias = tile_heads(jnp.where(col <= row, 0.0, -jnp.inf))
        elif d == nback:
            bias = tile_heads(jnp.where(col > row, exists, -jnp.inf))
        else:
            bias = exists
        for g in groups:
            scores_w[g].append(_nt(qst[g], kwb_ref[keys, lanes[g]]) + bias)
            values_w[g].append(vwb_ref[keys, lanes[g]])

    slot = lax.broadcasted_iota(jnp.int32, (tq, nslot), 1)
    tpos = q0 + lax.broadcasted_iota(jnp.int32, (tq, nslot), 0)
    valid = (slot >= 1) & (slot * NSA_CMP_STRIDE + (NSA_CMP_STRIDE - 1) <= tpos)
    bias_c = tile_heads(jnp.where(valid, 0.0, -jnp.inf))
    s_c = [_nt(qst[g], kc_ref[0, g]) + bias_c for g in groups]
    m_c = [jnp.max(s, axis=-1, keepdims=True) for s in s_c]
    m_c = [jnp.where(m == -jnp.inf, 0.0, m) for m in m_c]
    e_c = [jnp.exp(s - m) for s, m in zip(s_c, m_c)]
    p_c = [e / jnp.maximum(jnp.sum(e, axis=-1, keepdims=True), TINY) for e in e_c]
    o_c = [jnp.dot(p_c[g].astype(BF16), vc_ref[0, g], preferred_element_type=F32) for g in groups]

    b_idx = lax.broadcasted_iota(jnp.int32, (nsel, nslot), 0)
    j_idx = lax.broadcasted_iota(jnp.int32, (nsel, nslot), 1)
    overlap = ((j_idx >= 1) & (j_idx - 1 > per_sel * b_idx - span)
               & (j_idx - 1 < per_sel * (b_idx + 1))).astype(F32)
    blk = lax.broadcasted_iota(jnp.int32, (nsel, tq), 0)
    cur = (q0 + lax.broadcasted_iota(jnp.int32, (nsel, tq), 1)) >> NSA_SEL_SHIFT
    forced = (blk == cur) | (blk == 0)
    imps = []
    for g in groups:
        p_sum = p_c[g][0:tq]
        for h in range(1, hg):
            p_sum = p_sum + p_c[g][h * tq:(h + 1) * tq]
        imp = _nt(overlap, p_sum, precision=HIGHEST)
        imps.append(jnp.where(forced, jnp.inf, jnp.where(blk <= cur, imp, -jnp.inf)))
    o_w = [acc / l for acc, _, l in _softmax_chunks_many(scores_w, values_w)]
    sels = [_to_columns((rank < topn).astype(F32)).astype(BF16) for rank in _top_rank_many(imps)]

    e_row = lax.broadcasted_iota(jnp.int32, (nsel, tq), 0)
    e_col = lax.broadcasted_iota(jnp.int32, (nsel, tq), 1)
    top_ref[...] = jnp.full(top_ref.shape, -jnp.inf, F32)

    def score_chunk(c, carry):
        keys = pl.ds(pl.multiple_of(c * tq, tq), tq)
        expand = (((c * tq + e_col) >> NSA_SEL_SHIFT) == e_row).astype(BF16)
        causal = c * tq + col <= q0 + row
        scores = [_nt(qst[g], ksb_ref[keys, lanes[g]]) for g in groups]
        for g in groups:
            keep = (jnp.dot(sels[g], expand, preferred_element_type=F32) > 0.5) & causal
            s = scores[g] + tile_heads(jnp.where(keep, 0.0, -jnp.inf))
            s_ref[g, c] = s
            top_ref[g] = jnp.maximum(top_ref[g], s)
        return carry

    lax.fori_loop(0, qi + 1, score_chunk, 0)
    m_s = [jnp.max(top_ref[g], axis=-1, keepdims=True) for g in groups]
    sum_ref[...] = jnp.zeros(sum_ref.shape, F32)
    acc_ref[...] = jnp.zeros(acc_ref.shape, F32)

    def weigh_chunk(c, carry):
        keys = pl.ds(pl.multiple_of(c * tq, tq), tq)
        probs = [jnp.exp(s_ref[g, c] - m_s[g]) for g in groups]
        for g in groups:
            sum_ref[g] += probs[g]
            acc_ref[g] += jnp.dot(probs[g].astype(BF16), vsb_ref[keys, lanes[g]], preferred_element_type=F32)
        return carry

    lax.fori_loop(0, qi + 1, weigh_chunk, 0)

    gl = gl_ref[...]
    for g in groups:
        o_s = acc_ref[g] / jnp.sum(sum_ref[g], axis=-1, keepdims=True)
        group = pl.program_id(1) * NSA_GROUP_BATCH + g
        gates = jax.nn.sigmoid(pltpu.roll(gl, (HEAD_DIM - 3 * hg * group) % HEAD_DIM, 1))
        for h in range(hg):
            hs = slice(h * tq, (h + 1) * tq)
            out = (gates[:, 3 * h:3 * h + 1] * o_c[g][hs] + gates[:, 3 * h + 1:3 * h + 2] * o_s[hs]
                   + gates[:, 3 * h + 2:3 * h + 3] * o_w[g][hs])
            c0 = (g * hg + h) * HEAD_DIM
            o_ref[:, c0:c0 + HEAD_DIM] = out.astype(o_ref.dtype)


def nsa_attention_batched(p, q_col, ks_col, vs_col, kw_col, vw_col, gl_col, kcmp, vcmp, gq, gks, gkw,
                          cos, sin, batch, seq):
    tq = min(NSA_QUERY_TILE, seq)
    nq = seq // tq
    gb = NSA_GROUP_BATCH
    gw = gb * NSA_HG * HEAD_DIM
    rows = NSA_HG * tq
    nslot = seq // NSA_CMP_STRIDE
    assert NSA_GROUPS % gb == 0 and all(c % gb == 0 for c in (ks_col, vs_col, kw_col, vw_col))
    heads = lambda c0: pl.BlockSpec((seq, gb * HEAD_DIM), lambda b, g, i: (b, c0 // gb + g),
                                    pipeline_mode=pl.Buffered(1))
    table = pl.BlockSpec((seq, HEAD_DIM), lambda b, g, i: (b, 0))
    row_spec = pl.BlockSpec((1, HEAD_DIM), lambda b, g, i: (0, 0))
    cmp_spec = pl.BlockSpec((1, gb, nslot, HEAD_DIM), lambda b, g, i: (b, g, 0, 0))
    return pl.pallas_call(
        functools.partial(_nsa_batch_kernel, tq=tq, seq=seq),
        grid=(batch, NSA_GROUPS // gb, nq),
        in_specs=[pl.BlockSpec((tq, gw), lambda b, g, i: (b * nq + i, q_col // (gb * NSA_HG) + g)),
                  pl.BlockSpec((tq, HEAD_DIM), lambda b, g, i: (b * nq + i, gl_col)),
                  cmp_spec, cmp_spec,
                  heads(ks_col), heads(vs_col), heads(kw_col), heads(vw_col), table, table,
                  row_spec, row_spec, row_spec],
        out_specs=pl.BlockSpec((tq, gw), lambda b, g, i: (b * nq + i, g)),
        out_shape=jax.ShapeDtypeStruct((batch * seq, NSA_HEADS * HEAD_DIM), BF16),
        scratch_shapes=[pltpu.VMEM((seq, gb * HEAD_DIM), BF16)] * 4
                       + [pltpu.VMEM((gb, nq, rows, tq), F32),
                          pltpu.VMEM((gb, rows, tq), F32), pltpu.VMEM((gb, rows, tq), F32),
                          pltpu.VMEM((gb, rows, HEAD_DIM), F32)],
        compiler_params=_params(3),
        name="nsa",
    )(p, p, kcmp, vcmp, p, p, p, p, cos, sin, _row1(gq), _row1(gks), _row1(gkw))


def _sb_kernel(q_ref, k_ref, v_ref, o_ref, kb_ref, vb_ref, *, tq, tk, seq):
    per_q = tq // tk
    kb_ref[...] = k_ref[...].astype(BF16)
    vb_ref[...] = v_ref[...].astype(BF16)

    later = (lax.broadcasted_iota(jnp.int32, (tk, tk), 0)
             > lax.broadcasted_iota(jnp.int32, (tk, tk), 1)).astype(BF16)
    row = lax.broadcasted_iota(jnp.int32, (tq, tk), 0)
    col = lax.broadcasted_iota(jnp.int32, (tq, tk), 1)

    def tiles(qb, first, key0s, run, acc):
        keys = [slice((first - j) * tk, (first - j + 1) * tk) for j in range(len(key0s))]
        zs = [_nt(qb, kb_ref[kk, :]) for kk in keys]
        log_betas, log_rests, stricts = [], [], []
        for z, key0 in zip(zs, key0s):
            log_beta = jnp.minimum(z, 0.0) - jnp.log(1.0 + jnp.exp(-jnp.abs(z)))
            log_rest = log_beta - z
            strict = None
            if key0 is not None:
                strict = key0 + col < row
                log_rest = jnp.where(strict, log_rest, 0.0)
            log_betas.append(log_beta)
            log_rests.append(log_rest)
            stricts.append(strict)
        insides = []
        for log_rest in log_rests:
            hi = log_rest.astype(BF16)
            lo = (log_rest - hi.astype(F32)).astype(BF16)
            both = jnp.dot(jnp.concatenate([hi, lo], axis=0), later, preferred_element_type=F32)
            insides.append(both[:tq] + both[tq:])
        for kk, log_beta, log_rest, strict, inside in zip(keys, log_betas, log_rests, stricts, insides):
            a = jnp.exp(log_beta + inside + jnp.concatenate([run] * (tk // HEAD_DIM), axis=1))
            if strict is not None:
                a = jnp.where(strict, a, 0.0)
            acc = acc + jnp.dot(a.astype(BF16), vb_ref[kk, :], preferred_element_type=F32)
            run = run + jnp.sum(log_rest, axis=-1, keepdims=True)
        return run, acc

    for qt in range(seq // tq):
        own = slice(qt * tq, (qt + 1) * tq)
        qb = (q_ref[own, :] * SCALE).astype(BF16)
        run = jnp.zeros((tq, HEAD_DIM), F32)
        acc = jnp.zeros((tq, HEAD_DIM), F32)
        run, acc = tiles(qb, (qt + 1) * per_q - 1, [(per_q - 1 - j) * tk for j in range(per_q)], run, acc)
        for i in range(qt):
            run, acc = tiles(qb, (qt - i) * per_q - 1, [None] * per_q, run, acc)
        o_ref[own, :] = acc.astype(o_ref.dtype)


def sb_attention(p, q_col, k_col, v_col, batch, seq, tq=512, tk=256):
    tq = min(tq, seq)
    tk = min(tk, tq)
    head = lambda c0: pl.BlockSpec((seq, HEAD_DIM), lambda b, h: (b, c0 + h))
    return pl.pallas_call(
        functools.partial(_sb_kernel, tq=tq, tk=tk, seq=seq),
        grid=(batch, SB_HEADS),
        in_specs=[head(q_col), head(k_col), head(v_col)],
        out_specs=pl.BlockSpec((seq, HEAD_DIM), lambda b, h: (b, h)),
        out_shape=jax.ShapeDtypeStruct((batch * seq, SB_HEADS * HEAD_DIM), BF16),
        scratch_shapes=[pltpu.VMEM((seq, HEAD_DIM), BF16), pltpu.VMEM((seq, HEAD_DIM), BF16)],
        compiler_params=_params(2),
        name="stick_breaking",
    )(p, p, p)


def _dil_kernel(q0_ref, q1_ref, q2_ref, k0_ref, k1_ref, k2_ref, v0_ref, v1_ref, v2_ref,
                cos_ref, sin_ref, gq_ref, gk_ref, o_ref,
                tmp_ref, qd_ref, kd_ref, vd_ref, og0_ref, og1_ref, og2_ref, lse0_ref, lse1_ref, lse2_ref,
                *, seq):
    q_refs, k_refs, v_refs = (q0_ref, q1_ref, q2_ref), (k0_ref, k1_ref, k2_ref), (v0_ref, v1_ref, v2_ref)
    og_refs, lse_refs = (og0_ref, og1_ref, og2_ref), (lse0_ref, lse1_ref, lse2_ref)
    ta = DIL_SPAN
    row = lax.broadcasted_iota(jnp.int32, (ta, ta), 0)
    col = lax.broadcasted_iota(jnp.int32, (ta, ta), 1)

    for gi, (window, dil) in enumerate(DIL_CONFIGS):
        assert window == dil * DIL_SPAN and seq % (dil * ta) == 0
        n_a = seq // dil
        tiles_per_class = n_a // ta

        def class_major(dst_ref, rows, val, dil=dil, n_a=n_a):
            if dil == 1:
                dst_ref[rows, :] = val.astype(BF16)
                return
            tmp_ref[rows, :] = val
            per = (rows.stop - rows.start) // dil
            a0 = rows.start // dil
            for rho in range(dil):
                src = pl.ds(rows.start + rho, per, stride=dil)
                dst_ref[rho * n_a + a0:rho * n_a + a0 + per, :] = tmp_ref[src, :].astype(BF16)

        for r0 in range(0, seq, PREP_ROWS):
            rows = slice(r0, r0 + PREP_ROWS)
            cos, sin = cos_ref[rows, :], sin_ref[rows, :]
            class_major(qd_ref, rows, _rope(_rms(q_refs[gi][rows, :], gq_ref[...]), cos, sin) * SCALE)
            class_major(kd_ref, rows, _rope(_rms(k_refs[gi][rows, :], gk_ref[...]), cos, sin))
            class_major(vd_ref, rows, v_refs[gi][rows, :])

        for j0 in range(0, seq // ta, DIL_TILE_BATCH):
            batch = range(j0, min(j0 + DIL_TILE_BATCH, seq // ta))
            scores, values = [], []
            for j in batch:
                rows = slice(j * ta, (j + 1) * ta)
                qj = qd_ref[rows, :]
                s, v = [jnp.where(col <= row, _nt(qj, kd_ref[rows, :]), -jnp.inf)], [vd_ref[rows, :]]
                if j % tiles_per_class > 0:
                    prev = slice((j - 1) * ta, j * ta)
                    s.append(jnp.where(col >= row, _nt(qj, kd_ref[prev, :]), -jnp.inf))
                    v.append(vd_ref[prev, :])
                scores.append(s)
                values.append(v)
            for j, (acc, m, l) in zip(batch, _softmax_chunks_many(scores, values)):
                rho, at = divmod(j, tiles_per_class)
                tokens = pl.ds(dil * at * ta + rho, ta, stride=dil) if dil > 1 else slice(j * ta, (j + 1) * ta)
                og_refs[gi][tokens, :] = acc / l
                lse_refs[gi][tokens, :] = jnp.broadcast_to(m + jnp.log(l), (ta, HEAD_DIM))

    for r0 in range(0, seq, PREP_ROWS):
        rows = slice(r0, r0 + PREP_ROWS)
        lses = [ref[rows, :] for ref in lse_refs]
        top = functools.reduce(jnp.maximum, lses)
        ws = [jnp.exp(x - top) for x in lses]
        total = functools.reduce(lambda a, b: a + b, ws)
        out = functools.reduce(lambda a, b: a + b, [(w / total) * ref[rows, :] for w, ref in zip(ws, og_refs)])
        o_ref[rows, :] = out.astype(o_ref.dtype)


def dilated_attention(p, q_col, k_col, v_col, gq, gk, cos, sin, batch, seq):
    head = lambda c0: pl.BlockSpec((seq, HEAD_DIM), lambda b, h: (b, c0 + h))
    table = pl.BlockSpec((seq, HEAD_DIM), lambda b, h: (b, 0))
    row_spec = pl.BlockSpec((1, HEAD_DIM), lambda b, h: (0, 0))
    groups = range(DIL_GROUPS)
    return pl.pallas_call(
        functools.partial(_dil_kernel, seq=seq),
        grid=(batch, DIL_HEADS),
        in_specs=([head(q_col + gi * DIL_HEADS) for gi in groups]
                  + [head(k_col + gi * DIL_HEADS) for gi in groups]
                  + [head(v_col + gi * DIL_HEADS) for gi in groups]
                  + [table, table, row_spec, row_spec]),
        out_specs=pl.BlockSpec((seq, HEAD_DIM), lambda b, h: (b, h)),
        out_shape=jax.ShapeDtypeStruct((batch * seq, DIL_HEADS * HEAD_DIM), BF16),
        scratch_shapes=[pltpu.VMEM((seq, HEAD_DIM), F32)]
                       + [pltpu.VMEM((seq, HEAD_DIM), BF16)] * 3
                       + [pltpu.VMEM((seq, HEAD_DIM), F32)] * (2 * DIL_GROUPS),
        compiler_params=_params(2),
        name="dilated",
    )(*([p] * 9), cos, sin, _row1(gq), _row1(gk))


def _xattn_kernel(q_ref, kv_ref, gq_ref, gk_ref, o_ref):
    gq, gk = gq_ref[...], gk_ref[...]
    width = XATTN_HEADS * HEAD_DIM
    for h in range(XATTN_HEADS):
        hs = slice(h * HEAD_DIM, (h + 1) * HEAD_DIM)
        q = (_rms(q_ref[:, hs], gq) * SCALE).astype(BF16)
        k = _rms(kv_ref[:, hs], gk).astype(BF16)
        v = kv_ref[:, width + h * HEAD_DIM:width + (h + 1) * HEAD_DIM].astype(BF16)
        s = _nt(q, k)
        e = jnp.exp(s - jnp.max(s, axis=-1, keepdims=True))
        p = e / jnp.sum(e, axis=-1, keepdims=True)
        o_ref[:, hs] = jnp.dot(p.astype(BF16), v, preferred_element_type=F32).astype(o_ref.dtype)


def memory_cross_attention(q, kv, gq, gk, batch, seq, mem_len, tq=512):
    tq = min(tq, seq)
    nq = seq // tq
    width = XATTN_HEADS * HEAD_DIM
    return pl.pallas_call(
        _xattn_kernel,
        grid=(batch, nq),
        in_specs=[pl.BlockSpec((tq, width), lambda b, i: (b * nq + i, 0)),
                  pl.BlockSpec((mem_len, 2 * width), lambda b, i: (b, 0)),
                  pl.BlockSpec((1, HEAD_DIM), lambda b, i: (0, 0)),
                  pl.BlockSpec((1, HEAD_DIM), lambda b, i: (0, 0))],
        out_specs=pl.BlockSpec((tq, width), lambda b, i: (b * nq + i, 0)),
        out_shape=jax.ShapeDtypeStruct((batch * seq, width), BF16),
        compiler_params=_params(2),
        name="xattn",
    )(q, kv, _row1(gq), _row1(gk))


def _even_mixer(x2, h, cos, sin, batch, seq, e, w_in, w_out, moba_gq, moba_gk, nsa_gq, nsa_gk_cmp,
                nsa_gk_slc, nsa_gk_win, pe_k, pe_v, phi_k, phi_v):
    hd = HEAD_DIM
    main = (3 * MOBA_HEADS + NSA_HEADS + 6 * NSA_GROUPS) * hd
    pad = -w_in.shape[-1] % hd
    p = matmul(h, jnp.pad(w_in[e].astype(BF16), ((0, 0), (0, pad))), tm=BIG_ROW_TILE)

    o_a = moba_attention(p, 0, 16, 32, moba_gq, moba_gk, cos, sin, batch, seq)
    kcmp, vcmp = nsa_compress(p, 64, 68, pe_k, pe_v, phi_k, phi_v, nsa_gk_cmp, cos, sin, batch, seq)
    o_b = nsa_attention_batched(p, 48, 72, 76, 80, 84, main // hd, kcmp, vcmp, nsa_gq, nsa_gk_slc,
                                nsa_gk_win, cos, sin, batch, seq)

    return matmul([o_a, o_b], w_out, layer=e, residual=x2)


def _odd_mixer(x2, h, cos, sin, batch, seq, o, w_in, w_out, dil_gq, dil_gk):
    hd = HEAD_DIM
    p = matmul(h, w_in, layer=o, tm=BIG_ROW_TILE)
    o_c = sb_attention(p, 0, SB_HEADS, 2 * SB_HEADS, batch, seq)
    nd = DIL_GROUPS * DIL_HEADS
    o_d = dilated_attention(p, 3 * SB_HEADS, 3 * SB_HEADS + nd, 3 * SB_HEADS + 2 * nd, dil_gq, dil_gk,
                            cos, sin, batch, seq)
    return matmul([o_c, o_d], w_out, layer=o, residual=x2)


def kernel(x, mem, positions, mix_norm, even_w_in, even_w_out, moba_gq, moba_gk, nsa_gq, nsa_gk_cmp, nsa_gk_slc, nsa_gk_win, nsa_pe_k, nsa_pe_v, nsa_phi_k, nsa_phi_v, odd_w_in, odd_w_out, dil_gq, dil_gk, xattn_norm, mem_norm, xattn_wq, xattn_wkv, xattn_wo, xattn_gq, xattn_gk, ffn_norm, ffn_wg, ffn_wu, ffn_wd):
    batch, seq, d = x.shape
    mem_len = mem.shape[1]
    depth = mix_norm.shape[0]
    x2 = x.reshape(batch * seq, d)
    mem2 = mem.reshape(batch * mem_len, d)
    cos, sin = rope_tables(positions)

    for layer in range(depth):
        h = rmsnorm_rows(x2, mix_norm[layer])
        if layer % 2 == 0:
            e = layer // 2
            x2 = _even_mixer(x2, h, cos, sin, batch, seq, e, even_w_in, even_w_out, moba_gq[e],
                             moba_gk[e], nsa_gq[e], nsa_gk_cmp[e], nsa_gk_slc[e], nsa_gk_win[e],
                             nsa_pe_k[e], nsa_pe_v[e], nsa_phi_k[e], nsa_phi_v[e])
        else:
            o = layer // 2
            x2 = _odd_mixer(x2, h, cos, sin, batch, seq, o, odd_w_in, odd_w_out, dil_gq[o], dil_gk[o])

        q = norm_matmul(x2, xattn_norm[layer], xattn_wq, layer)
        kv = norm_matmul(mem2, mem_norm[layer], xattn_wkv, layer)
        o_x = memory_cross_attention(q, kv, xattn_gq[layer], xattn_gk[layer], batch, seq, mem_len)
        x2, h = matmul_residual_norm(o_x, xattn_wo, layer, x2, ffn_norm[layer])
        hidden = swiglu_gate_up(h, ffn_wg, ffn_wu, layer)
        x2 = matmul(hidden, ffn_wd, layer=layer, residual=x2, tm=BIG_ROW_TILE)

    return x2.reshape(batch, seq, d)
```

```python
import functools

import jax
import jax.numpy as jnp
from jax import lax
from jax.experimental import pallas as pl
from jax.experimental.pallas import tpu as pltpu

F32 = jnp.float32
BF16 = jnp.bfloat16
HIGHEST = lax.Precision.HIGHEST

HEAD_DIM = 128
HALF = HEAD_DIM // 2
ROPE_THETA = 10000.0
NORM_EPS = 1e-6
TINY = 1e-30
SCALE = HEAD_DIM ** -0.5

MOBA_HEADS = 16
MOBA_BLOCK = 256
MOBA_TOPK = 3
MOBA_BLOCK_BATCH = 4
NSA_HEADS = 16
NSA_GROUPS = 4
NSA_HG = NSA_HEADS // NSA_GROUPS
NSA_CMP_LEN = 32
NSA_CMP_STRIDE = 16
NSA_SEL_BLOCK = 64
NSA_SEL_SHIFT = NSA_SEL_BLOCK.bit_length() - 1
NSA_SEL_TOPN = 16
NSA_WINDOW = 512
DIL_CONFIGS = ((128, 1), (512, 4), (2048, 16))
DIL_GROUPS = len(DIL_CONFIGS)
DIL_HEADS = 8
DIL_SPAN = 128
DIL_TILE_BATCH = 8
SB_HEADS = 24
XATTN_HEADS = 4

VMEM_LIMIT_BYTES = 56 * 1024 * 1024
NT_DIMS = (((1,), (1,)), ((), ()))
PREP_ROWS = 256
MATMUL_ROW_TILE = 1024
BIG_ROW_TILE = 2048


def _params(n_grid):
    return pltpu.CompilerParams(dimension_semantics=("arbitrary",) * n_grid,
                                vmem_limit_bytes=VMEM_LIMIT_BYTES)


def _nt(a, b, precision=None):
    return lax.dot_general(a, b, NT_DIMS, precision=precision, preferred_element_type=F32)


def _rms(x, g):
    return x * lax.rsqrt(jnp.mean(x * x, axis=-1, keepdims=True) + NORM_EPS) * g


def _rope(x, cos, sin_signed):
    return x * cos + pltpu.roll(x, HALF, 1) * sin_signed


def _softmax_chunks_many(scores, values):
    ms = [jnp.max(functools.reduce(jnp.maximum, s), axis=-1, keepdims=True) for s in scores]
    probs = [[jnp.exp(c - m) for c in s] for s, m in zip(scores, ms)]
    ls = [jnp.sum(functools.reduce(lambda a, b: a + b, p), axis=-1, keepdims=True) for p in probs]
    accs = []
    for p, v in zip(probs, values):
        acc = None
        for pc, vc in zip(p, v):
            part = jnp.dot(pc.astype(BF16), vc, preferred_element_type=F32)
            acc = part if acc is None else acc + part
        accs.append(acc)
    return list(zip(accs, ms, ls))


def _top_rank_many(vals_list):
    idx = lax.broadcasted_iota(jnp.int32, vals_list[0].shape, 0)
    ranks = [jnp.zeros(v.shape, jnp.int32) for v in vals_list]
    for m in range(vals_list[0].shape[0]):
        for n, vals in enumerate(vals_list):
            c = vals[m:m + 1, :]
            ranks[n] = ranks[n] + ((c > vals) | ((c == vals) & (idx > m))).astype(jnp.int32)
    return ranks


def _top_rank(vals):
    return _top_rank_many([vals])[0]


def _to_columns(x_t):
    rows = x_t.shape[1]
    eye = (lax.broadcasted_iota(jnp.int32, (rows, rows), 0)
           == lax.broadcasted_iota(jnp.int32, (rows, rows), 1)).astype(BF16)
    return _nt(eye, x_t.astype(BF16))


def _rmsnorm_kernel(x_ref, g_ref, o_ref):
    o_ref[...] = _rms(x_ref[...], g_ref[...]).astype(o_ref.dtype)


def rmsnorm_rows(x, g, tm=256):
    m, d = x.shape
    tm = min(tm, m)
    return pl.pallas_call(
        _rmsnorm_kernel,
        grid=(m // tm,),
        in_specs=[pl.BlockSpec((tm, d), lambda i: (i, 0)), pl.BlockSpec((1, d), lambda i: (0, 0))],
        out_specs=pl.BlockSpec((tm, d), lambda i: (i, 0)),
        out_shape=jax.ShapeDtypeStruct((m, d), BF16),
        compiler_params=_params(1),
        name="rmsnorm",
    )(x, g.reshape(1, d))


def _matmul_kernel(*refs, n_lhs, has_res):
    a_refs, w_refs = refs[:n_lhs], refs[n_lhs:2 * n_lhs]
    out = None
    for a_ref, w_ref in zip(a_refs, w_refs):
        part = jnp.dot(a_ref[...], w_ref[...].astype(BF16), preferred_element_type=F32)
        out = part if out is None else out + part
    if has_res:
        out = out + refs[2 * n_lhs][...]
    refs[-1][...] = out.astype(refs[-1].dtype)


def _contraction_tile(kdim, limit=6144):
    if kdim <= limit:
        return kdim
    return max(t for t in range(HEAD_DIM, limit + 1, HEAD_DIM) if kdim % t == 0)


def _matmul_call(pieces, w, layer, n, residual, out_dtype, tm):
    m = pieces[0][0].shape[0]
    tn = 512 if sum(k for _, _, k, _ in pieces) <= 4096 else 256
    tm, tn = min(tm, m), min(tn, n)
    assert m % tm == 0
    lhs_mode = dict(pipeline_mode=pl.Buffered(1)) if tm > MATMUL_ROW_TILE else {}
    a_specs, w_specs = [], []
    for _, col0, k, row0 in pieces:
        assert col0 % k == 0 and row0 % k == 0
        a_specs.append(pl.BlockSpec((tm, k), lambda i, j, c=col0 // k: (i, c), **lhs_mode))
        if layer is None:
            w_specs.append(pl.BlockSpec((k, tn), lambda i, j, r=row0 // k: (r, j)))
        else:
            w_specs.append(pl.BlockSpec((None, k, tn), lambda i, j, r=row0 // k: (layer, r, j)))
    args = [a for a, _, _, _ in pieces] + [w] * len(pieces)
    in_specs = a_specs + w_specs
    if residual is not None:
        in_specs.append(pl.BlockSpec((tm, tn), lambda i, j: (i, j)))
        args.append(residual)
    return pl.pallas_call(
        functools.partial(_matmul_kernel, n_lhs=len(pieces), has_res=residual is not None),
        grid=(m // tm, pl.cdiv(n, tn)),
        in_specs=in_specs,
        out_specs=pl.BlockSpec((tm, tn), lambda i, j: (i, j)),
        out_shape=jax.ShapeDtypeStruct((m, n), out_dtype),
        compiler_params=_params(2),
        name="matmul",
    )(*args)


def matmul(lhs, w, layer=None, residual=None, out_dtype=F32, tm=MATMUL_ROW_TILE):
    arrays = lhs if isinstance(lhs, (list, tuple)) else [lhs]
    n = w.shape[-1]
    if len(arrays) > 1:
        pieces, row0 = [], 0
        for a in arrays:
            pieces.append((a, 0, a.shape[1], row0))
            row0 += a.shape[1]
        return _matmul_call(pieces, w, layer, n, residual, out_dtype, tm)
    a = arrays[0]
    kdim = a.shape[1]
    tk = _contraction_tile(kdim)
    out = residual
    for c in range(kdim // tk):
        last = c == kdim // tk - 1
        out = _matmul_call([(a, c * tk, tk, c * tk)], w, layer, n, out, out_dtype if last else F32, tm)
    return out


def _matmul_norm_kernel(a_ref, w_ref, r_ref, g_ref, x_ref, h_ref):
    x = jnp.dot(a_ref[...], w_ref[...].astype(BF16), preferred_element_type=F32) + r_ref[...]
    x_ref[...] = x
    h_ref[...] = _rms(x, g_ref[...]).astype(h_ref.dtype)


def matmul_residual_norm(a, w, layer, residual, gain, tm=256):
    m, kdim = a.shape
    n = w.shape[-1]
    tm = min(tm, m)
    rows = lambda width: pl.BlockSpec((tm, width), lambda i: (i, 0))
    return pl.pallas_call(
        _matmul_norm_kernel,
        grid=(m // tm,),
        in_specs=[rows(kdim), pl.BlockSpec((None, kdim, n), lambda i: (layer, 0, 0)), rows(n),
                  pl.BlockSpec((1, n), lambda i: (0, 0))],
        out_specs=[rows(n), rows(n)],
        out_shape=[jax.ShapeDtypeStruct((m, n), F32), jax.ShapeDtypeStruct((m, n), BF16)],
        compiler_params=_params(1),
        name="matmul_norm",
    )(a, w, residual, gain.reshape(1, n))


def _norm_matmul_kernel(x_ref, g_ref, w_ref, o_ref):
    h = _rms(x_ref[...], g_ref[...]).astype(BF16)
    o_ref[...] = jnp.dot(h, w_ref[...].astype(BF16), preferred_element_type=F32)


def norm_matmul(x, gain, w, layer, tm=512):
    m, d = x.shape
    n = w.shape[-1]
    tm = min(tm, m)
    return pl.pallas_call(
        _norm_matmul_kernel,
        grid=(m // tm,),
        in_specs=[pl.BlockSpec((tm, d), lambda i: (i, 0)), pl.BlockSpec((1, d), lambda i: (0, 0)),
                  pl.BlockSpec((None, d, n), lambda i: (layer, 0, 0), pipeline_mode=pl.Buffered(1))],
        out_specs=pl.BlockSpec((tm, n), lambda i: (i, 0)),
        out_shape=jax.ShapeDtypeStruct((m, n), F32),
        compiler_params=_params(1),
        name="norm_matmul",
    )(x, gain.reshape(1, d), w)


def _gate_up_kernel(a_ref, wg_ref, wu_ref, o_ref):
    a = a_ref[...]
    g = jnp.dot(a, wg_ref[...].astype(BF16), preferred_element_type=F32)
    u = jnp.dot(a, wu_ref[...].astype(BF16), preferred_element_type=F32)
    o_ref[...] = (g * jax.nn.sigmoid(g) * u).astype(o_ref.dtype)


def swiglu_gate_up(a, wg, wu, layer, tm=BIG_ROW_TILE, tn=256):
    m, kdim = a.shape
    n = wg.shape[-1]
    tm, tn = min(tm, m), min(tn, n)
    w_spec = pl.BlockSpec((None, kdim, tn), lambda i, j: (layer, 0, j))
    lhs_mode = dict(pipeline_mode=pl.Buffered(1)) if tm > MATMUL_ROW_TILE else {}
    return pl.pallas_call(
        _gate_up_kernel,
        grid=(m // tm, pl.cdiv(n, tn)),
        in_specs=[pl.BlockSpec((tm, kdim), lambda i, j: (i, 0), **lhs_mode), w_spec, w_spec],
        out_specs=pl.BlockSpec((tm, tn), lambda i, j: (i, j)),
        out_shape=jax.ShapeDtypeStruct((m, n), BF16),
        compiler_params=_params(2),
        name="swiglu_gate_up",
    )(a, wg, wu)


def _rope_table_kernel(pos_ref, invf_ref, cos_ref, sin_ref):
    ang = pos_ref[...] * invf_ref[...]
    lane = lax.broadcasted_iota(jnp.int32, ang.shape, 1)
    s = jnp.sin(ang)
    cos_ref[...] = jnp.cos(ang)
    sin_ref[...] = jnp.where(lane < HALF, -s, s)


def rope_tables(positions, tr=256):
    n = positions.size
    tr = min(tr, n)
    inv_freq = ROPE_THETA ** (-jnp.arange(HALF, dtype=F32) / HALF)
    invf = jnp.concatenate([inv_freq, inv_freq]).reshape(1, HEAD_DIM)
    pos = jnp.broadcast_to(positions.astype(F32).reshape(n, 1), (n, HEAD_DIM))
    spec = pl.BlockSpec((tr, HEAD_DIM), lambda i: (i, 0))
    return pl.pallas_call(
        _rope_table_kernel,
        grid=(n // tr,),
        in_specs=[spec, pl.BlockSpec((1, HEAD_DIM), lambda i: (0, 0))],
        out_specs=[spec, spec],
        out_shape=[jax.ShapeDtypeStruct((n, HEAD_DIM), F32)] * 2,
        compiler_params=_params(1),
        name="rope_tables",
    )(pos, invf)


def _prep_keys(k_ref, g, cos_ref, sin_ref, out_ref, seq):
    for r0 in range(0, seq, PREP_ROWS):
        rows = slice(r0, min(r0 + PREP_ROWS, seq))
        out_ref[rows, :] = _rope(_rms(k_ref[rows, :], g), cos_ref[rows, :], sin_ref[rows, :]).astype(BF16)


def _row1(x):
    return x.reshape(1, HEAD_DIM)


def _moba_kernel(q_ref, k_ref, v_ref, cos_ref, sin_ref, gq_ref, gk_ref, o_ref, kb_ref, vb_ref, *, nblk):
    blk = MOBA_BLOCK
    means = []
    for n in range(nblk):
        rows = slice(n * blk, (n + 1) * blk)
        k = _rope(_rms(k_ref[rows, :], gk_ref[...]), cos_ref[rows, :], sin_ref[rows, :])
        kb_ref[rows, :] = k.astype(BF16)
        means.append(jnp.mean(k, axis=0, keepdims=True))
    vb_ref[...] = v_ref[...].astype(BF16)
    row = lax.broadcasted_iota(jnp.int32, (blk, blk), 0)
    col = lax.broadcasted_iota(jnp.int32, (blk, blk), 1)

    qbs, sels = [], []
    for i in range(nblk):
        own = slice(i * blk, (i + 1) * blk)
        q = _rope(_rms(q_ref[own, :], gq_ref[...]), cos_ref[own, :], sin_ref[own, :])
        qbs.append((q * SCALE).astype(BF16))
        sel = None
        if i > MOBA_TOPK:
            gate = _nt(jnp.concatenate(means[:i], axis=0), q, precision=HIGHEST)
            sel = _to_columns((_top_rank(gate) < MOBA_TOPK).astype(F32))
        sels.append(sel)

    for i0 in range(0, nblk, MOBA_BLOCK_BATCH):
        batch = range(i0, min(i0 + MOBA_BLOCK_BATCH, nblk))
        scores, values = [], []
        for i in batch:
            own = slice(i * blk, (i + 1) * blk)
            s, v = [jnp.where(col <= row, _nt(qbs[i], kb_ref[own, :]), -jnp.inf)], [vb_ref[own, :]]
            for n in range(i):
                rows = slice(n * blk, (n + 1) * blk)
                sn = _nt(qbs[i], kb_ref[rows, :])
                if sels[i] is not None:
                    sn = jnp.where(sels[i][:, n:n + 1] > 0.5, sn, -jnp.inf)
                s.append(sn)
                v.append(vb_ref[rows, :])
            scores.append(s)
            values.append(v)
        for i, (acc, _, l) in zip(batch, _softmax_chunks_many(scores, values)):
            o_ref[i * blk:(i + 1) * blk, :] = (acc / l).astype(o_ref.dtype)


def moba_attention(p, q_col, k_col, v_col, gq, gk, cos, sin, batch, seq):
    nblk = seq // MOBA_BLOCK
    head = lambda c0: pl.BlockSpec((seq, HEAD_DIM), lambda b, h: (b, c0 + h))
    table = pl.BlockSpec((seq, HEAD_DIM), lambda b, h: (b, 0))
    row_spec = pl.BlockSpec((1, HEAD_DIM), lambda b, h: (0, 0))
    return pl.pallas_call(
        functools.partial(_moba_kernel, nblk=nblk),
        grid=(batch, MOBA_HEADS),
        in_specs=[head(q_col), head(k_col), head(v_col), table, table, row_spec, row_spec],
        out_specs=pl.BlockSpec((seq, HEAD_DIM), lambda b, h: (b, h)),
        out_shape=jax.ShapeDtypeStruct((batch * seq, MOBA_HEADS * HEAD_DIM), BF16),
        scratch_shapes=[pltpu.VMEM((seq, HEAD_DIM), BF16), pltpu.VMEM((seq, HEAD_DIM), BF16)],
        compiler_params=_params(2),
        name="moba",
    )(p, p, p, cos, sin, _row1(gq), _row1(gk))


def _nsa_compress_kernel(kc_ref, vc_ref, pek_ref, pev_ref, phik_ref, phiv_ref, g_ref, cos_ref, sin_ref,
                         ko_ref, vo_ref, *, nslot):
    half = NSA_CMP_LEN // 2

    def compress(x_ref, pe_ref, phi_ref):
        first = jnp.zeros((nslot, HEAD_DIM), F32)
        second = jnp.zeros((nslot, HEAD_DIM), F32)
        for l in range(half):
            xl = x_ref[pl.ds(l, nslot, stride=NSA_CMP_STRIDE), :]
            first += jnp.dot((xl + pe_ref[l:l + 1, :]).astype(BF16), phi_ref[l].astype(BF16),
                             preferred_element_type=F32)
            second += jnp.dot((xl + pe_ref[half + l:half + l + 1, :]).astype(BF16),
                              phi_ref[half + l].astype(BF16), preferred_element_type=F32)
        return pltpu.roll(first, 1, 0) + second

    ends = pl.ds(NSA_CMP_STRIDE - 1, nslot, stride=NSA_CMP_STRIDE)
    kc = compress(kc_ref, pek_ref, phik_ref)
    kc = _rope(_rms(kc, g_ref[...]), cos_ref[ends, :], sin_ref[ends, :])
    ko_ref[0, 0] = kc.astype(BF16)
    vo_ref[0, 0] = compress(vc_ref, pev_ref, phiv_ref).astype(BF16)


def nsa_compress(p, kc_col, vc_col, pe_k, pe_v, phi_k, phi_v, g_kc, cos, sin, batch, seq):
    nslot = seq // NSA_CMP_STRIDE
    full = lambda shape: pl.BlockSpec(shape, lambda b, g: (0,) * len(shape))
    out_spec = pl.BlockSpec((1, 1, nslot, HEAD_DIM), lambda b, g: (b, g, 0, 0))
    out_shape = jax.ShapeDtypeStruct((batch, NSA_GROUPS, nslot, HEAD_DIM), BF16)
    return pl.pallas_call(
        functools.partial(_nsa_compress_kernel, nslot=nslot),
        grid=(batch, NSA_GROUPS),
        in_specs=[pl.BlockSpec((seq, HEAD_DIM), lambda b, g: (b, kc_col + g)),
                  pl.BlockSpec((seq, HEAD_DIM), lambda b, g: (b, vc_col + g)),
                  full((NSA_CMP_LEN, HEAD_DIM)), full((NSA_CMP_LEN, HEAD_DIM)),
                  full((NSA_CMP_LEN, HEAD_DIM, HEAD_DIM)), full((NSA_CMP_LEN, HEAD_DIM, HEAD_DIM)),
                  full((1, HEAD_DIM)),
                  pl.BlockSpec((seq, HEAD_DIM), lambda b, g: (b, 0)),
                  pl.BlockSpec((seq, HEAD_DIM), lambda b, g: (b, 0))],
        out_specs=[out_spec, out_spec],
        out_shape=[out_shape, out_shape],
        compiler_params=_params(2),
        name="nsa_compress",
    )(p, p, pe_k, pe_v, phi_k, phi_v, _row1(g_kc), cos, sin)


NSA_QUERY_TILE = 256


NSA_GROUP_BATCH = 2


def _nsa_batch_kernel(q_ref, gl_ref, kc_ref, vc_ref, ks_ref, vs_ref, kw_ref, vw_ref, cos_ref, sin_ref,
                      gq_ref, gks_ref, gkw_ref, o_ref, ksb_ref, vsb_ref, kwb_ref, vwb_ref,
                      s_ref, top_ref, sum_ref, acc_ref, *, tq, seq):
    hg = NSA_HG
    groups = range(NSA_GROUP_BATCH)
    nslot = seq // NSA_CMP_STRIDE
    nsel = seq // NSA_SEL_BLOCK
    topn = min(NSA_SEL_TOPN, nsel)
    qi = pl.program_id(2)
    q0 = qi * tq
    lanes = [slice(g * HEAD_DIM, (g + 1) * HEAD_DIM) for g in groups]

    @pl.when(qi == 0)
    def _():
        for g in groups:
            _prep_keys(ks_ref.at[:, lanes[g]], gks_ref[...], cos_ref, sin_ref, ksb_ref.at[:, lanes[g]], seq)
            _prep_keys(kw_ref.at[:, lanes[g]], gkw_ref[...], cos_ref, sin_ref, kwb_ref.at[:, lanes[g]], seq)
        vsb_ref[...] = vs_ref[...].astype(BF16)
        vwb_ref[...] = vw_ref[...].astype(BF16)

    def tile_heads(x):
        return jnp.concatenate([x] * hg, axis=0)

    row = lax.broadcasted_iota(jnp.int32, (tq, tq), 0)
    col = lax.broadcasted_iota(jnp.int32, (tq, tq), 1)
    per_sel = NSA_SEL_BLOCK // NSA_CMP_STRIDE
    span = NSA_CMP_LEN // NSA_CMP_STRIDE
    nback = NSA_WINDOW // tq

    own = pl.ds(pl.multiple_of(q0, tq), tq)
    cos, sin, gq = cos_ref[own, :], sin_ref[own, :], gq_ref[...]
    qst = []
    for g in groups:
        qs = []
        for h in range(hg):
            c0 = (g * hg + h) * HEAD_DIM
            qs.append((_rope(_rms(q_ref[:, c0:c0 + HEAD_DIM], gq), cos, sin) * SCALE).astype(BF16))
        qst.append(jnp.concatenate(qs, axis=0))

    scores_w, values_w = [[] for _ in groups], [[] for _ in groups]
    for d in range(nback + 1):
        keys = pl.ds(pl.multiple_of(jnp.maximum(qi - d, 0) * tq, tq), tq)
        exists = jnp.where(qi >= d, 0.0, -jnp.inf)
        if d == 0:
            bias = tile_heads(jnp.where(col <= row, 0.0, -jnp.inf))
        elif d == nback:
            bias = tile_heads(jnp.where(col > row, exists, -jnp.inf))
        else:
            bias = exists
        for g in groups:
            scores_w[g].append(_nt(qst[g], kwb_ref[keys, lanes[g]]) + bias)
            values_w[g].append(vwb_ref[keys, lanes[g]])

    slot = lax.broadcasted_iota(jnp.int32, (tq, nslot), 1)
    tpos = q0 + lax.broadcasted_iota(jnp.int32, (tq, nslot), 0)
    valid = (slot >= 1) & (slot * NSA_CMP_STRIDE + (NSA_CMP_STRIDE - 1) <= tpos)
    bias_c = tile_heads(jnp.where(valid, 0.0, -jnp.inf))
    s_c = [_nt(qst[g], kc_ref[0, g]) + bias_c for g in groups]
    m_c = [jnp.max(s, axis=-1, keepdims=True) for s in s_c]
    m_c = [jnp.where(m == -jnp.inf, 0.0, m) for m in m_c]
    e_c = [jnp.exp(s - m) for s, m in zip(s_c, m_c)]
    p_c = [e / jnp.maximum(jnp.sum(e, axis=-1, keepdims=True), TINY) for e in e_c]
    o_c = [jnp.dot(p_c[g].astype(BF16), vc_ref[0, g], preferred_element_type=F32) for g in groups]

    b_idx = lax.broadcasted_iota(jnp.int32, (nsel, nslot), 0)
    j_idx = lax.broadcasted_iota(jnp.int32, (nsel, nslot), 1)
    overlap = ((j_idx >= 1) & (j_idx - 1 > per_sel * b_idx - span)
               & (j_idx - 1 < per_sel * (b_idx + 1))).astype(F32)
    blk = lax.broadcasted_iota(jnp.int32, (nsel, tq), 0)
    cur = (q0 + lax.broadcasted_iota(jnp.int32, (nsel, tq), 1)) >> NSA_SEL_SHIFT
    forced = (blk == cur) | (blk == 0)
    imps = []
    for g in groups:
        p_sum = p_c[g][0:tq]
        for h in range(1, hg):
            p_sum = p_sum + p_c[g][h * tq:(h + 1) * tq]
        imp = _nt(overlap, p_sum, precision=HIGHEST)
        imps.append(jnp.where(forced, jnp.inf, jnp.where(blk <= cur, imp, -jnp.inf)))
    o_w = [acc / l for acc, _, l in _softmax_chunks_many(scores_w, values_w)]
    sels = [_to_columns((rank < topn).astype(F32)).astype(BF16) for rank in _top_rank_many(imps)]

    e_row = lax.broadcasted_iota(jnp.int32, (nsel, tq), 0)
    e_col = lax.broadcasted_iota(jnp.int32, (nsel, tq), 1)
    top_ref[...] = jnp.full(top_ref.shape, -jnp.inf, F32)

    def score_chunk(c, carry):
        keys = pl.ds(pl.multiple_of(c * tq, tq), tq)
        expand = (((c * tq + e_col) >> NSA_SEL_SHIFT) == e_row).astype(BF16)
        causal = c * tq + col <= q0 + row
        scores = [_nt(qst[g], ksb_ref[keys, lanes[g]]) for g in groups]
        for g in groups:
            keep = (jnp.dot(sels[g], expand, preferred_element_type=F32) > 0.5) & causal
            s = scores[g] + tile_heads(jnp.where(keep, 0.0, -jnp.inf))
            s_ref[g, c] = s
            top_ref[g] = jnp.maximum(top_ref[g], s)
        return carry

    lax.fori_loop(0, qi + 1, score_chunk, 0)
    m_s = [jnp.max(top_ref[g], axis=-1, keepdims=True) for g in groups]
    sum_ref[...] = jnp.zeros(sum_ref.shape, F32)
    acc_ref[...] = jnp.zeros(acc_ref.shape, F32)

    def weigh_chunk(c, carry):
        keys = pl.ds(pl.multiple_of(c * tq, tq), tq)
        probs = [jnp.exp(s_ref[g, c] - m_s[g]) for g in groups]
        for g in groups:
            sum_ref[g] += probs[g]
            acc_ref[g] += jnp.dot(probs[g].astype(BF16), vsb_ref[keys, lanes[g]], preferred_element_type=F32)
        return carry

    lax.fori_loop(0, qi + 1, weigh_chunk, 0)

    gl = gl_ref[...]
    for g in groups:
        o_s = acc_ref[g] / jnp.sum(sum_ref[g], axis=-1, keepdims=True)
        group = pl.program_id(1) * NSA_GROUP_BATCH + g
        gates = jax.nn.sigmoid(pltpu.roll(gl, (HEAD_DIM - 3 * hg * group) % HEAD_DIM, 1))
        for h in range(hg):
            hs = slice(h * tq, (h + 1) * tq)
            out = (gates[:, 3 * h:3 * h + 1] * o_c[g][hs] + gates[:, 3 * h + 1:3 * h + 2] * o_s[hs]
                   + gates[:, 3 * h + 2:3 * h + 3] * o_w[g][hs])
            c0 = (g * hg + h) * HEAD_DIM
            o_ref[:, c0:c0 + HEAD_DIM] = out.astype(o_ref.dtype)


def nsa_attention_batched(p, q_col, ks_col, vs_col, kw_col, vw_col, gl_col, kcmp, vcmp, gq, gks, gkw,
                          cos, sin, batch, seq):
    tq = min(NSA_QUERY_TILE, seq)
    nq = seq // tq
    gb = NSA_GROUP_BATCH
    gw = gb * NSA_HG * HEAD_DIM
    rows = NSA_HG * tq
    nslot = seq // NSA_CMP_STRIDE
    assert NSA_GROUPS % gb == 0 and all(c % gb == 0 for c in (ks_col, vs_col, kw_col, vw_col))
    heads = lambda c0: pl.BlockSpec((seq, gb * HEAD_DIM), lambda b, g, i: (b, c0 // gb + g),
                                    pipeline_mode=pl.Buffered(1))
    table = pl.BlockSpec((seq, HEAD_DIM), lambda b, g, i: (b, 0))
    row_spec = pl.BlockSpec((1, HEAD_DIM), lambda b, g, i: (0, 0))
    cmp_spec = pl.BlockSpec((1, gb, nslot, HEAD_DIM), lambda b, g, i: (b, g, 0, 0))
    return pl.pallas_call(
        functools.partial(_nsa_batch_kernel, tq=tq, seq=seq),
        grid=(batch, NSA_GROUPS // gb, nq),
        in_specs=[pl.BlockSpec((tq, gw), lambda b, g, i: (b * nq + i, q_col // (gb * NSA_HG) + g)),
                  pl.BlockSpec((tq, HEAD_DIM), lambda b, g, i: (b * nq + i, gl_col)),
                  cmp_spec, cmp_spec,
                  heads(ks_col), heads(vs_col), heads(kw_col), heads(vw_col), table, table,
                  row_spec, row_spec, row_spec],
        out_specs=pl.BlockSpec((tq, gw), lambda b, g, i: (b * nq + i, g)),
        out_shape=jax.ShapeDtypeStruct((batch * seq, NSA_HEADS * HEAD_DIM), BF16),
        scratch_shapes=[pltpu.VMEM((seq, gb * HEAD_DIM), BF16)] * 4
                       + [pltpu.VMEM((gb, nq, rows, tq), F32),
                          pltpu.VMEM((gb, rows, tq), F32), pltpu.VMEM((gb, rows, tq), F32),
                          pltpu.VMEM((gb, rows, HEAD_DIM), F32)],
        compiler_params=_params(3),
        name="nsa",
    )(p, p, kcmp, vcmp, p, p, p, p, cos, sin, _row1(gq), _row1(gks), _row1(gkw))


def _sb_kernel(q_ref, k_ref, v_ref, o_ref, kb_ref, vb_ref, *, tq, tk, seq):
    per_q = tq // tk
    kb_ref[...] = k_ref[...].astype(BF16)
    vb_ref[...] = v_ref[...].astype(BF16)

    later = (lax.broadcasted_iota(jnp.int32, (tk, tk), 0)
             > lax.broadcasted_iota(jnp.int32, (tk, tk), 1)).astype(BF16)
    row = lax.broadcasted_iota(jnp.int32, (tq, tk), 0)
    col = lax.broadcasted_iota(jnp.int32, (tq, tk), 1)

    def tiles(qb, first, key0s, run, acc):
        keys = [slice((first - j) * tk, (first - j + 1) * tk) for j in range(len(key0s))]
        zs = [_nt(qb, kb_ref[kk, :]) for kk in keys]
        log_betas, log_rests, stricts = [], [], []
        for z, key0 in zip(zs, key0s):
            log_beta = jnp.minimum(z, 0.0) - jnp.log(1.0 + jnp.exp(-jnp.abs(z)))
            log_rest = log_beta - z
            strict = None
            if key0 is not None:
                strict = key0 + col < row
                log_rest = jnp.where(strict, log_rest, 0.0)
            log_betas.append(log_beta)
            log_rests.append(log_rest)
            stricts.append(strict)
        insides = []
        for log_rest in log_rests:
            hi = log_rest.astype(BF16)
            lo = (log_rest - hi.astype(F32)).astype(BF16)
            both = jnp.dot(jnp.concatenate([hi, lo], axis=0), later, preferred_element_type=F32)
            insides.append(both[:tq] + both[tq:])
        for kk, log_beta, log_rest, strict, inside in zip(keys, log_betas, log_rests, stricts, insides):
            a = jnp.exp(log_beta + inside + jnp.concatenate([run] * (tk // HEAD_DIM), axis=1))
            if strict is not None:
                a = jnp.where(strict, a, 0.0)
            acc = acc + jnp.dot(a.astype(BF16), vb_ref[kk, :], preferred_element_type=F32)
            run = run + jnp.sum(log_rest, axis=-1, keepdims=True)
        return run, acc

    for qt in range(seq // tq):
        own = slice(qt * tq, (qt + 1) * tq)
        qb = (q_ref[own, :] * SCALE).astype(BF16)
        run = jnp.zeros((tq, HEAD_DIM), F32)
        acc = jnp.zeros((tq, HEAD_DIM), F32)
        run, acc = tiles(qb, (qt + 1) * per_q - 1, [(per_q - 1 - j) * tk for j in range(per_q)], run, acc)
        for i in range(qt):
            run, acc = tiles(qb, (qt - i) * per_q - 1, [None] * per_q, run, acc)
        o_ref[own, :] = acc.astype(o_ref.dtype)


def sb_attention(p, q_col, k_col, v_col, batch, seq, tq=512, tk=256):
    tq = min(tq, seq)
    tk = min(tk, tq)
    head = lambda c0: pl.BlockSpec((seq, HEAD_DIM), lambda b, h: (b, c0 + h))
    return pl.pallas_call(
        functools.partial(_sb_kernel, tq=tq, tk=tk, seq=seq),
        grid=(batch, SB_HEADS),
        in_specs=[head(q_col), head(k_col), head(v_col)],
        out_specs=pl.BlockSpec((seq, HEAD_DIM), lambda b, h: (b, h)),
        out_shape=jax.ShapeDtypeStruct((batch * seq, SB_HEADS * HEAD_DIM), BF16),
        scratch_shapes=[pltpu.VMEM((seq, HEAD_DIM), BF16), pltpu.VMEM((seq, HEAD_DIM), BF16)],
        compiler_params=_params(2),
        name="stick_breaking",
    )(p, p, p)


def _dil_kernel(q0_ref, q1_ref, q2_ref, k0_ref, k1_ref, k2_ref, v0_ref, v1_ref, v2_ref,
                cos_ref, sin_ref, gq_ref, gk_ref, o_ref,
                tmp_ref, qd_ref, kd_ref, vd_ref, og0_ref, og1_ref, og2_ref, lse0_ref, lse1_ref, lse2_ref,
                *, seq):
    q_refs, k_refs, v_refs = (q0_ref, q1_ref, q2_ref), (k0_ref, k1_ref, k2_ref), (v0_ref, v1_ref, v2_ref)
    og_refs, lse_refs = (og0_ref, og1_ref, og2_ref), (lse0_ref, lse1_ref, lse2_ref)
    ta = DIL_SPAN
    row = lax.broadcasted_iota(jnp.int32, (ta, ta), 0)
    col = lax.broadcasted_iota(jnp.int32, (ta, ta), 1)

    for gi, (window, dil) in enumerate(DIL_CONFIGS):
        assert window == dil * DIL_SPAN and seq % (dil * ta) == 0
        n_a = seq // dil
        tiles_per_class = n_a // ta

        def class_major(dst_ref, rows, val, dil=dil, n_a=n_a):
            if dil == 1:
                dst_ref[rows, :] = val.astype(BF16)
                return
            tmp_ref[rows, :] = val
            per = (rows.stop - rows.start) // dil
            a0 = rows.start // dil
            for rho in range(dil):
                src = pl.ds(rows.start + rho, per, stride=dil)
                dst_ref[rho * n_a + a0:rho * n_a + a0 + per, :] = tmp_ref[src, :].astype(BF16)

        for r0 in range(0, seq, PREP_ROWS):
            rows = slice(r0, r0 + PREP_ROWS)
            cos, sin = cos_ref[rows, :], sin_ref[rows, :]
            class_major(qd_ref, rows, _rope(_rms(q_refs[gi][rows, :], gq_ref[...]), cos, sin) * SCALE)
            class_major(kd_ref, rows, _rope(_rms(k_refs[gi][rows, :], gk_ref[...]), cos, sin))
            class_major(vd_ref, rows, v_refs[gi][rows, :])

        for j0 in range(0, seq // ta, DIL_TILE_BATCH):
            batch = range(j0, min(j0 + DIL_TILE_BATCH, seq // ta))
            scores, values = [], []
            for j in batch:
                rows = slice(j * ta, (j + 1) * ta)
                qj = qd_ref[rows, :]
                s, v = [jnp.where(col <= row, _nt(qj, kd_ref[rows, :]), -jnp.inf)], [vd_ref[rows, :]]
                if j % tiles_per_class > 0:
                    prev = slice((j - 1) * ta, j * ta)
                    s.append(jnp.where(col >= row, _nt(qj, kd_ref[prev, :]), -jnp.inf))
                    v.append(vd_ref[prev, :])
                scores.append(s)
                values.append(v)
            for j, (acc, m, l) in zip(batch, _softmax_chunks_many(scores, values)):
                rho, at = divmod(j, tiles_per_class)
                tokens = pl.ds(dil * at * ta + rho, ta, stride=dil) if dil > 1 else slice(j * ta, (j + 1) * ta)
                og_refs[gi][tokens, :] = acc / l
                lse_refs[gi][tokens, :] = jnp.broadcast_to(m + jnp.log(l), (ta, HEAD_DIM))

    for r0 in range(0, seq, PREP_ROWS):
        rows = slice(r0, r0 + PREP_ROWS)
        lses = [ref[rows, :] for ref in lse_refs]
        top = functools.reduce(jnp.maximum, lses)
        ws = [jnp.exp(x - top) for x in lses]
        total = functools.reduce(lambda a, b: a + b, ws)
        out = functools.reduce(lambda a, b: a + b, [(w / total) * ref[rows, :] for w, ref in zip(ws, og_refs)])
        o_ref[rows, :] = out.astype(o_ref.dtype)


def dilated_attention(p, q_col, k_col, v_col, gq, gk, cos, sin, batch, seq):
    head = lambda c0: pl.BlockSpec((seq, HEAD_DIM), lambda b, h: (b, c0 + h))
    table = pl.BlockSpec((seq, HEAD_DIM), lambda b, h: (b, 0))
    row_spec = pl.BlockSpec((1, HEAD_DIM), lambda b, h: (0, 0))
    groups = range(DIL_GROUPS)
    return pl.pallas_call(
        functools.partial(_dil_kernel, seq=seq),
        grid=(batch, DIL_HEADS),
        in_specs=([head(q_col + gi * DIL_HEADS) for gi in groups]
                  + [head(k_col + gi * DIL_HEADS) for gi in groups]
                  + [head(v_col + gi * DIL_HEADS) for gi in groups]
                  + [table, table, row_spec, row_spec]),
        out_specs=pl.BlockSpec((seq, HEAD_DIM), lambda b, h: (b, h)),
        out_shape=jax.ShapeDtypeStruct((batch * seq, DIL_HEADS * HEAD_DIM), BF16),
        scratch_shapes=[pltpu.VMEM((seq, HEAD_DIM), F32)]
                       + [pltpu.VMEM((seq, HEAD_DIM), BF16)] * 3
                       + [pltpu.VMEM((seq, HEAD_DIM), F32)] * (2 * DIL_GROUPS),
        compiler_params=_params(2),
        name="dilated",
    )(*([p] * 9), cos, sin, _row1(gq), _row1(gk))


def _xattn_kernel(q_ref, kv_ref, gq_ref, gk_ref, o_ref):
    gq, gk = gq_ref[...], gk_ref[...]
    width = XATTN_HEADS * HEAD_DIM
    for h in range(XATTN_HEADS):
        hs = slice(h * HEAD_DIM, (h + 1) * HEAD_DIM)
        q = (_rms(q_ref[:, hs], gq) * SCALE).astype(BF16)
        k = _rms(kv_ref[:, hs], gk).astype(BF16)
        v = kv_ref[:, width + h * HEAD_DIM:width + (h + 1) * HEAD_DIM].astype(BF16)
        s = _nt(q, k)
        e = jnp.exp(s - jnp.max(s, axis=-1, keepdims=True))
        p = e / jnp.sum(e, axis=-1, keepdims=True)
        o_ref[:, hs] = jnp.dot(p.astype(BF16), v, preferred_element_type=F32).astype(o_ref.dtype)


def memory_cross_attention(q, kv, gq, gk, batch, seq, mem_len, tq=512):
    tq = min(tq, seq)
    nq = seq // tq
    width = XATTN_HEADS * HEAD_DIM
    return pl.pallas_call(
        _xattn_kernel,
        grid=(batch, nq),
        in_specs=[pl.BlockSpec((tq, width), lambda b, i: (b * nq + i, 0)),
                  pl.BlockSpec((mem_len, 2 * width), lambda b, i: (b, 0)),
                  pl.BlockSpec((1, HEAD_DIM), lambda b, i: (0, 0)),
                  pl.BlockSpec((1, HEAD_DIM), lambda b, i: (0, 0))],
        out_specs=pl.BlockSpec((tq, width), lambda b, i: (b * nq + i, 0)),
        out_shape=jax.ShapeDtypeStruct((batch * seq, width), BF16),
        compiler_params=_params(2),
        name="xattn",
    )(q, kv, _row1(gq), _row1(gk))


def _even_mixer(x2, h, cos, sin, batch, seq, e, w_in, w_out, moba_gq, moba_gk, nsa_gq, nsa_gk_cmp,
                nsa_gk_slc, nsa_gk_win, pe_k, pe_v, phi_k, phi_v):
    hd = HEAD_DIM
    main = (3 * MOBA_HEADS + NSA_HEADS + 6 * NSA_GROUPS) * hd
    pad = -w_in.shape[-1] % hd
    p = matmul(h, jnp.pad(w_in[e].astype(BF16), ((0, 0), (0, pad))), tm=BIG_ROW_TILE)

    o_a = moba_attention(p, 0, 16, 32, moba_gq, moba_gk, cos, sin, batch, seq)
    kcmp, vcmp = nsa_compress(p, 64, 68, pe_k, pe_v, phi_k, phi_v, nsa_gk_cmp, cos, sin, batch, seq)
    o_b = nsa_attention_batched(p, 48, 72, 76, 80, 84, main // hd, kcmp, vcmp, nsa_gq, nsa_gk_slc,
                                nsa_gk_win, cos, sin, batch, seq)

    return matmul([o_a, o_b], w_out, layer=e, residual=x2, tm=BIG_ROW_TILE)


def _odd_mixer(x2, h, cos, sin, batch, seq, o, w_in, w_out, dil_gq, dil_gk):
    hd = HEAD_DIM
    p = matmul(h, w_in, layer=o, tm=BIG_ROW_TILE)
    o_c = sb_attention(p, 0, SB_HEADS, 2 * SB_HEADS, batch, seq)
    nd = DIL_GROUPS * DIL_HEADS
    o_d = dilated_attention(p, 3 * SB_HEADS, 3 * SB_HEADS + nd, 3 * SB_HEADS + 2 * nd, dil_gq, dil_gk,
                            cos, sin, batch, seq)
    return matmul([o_c, o_d], w_out, layer=o, residual=x2, tm=BIG_ROW_TILE)


def kernel(x, mem, positions, mix_norm, even_w_in, even_w_out, moba_gq, moba_gk, nsa_gq, nsa_gk_cmp, nsa_gk_slc, nsa_gk_win, nsa_pe_k, nsa_pe_v, nsa_phi_k, nsa_phi_v, odd_w_in, odd_w_out, dil_gq, dil_gk, xattn_norm, mem_norm, xattn_wq, xattn_wkv, xattn_wo, xattn_gq, xattn_gk, ffn_norm, ffn_wg, ffn_wu, ffn_wd):
    batch, seq, d = x.shape
    mem_len = mem.shape[1]
    depth = mix_norm.shape[0]
    x2 = x.reshape(batch * seq, d)
    mem2 = mem.reshape(batch * mem_len, d)
    cos, sin = rope_tables(positions)

    for layer in range(depth):
        h = rmsnorm_rows(x2, mix_norm[layer])
        if layer % 2 == 0:
            e = layer // 2
            x2 = _even_mixer(x2, h, cos, sin, batch, seq, e, even_w_in, even_w_out, moba_gq[e],
                             moba_gk[e], nsa_gq[e], nsa_gk_cmp[e], nsa_gk_slc[e], nsa_gk_win[e],
                             nsa_pe_k[e], nsa_pe_v[e], nsa_phi_k[e], nsa_phi_v[e])
        else:
            o = layer // 2
            x2 = _odd_mixer(x2, h, cos, sin, batch, seq, o, odd_w_in, odd_w_out, dil_gq[o], dil_gk[o])

        q = norm_matmul(x2, xattn_norm[layer], xattn_wq, layer)
        kv = norm_matmul(mem2, mem_norm[layer], xattn_wkv, layer)
        o_x = memory_cross_attention(q, kv, xattn_gq[layer], xattn_gk[layer], batch, seq, mem_len)
        x2, h = matmul_residual_norm(o_x, xattn_wo, layer, x2, ffn_norm[layer])
        hidden = swiglu_gate_up(h, ffn_wg, ffn_wu, layer)
        x2 = matmul(hidden, ffn_wd, layer=layer, residual=x2, tm=BIG_ROW_TILE)

    return x2.reshape(batch, seq, d)
```

```python
import functools

import jax
import jax.numpy as jnp
from jax import lax
from jax.experimental import pallas as pl
from jax.experimental.pallas import tpu as pltpu

F32 = jnp.float32
BF16 = jnp.bfloat16
HIGHEST = lax.Precision.HIGHEST

HEAD_DIM = 128
HALF = HEAD_DIM // 2
ROPE_THETA = 10000.0
NORM_EPS = 1e-6
TINY = 1e-30
SCALE = HEAD_DIM ** -0.5

MOBA_HEADS = 16
MOBA_BLOCK = 256
MOBA_TOPK = 3
MOBA_BLOCK_BATCH = 4
NSA_HEADS = 16
NSA_GROUPS = 4
NSA_HG = NSA_HEADS // NSA_GROUPS
NSA_CMP_LEN = 32
NSA_CMP_STRIDE = 16
NSA_SEL_BLOCK = 64
NSA_SEL_SHIFT = NSA_SEL_BLOCK.bit_length() - 1
NSA_SEL_TOPN = 16
NSA_WINDOW = 512
DIL_CONFIGS = ((128, 1), (512, 4), (2048, 16))
DIL_GROUPS = len(DIL_CONFIGS)
DIL_HEADS = 8
DIL_SPAN = 128
DIL_TILE_BATCH = 8
SB_HEADS = 24
XATTN_HEADS = 4

VMEM_LIMIT_BYTES = 56 * 1024 * 1024
NT_DIMS = (((1,), (1,)), ((), ()))
PREP_ROWS = 256
MATMUL_ROW_TILE = 1024
BIG_ROW_TILE = 2048


def _params(n_grid):
    return pltpu.CompilerParams(dimension_semantics=("arbitrary",) * n_grid,
                                vmem_limit_bytes=VMEM_LIMIT_BYTES)


def _nt(a, b, precision=None):
    return lax.dot_general(a, b, NT_DIMS, precision=precision, preferred_element_type=F32)


def _rms(x, g):
    return x * lax.rsqrt(jnp.mean(x * x, axis=-1, keepdims=True) + NORM_EPS) * g


def _rope(x, cos, sin_signed):
    return x * cos + pltpu.roll(x, HALF, 1) * sin_signed


def _softmax_chunks_many(scores, values):
    ms = [jnp.max(functools.reduce(jnp.maximum, s), axis=-1, keepdims=True) for s in scores]
    probs = [[jnp.exp(c - m) for c in s] for s, m in zip(scores, ms)]
    ls = [jnp.sum(functools.reduce(lambda a, b: a + b, p), axis=-1, keepdims=True) for p in probs]
    accs = []
    for p, v in zip(probs, values):
        acc = None
        for pc, vc in zip(p, v):
            part = jnp.dot(pc.astype(BF16), vc, preferred_element_type=F32)
            acc = part if acc is None else acc + part
        accs.append(acc)
    return list(zip(accs, ms, ls))


def _top_rank_many(vals_list):
    idx = lax.broadcasted_iota(jnp.int32, vals_list[0].shape, 0)
    ranks = [jnp.zeros(v.shape, jnp.int32) for v in vals_list]
    for m in range(vals_list[0].shape[0]):
        for n, vals in enumerate(vals_list):
            c = vals[m:m + 1, :]
            ranks[n] = ranks[n] + ((c > vals) | ((c == vals) & (idx > m))).astype(jnp.int32)
    return ranks


def _top_rank(vals):
    return _top_rank_many([vals])[0]


def _to_columns(x_t):
    rows = x_t.shape[1]
    eye = (lax.broadcasted_iota(jnp.int32, (rows, rows), 0)
           == lax.broadcasted_iota(jnp.int32, (rows, rows), 1)).astype(BF16)
    return _nt(eye, x_t.astype(BF16))


def _rmsnorm_kernel(x_ref, g_ref, o_ref):
    o_ref[...] = _rms(x_ref[...], g_ref[...]).astype(o_ref.dtype)


def rmsnorm_rows(x, g, tm=256):
    m, d = x.shape
    tm = min(tm, m)
    return pl.pallas_call(
        _rmsnorm_kernel,
        grid=(m // tm,),
        in_specs=[pl.BlockSpec((tm, d), lambda i: (i, 0)), pl.BlockSpec((1, d), lambda i: (0, 0))],
        out_specs=pl.BlockSpec((tm, d), lambda i: (i, 0)),
        out_shape=jax.ShapeDtypeStruct((m, d), BF16),
        compiler_params=_params(1),
        name="rmsnorm",
    )(x, g.reshape(1, d))


def _matmul_kernel(*refs, n_lhs, has_res):
    a_refs, w_refs = refs[:n_lhs], refs[n_lhs:2 * n_lhs]
    out = None
    for a_ref, w_ref in zip(a_refs, w_refs):
        part = jnp.dot(a_ref[...], w_ref[...].astype(BF16), preferred_element_type=F32)
        out = part if out is None else out + part
    if has_res:
        out = out + refs[2 * n_lhs][...]
    refs[-1][...] = out.astype(refs[-1].dtype)


def _contraction_tile(kdim, limit=6144):
    if kdim <= limit:
        return kdim
    return max(t for t in range(HEAD_DIM, limit + 1, HEAD_DIM) if kdim % t == 0)


def _matmul_call(pieces, w, layer, n, residual, out_dtype, tm):
    m = pieces[0][0].shape[0]
    tn = 512 if sum(k for _, _, k, _ in pieces) <= 4096 else 256
    tm, tn = min(tm, m), min(tn, n)
    assert m % tm == 0
    lhs_mode = dict(pipeline_mode=pl.Buffered(1)) if tm > MATMUL_ROW_TILE else {}
    a_specs, w_specs = [], []
    for _, col0, k, row0 in pieces:
        assert col0 % k == 0 and row0 % k == 0
        a_specs.append(pl.BlockSpec((tm, k), lambda i, j, c=col0 // k: (i, c), **lhs_mode))
        if layer is None:
            w_specs.append(pl.BlockSpec((k, tn), lambda i, j, r=row0 // k: (r, j)))
        else:
            w_specs.append(pl.BlockSpec((None, k, tn), lambda i, j, r=row0 // k: (layer, r, j)))
    args = [a for a, _, _, _ in pieces] + [w] * len(pieces)
    in_specs = a_specs + w_specs
    if residual is not None:
        in_specs.append(pl.BlockSpec((tm, tn), lambda i, j: (i, j)))
        args.append(residual)
    return pl.pallas_call(
        functools.partial(_matmul_kernel, n_lhs=len(pieces), has_res=residual is not None),
        grid=(m // tm, pl.cdiv(n, tn)),
        in_specs=in_specs,
        out_specs=pl.BlockSpec((tm, tn), lambda i, j: (i, j)),
        out_shape=jax.ShapeDtypeStruct((m, n), out_dtype),
        compiler_params=_params(2),
        name="matmul",
    )(*args)


def matmul(lhs, w, layer=None, residual=None, out_dtype=F32, tm=MATMUL_ROW_TILE):
    arrays = lhs if isinstance(lhs, (list, tuple)) else [lhs]
    n = w.shape[-1]
    if len(arrays) > 1:
        pieces, row0 = [], 0
        for a in arrays:
            pieces.append((a, 0, a.shape[1], row0))
            row0 += a.shape[1]
        return _matmul_call(pieces, w, layer, n, residual, out_dtype, tm)
    a = arrays[0]
    kdim = a.shape[1]
    tk = _contraction_tile(kdim)
    out = residual
    for c in range(kdim // tk):
        last = c == kdim // tk - 1
        out = _matmul_call([(a, c * tk, tk, c * tk)], w, layer, n, out, out_dtype if last else F32, tm)
    return out


def _matmul_norm_kernel(a_ref, w_ref, r_ref, g_ref, x_ref, h_ref):
    x = jnp.dot(a_ref[...], w_ref[...].astype(BF16), preferred_element_type=F32) + r_ref[...]
    x_ref[...] = x
    h_ref[...] = _rms(x, g_ref[...]).astype(h_ref.dtype)


def matmul_residual_norm(a, w, layer, residual, gain, tm=256):
    m, kdim = a.shape
    n = w.shape[-1]
    tm = min(tm, m)
    rows = lambda width: pl.BlockSpec((tm, width), lambda i: (i, 0))
    return pl.pallas_call(
        _matmul_norm_kernel,
        grid=(m // tm,),
        in_specs=[rows(kdim), pl.BlockSpec((None, kdim, n), lambda i: (layer, 0, 0)), rows(n),
                  pl.BlockSpec((1, n), lambda i: (0, 0))],
        out_specs=[rows(n), rows(n)],
        out_shape=[jax.ShapeDtypeStruct((m, n), F32), jax.ShapeDtypeStruct((m, n), BF16)],
        compiler_params=_params(1),
        name="matmul_norm",
    )(a, w, residual, gain.reshape(1, n))


def _norm_matmul_kernel(x_ref, g_ref, w_ref, o_ref):
    h = _rms(x_ref[...], g_ref[...]).astype(BF16)
    o_ref[...] = jnp.dot(h, w_ref[...].astype(BF16), preferred_element_type=F32)


def norm_matmul(x, gain, w, layer, tm=512):
    m, d = x.shape
    n = w.shape[-1]
    tm = min(tm, m)
    return pl.pallas_call(
        _norm_matmul_kernel,
        grid=(m // tm,),
        in_specs=[pl.BlockSpec((tm, d), lambda i: (i, 0)), pl.BlockSpec((1, d), lambda i: (0, 0)),
                  pl.BlockSpec((None, d, n), lambda i: (layer, 0, 0), pipeline_mode=pl.Buffered(1))],
        out_specs=pl.BlockSpec((tm, n), lambda i: (i, 0)),
        out_shape=jax.ShapeDtypeStruct((m, n), F32),
        compiler_params=_params(1),
        name="norm_matmul",
    )(x, gain.reshape(1, d), w)


def _gate_up_kernel(a_ref, wg_ref, wu_ref, o_ref):
    a = a_ref[...]
    g = jnp.dot(a, wg_ref[...].astype(BF16), preferred_element_type=F32)
    u = jnp.dot(a, wu_ref[...].astype(BF16), preferred_element_type=F32)
    o_ref[...] = (g * jax.nn.sigmoid(g) * u).astype(o_ref.dtype)


def swiglu_gate_up(a, wg, wu, layer, tm=BIG_ROW_TILE, tn=256):
    m, kdim = a.shape
    n = wg.shape[-1]
    tm, tn = min(tm, m), min(tn, n)
    w_spec = pl.BlockSpec((None, kdim, tn), lambda i, j: (layer, 0, j))
    lhs_mode = dict(pipeline_mode=pl.Buffered(1)) if tm > MATMUL_ROW_TILE else {}
    return pl.pallas_call(
        _gate_up_kernel,
        grid=(m // tm, pl.cdiv(n, tn)),
        in_specs=[pl.BlockSpec((tm, kdim), lambda i, j: (i, 0), **lhs_mode), w_spec, w_spec],
        out_specs=pl.BlockSpec((tm, tn), lambda i, j: (i, j)),
        out_shape=jax.ShapeDtypeStruct((m, n), BF16),
        compiler_params=_params(2),
        name="swiglu_gate_up",
    )(a, wg, wu)


def _rope_table_kernel(pos_ref, invf_ref, cos_ref, sin_ref):
    ang = pos_ref[...] * invf_ref[...]
    lane = lax.broadcasted_iota(jnp.int32, ang.shape, 1)
    s = jnp.sin(ang)
    cos_ref[...] = jnp.cos(ang)
    sin_ref[...] = jnp.where(lane < HALF, -s, s)


def rope_tables(positions, tr=256):
    n = positions.size
    tr = min(tr, n)
    inv_freq = ROPE_THETA ** (-jnp.arange(HALF, dtype=F32) / HALF)
    invf = jnp.concatenate([inv_freq, inv_freq]).reshape(1, HEAD_DIM)
    pos = jnp.broadcast_to(positions.astype(F32).reshape(n, 1), (n, HEAD_DIM))
    spec = pl.BlockSpec((tr, HEAD_DIM), lambda i: (i, 0))
    return pl.pallas_call(
        _rope_table_kernel,
        grid=(n // tr,),
        in_specs=[spec, pl.BlockSpec((1, HEAD_DIM), lambda i: (0, 0))],
        out_specs=[spec, spec],
        out_shape=[jax.ShapeDtypeStruct((n, HEAD_DIM), F32)] * 2,
        compiler_params=_params(1),
        name="rope_tables",
    )(pos, invf)


def _prep_keys(k_ref, g, cos_ref, sin_ref, out_ref, seq):
    for r0 in range(0, seq, PREP_ROWS):
        rows = slice(r0, min(r0 + PREP_ROWS, seq))
        out_ref[rows, :] = _rope(_rms(k_ref[rows, :], g), cos_ref[rows, :], sin_ref[rows, :]).astype(BF16)


def _row1(x):
    return x.reshape(1, HEAD_DIM)


def _moba_kernel(q_ref, k_ref, v_ref, cos_ref, sin_ref, gq_ref, gk_ref, o_ref, kb_ref, vb_ref, *, nblk):
    blk = MOBA_BLOCK
    means = []
    for n in range(nblk):
        rows = slice(n * blk, (n + 1) * blk)
        k = _rope(_rms(k_ref[rows, :], gk_ref[...]), cos_ref[rows, :], sin_ref[rows, :])
        kb_ref[rows, :] = k.astype(BF16)
        means.append(jnp.mean(k, axis=0, keepdims=True))
    vb_ref[...] = v_ref[...].astype(BF16)
    row = lax.broadcasted_iota(jnp.int32, (blk, blk), 0)
    col = lax.broadcasted_iota(jnp.int32, (blk, blk), 1)

    qbs, sels = [], []
    for i in range(nblk):
        own = slice(i * blk, (i + 1) * blk)
        q = _rope(_rms(q_ref[own, :], gq_ref[...]), cos_ref[own, :], sin_ref[own, :])
        qbs.append((q * SCALE).astype(BF16))
        sel = None
        if i > MOBA_TOPK:
            gate = _nt(jnp.concatenate(means[:i], axis=0), q, precision=HIGHEST)
            sel = _to_columns((_top_rank(gate) < MOBA_TOPK).astype(F32))
        sels.append(sel)

    for i0 in range(0, nblk, MOBA_BLOCK_BATCH):
        batch = range(i0, min(i0 + MOBA_BLOCK_BATCH, nblk))
        scores, values = [], []
        for i in batch:
            own = slice(i * blk, (i + 1) * blk)
            s, v = [jnp.where(col <= row, _nt(qbs[i], kb_ref[own, :]), -jnp.inf)], [vb_ref[own, :]]
            for n in range(i):
                rows = slice(n * blk, (n + 1) * blk)
                sn = _nt(qbs[i], kb_ref[rows, :])
                if sels[i] is not None:
                    sn = jnp.where(sels[i][:, n:n + 1] > 0.5, sn, -jnp.inf)
                s.append(sn)
                v.append(vb_ref[rows, :])
            scores.append(s)
            values.append(v)
        for i, (acc, _, l) in zip(batch, _softmax_chunks_many(scores, values)):
            o_ref[i * blk:(i + 1) * blk, :] = (acc / l).astype(o_ref.dtype)


def moba_attention(p, q_col, k_col, v_col, gq, gk, cos, sin, batch, seq):
    nblk = seq // MOBA_BLOCK
    head = lambda c0: pl.BlockSpec((seq, HEAD_DIM), lambda b, h: (b, c0 + h))
    table = pl.BlockSpec((seq, HEAD_DIM), lambda b, h: (b, 0))
    row_spec = pl.BlockSpec((1, HEAD_DIM), lambda b, h: (0, 0))
    return pl.pallas_call(
        functools.partial(_moba_kernel, nblk=nblk),
        grid=(batch, MOBA_HEADS),
        in_specs=[head(q_col), head(k_col), head(v_col), table, table, row_spec, row_spec],
        out_specs=pl.BlockSpec((seq, HEAD_DIM), lambda b, h: (b, h)),
        out_shape=jax.ShapeDtypeStruct((batch * seq, MOBA_HEADS * HEAD_DIM), BF16),
        scratch_shapes=[pltpu.VMEM((seq, HEAD_DIM), BF16), pltpu.VMEM((seq, HEAD_DIM), BF16)],
        compiler_params=_params(2),
        name="moba",
    )(p, p, p, cos, sin, _row1(gq), _row1(gk))


def _nsa_compress_kernel(kc_ref, vc_ref, pek_ref, pev_ref, phik_ref, phiv_ref, g_ref, cos_ref, sin_ref,
                         ko_ref, vo_ref, *, nslot):
    half = NSA_CMP_LEN // 2

    def compress(x_ref, pe_ref, phi_ref):
        first = jnp.zeros((nslot, HEAD_DIM), F32)
        second = jnp.zeros((nslot, HEAD_DIM), F32)
        for l in range(half):
            xl = x_ref[pl.ds(l, nslot, stride=NSA_CMP_STRIDE), :]
            first += jnp.dot((xl + pe_ref[l:l + 1, :]).astype(BF16), phi_ref[l].astype(BF16),
                             preferred_element_type=F32)
            second += jnp.dot((xl + pe_ref[half + l:half + l + 1, :]).astype(BF16),
                              phi_ref[half + l].astype(BF16), preferred_element_type=F32)
        return pltpu.roll(first, 1, 0) + second

    ends = pl.ds(NSA_CMP_STRIDE - 1, nslot, stride=NSA_CMP_STRIDE)
    kc = compress(kc_ref, pek_ref, phik_ref)
    kc = _rope(_rms(kc, g_ref[...]), cos_ref[ends, :], sin_ref[ends, :])
    ko_ref[0, 0] = kc.astype(BF16)
    vo_ref[0, 0] = compress(vc_ref, pev_ref, phiv_ref).astype(BF16)


def nsa_compress(p, kc_col, vc_col, pe_k, pe_v, phi_k, phi_v, g_kc, cos, sin, batch, seq):
    nslot = seq // NSA_CMP_STRIDE
    full = lambda shape: pl.BlockSpec(shape, lambda b, g: (0,) * len(shape))
    out_spec = pl.BlockSpec((1, 1, nslot, HEAD_DIM), lambda b, g: (b, g, 0, 0))
    out_shape = jax.ShapeDtypeStruct((batch, NSA_GROUPS, nslot, HEAD_DIM), BF16)
    return pl.pallas_call(
        functools.partial(_nsa_compress_kernel, nslot=nslot),
        grid=(batch, NSA_GROUPS),
        in_specs=[pl.BlockSpec((seq, HEAD_DIM), lambda b, g: (b, kc_col + g)),
                  pl.BlockSpec((seq, HEAD_DIM), lambda b, g: (b, vc_col + g)),
                  full((NSA_CMP_LEN, HEAD_DIM)), full((NSA_CMP_LEN, HEAD_DIM)),
                  full((NSA_CMP_LEN, HEAD_DIM, HEAD_DIM)), full((NSA_CMP_LEN, HEAD_DIM, HEAD_DIM)),
                  full((1, HEAD_DIM)),
                  pl.BlockSpec((seq, HEAD_DIM), lambda b, g: (b, 0)),
                  pl.BlockSpec((seq, HEAD_DIM), lambda b, g: (b, 0))],
        out_specs=[out_spec, out_spec],
        out_shape=[out_shape, out_shape],
        compiler_params=_params(2),
        name="nsa_compress",
    )(p, p, pe_k, pe_v, phi_k, phi_v, _row1(g_kc), cos, sin)


NSA_QUERY_TILE = 256


NSA_GROUP_BATCH = 2


def _nsa_batch_kernel(q_ref, gl_ref, kc_ref, vc_ref, ks_ref, vs_ref, kw_ref, vw_ref, cos_ref, sin_ref,
                      gq_ref, gks_ref, gkw_ref, o_ref, ksb_ref, vsb_ref, kwb_ref, vwb_ref,
                      s_ref, top_ref, sum_ref, acc_ref, *, tq, seq):
    hg = NSA_HG
    groups = range(NSA_GROUP_BATCH)
    nslot = seq // NSA_CMP_STRIDE
    nsel = seq // NSA_SEL_BLOCK
    topn = min(NSA_SEL_TOPN, nsel)
    qi = pl.program_id(2)
    q0 = qi * tq
    lanes = [slice(g * HEAD_DIM, (g + 1) * HEAD_DIM) for g in groups]

    @pl.when(qi == 0)
    def _():
        for g in groups:
            _prep_keys(ks_ref.at[:, lanes[g]], gks_ref[...], cos_ref, sin_ref, ksb_ref.at[:, lanes[g]], seq)
            _prep_keys(kw_ref.at[:, lanes[g]], gkw_ref[...], cos_ref, sin_ref, kwb_ref.at[:, lanes[g]], seq)
        vsb_ref[...] = vs_ref[...].astype(BF16)
        vwb_ref[...] = vw_ref[...].astype(BF16)

    def tile_heads(x):
        return jnp.concatenate([x] * hg, axis=0)

    row = lax.broadcasted_iota(jnp.int32, (tq, tq), 0)
    col = lax.broadcasted_iota(jnp.int32, (tq, tq), 1)
    per_sel = NSA_SEL_BLOCK // NSA_CMP_STRIDE
    span = NSA_CMP_LEN // NSA_CMP_STRIDE
    nback = NSA_WINDOW // tq

    own = pl.ds(pl.multiple_of(q0, tq), tq)
    cos, sin, gq = cos_ref[own, :], sin_ref[own, :], gq_ref[...]
    qst = []
    for g in groups:
        qs = []
        for h in range(hg):
            c0 = (g * hg + h) * HEAD_DIM
            qs.append((_rope(_rms(q_ref[:, c0:c0 + HEAD_DIM], gq), cos, sin) * SCALE).astype(BF16))
        qst.append(jnp.concatenate(qs, axis=0))

    scores_w, values_w = [[] for _ in groups], [[] for _ in groups]
    for d in range(nback + 1):
        keys = pl.ds(pl.multiple_of(jnp.maximum(qi - d, 0) * tq, tq), tq)
        exists = jnp.where(qi >= d, 0.0, -jnp.inf)
        if d == 0:
            bias = tile_heads(jnp.where(col <= row, 0.0, -jnp.inf))
        elif d == nback:
            bias = tile_heads(jnp.where(col > row, exists, -jnp.inf))
        else:
            bias = exists
        for g in groups:
            scores_w[g].append(_nt(qst[g], kwb_ref[keys, lanes[g]]) + bias)
            values_w[g].append(vwb_ref[keys, lanes[g]])

    slot = lax.broadcasted_iota(jnp.int32, (tq, nslot), 1)
    tpos = q0 + lax.broadcasted_iota(jnp.int32, (tq, nslot), 0)
    valid = (slot >= 1) & (slot * NSA_CMP_STRIDE + (NSA_CMP_STRIDE - 1) <= tpos)
    bias_c = tile_heads(jnp.where(valid, 0.0, -jnp.inf))
    s_c = [_nt(qst[g], kc_ref[0, g]) + bias_c for g in groups]
    m_c = [jnp.max(s, axis=-1, keepdims=True) for s in s_c]
    m_c = [jnp.where(m == -jnp.inf, 0.0, m) for m in m_c]
    e_c = [jnp.exp(s - m) for s, m in zip(s_c, m_c)]
    p_c = [e / jnp.maximum(jnp.sum(e, axis=-1, keepdims=True), TINY) for e in e_c]
    o_c = [jnp.dot(p_c[g].astype(BF16), vc_ref[0, g], preferred_element_type=F32) for g in groups]

    b_idx = lax.broadcasted_iota(jnp.int32, (nsel, nslot), 0)
    j_idx = lax.broadcasted_iota(jnp.int32, (nsel, nslot), 1)
    overlap = ((j_idx >= 1) & (j_idx - 1 > per_sel * b_idx - span)
               & (j_idx - 1 < per_sel * (b_idx + 1))).astype(F32)
    blk = lax.broadcasted_iota(jnp.int32, (nsel, tq), 0)
    cur = (q0 + lax.broadcasted_iota(jnp.int32, (nsel, tq), 1)) >> NSA_SEL_SHIFT
    forced = (blk == cur) | (blk == 0)
    imps = []
    for g in groups:
        p_sum = p_c[g][0:tq]
        for h in range(1, hg):
            p_sum = p_sum + p_c[g][h * tq:(h + 1) * tq]
        imp = _nt(overlap, p_sum, precision=HIGHEST)
        imps.append(jnp.where(forced, jnp.inf, jnp.where(blk <= cur, imp, -jnp.inf)))
    o_w = [acc / l for acc, _, l in _softmax_chunks_many(scores_w, values_w)]
    sels = [_to_columns((rank < topn).astype(F32)).astype(BF16) for rank in _top_rank_many(imps)]

    e_row = lax.broadcasted_iota(jnp.int32, (nsel, tq), 0)
    e_col = lax.broadcasted_iota(jnp.int32, (nsel, tq), 1)
    top_ref[...] = jnp.full(top_ref.shape, -jnp.inf, F32)

    def score_chunk(c, carry):
        keys = pl.ds(pl.multiple_of(c * tq, tq), tq)
        expand = (((c * tq + e_col) >> NSA_SEL_SHIFT) == e_row).astype(BF16)
        causal = c * tq + col <= q0 + row
        scores = [_nt(qst[g], ksb_ref[keys, lanes[g]]) for g in groups]
        for g in groups:
            keep = (jnp.dot(sels[g], expand, preferred_element_type=F32) > 0.5) & causal
            s = scores[g] + tile_heads(jnp.where(keep, 0.0, -jnp.inf))
            s_ref[g, c] = s
            top_ref[g] = jnp.maximum(top_ref[g], s)
        return carry

    lax.fori_loop(0, qi + 1, score_chunk, 0)
    m_s = [jnp.max(top_ref[g], axis=-1, keepdims=True) for g in groups]
    sum_ref[...] = jnp.zeros(sum_ref.shape, F32)
    acc_ref[...] = jnp.zeros(acc_ref.shape, F32)

    def weigh_chunk(c, carry):
        keys = pl.ds(pl.multiple_of(c * tq, tq), tq)
        probs = [jnp.exp(s_ref[g, c] - m_s[g]) for g in groups]
        for g in groups:
            sum_ref[g] += probs[g]
            acc_ref[g] += jnp.dot(probs[g].astype(BF16), vsb_ref[keys, lanes[g]], preferred_element_type=F32)
        return carry

    lax.fori_loop(0, qi + 1, weigh_chunk, 0)

    gl = gl_ref[...]
    for g in groups:
        o_s = acc_ref[g] / jnp.sum(sum_ref[g], axis=-1, keepdims=True)
        group = pl.program_id(1) * NSA_GROUP_BATCH + g
        gates = jax.nn.sigmoid(pltpu.roll(gl, (HEAD_DIM - 3 * hg * group) % HEAD_DIM, 1))
        for h in range(hg):
            hs = slice(h * tq, (h + 1) * tq)
            out = (gates[:, 3 * h:3 * h + 1] * o_c[g][hs] + gates[:, 3 * h + 1:3 * h + 2] * o_s[hs]
                   + gates[:, 3 * h + 2:3 * h + 3] * o_w[g][hs])
            c0 = (g * hg + h) * HEAD_DIM
            o_ref[:, c0:c0 + HEAD_DIM] = out.astype(o_ref.dtype)


def nsa_attention_batched(p, q_col, ks_col, vs_col, kw_col, vw_col, gl_col, kcmp, vcmp, gq, gks, gkw,
                          cos, sin, batch, seq):
    tq = min(NSA_QUERY_TILE, seq)
    nq = seq // tq
    gb = NSA_GROUP_BATCH
    gw = gb * NSA_HG * HEAD_DIM
    rows = NSA_HG * tq
    nslot = seq // NSA_CMP_STRIDE
    assert NSA_GROUPS % gb == 0 and all(c % gb == 0 for c in (ks_col, vs_col, kw_col, vw_col))
    heads = lambda c0: pl.BlockSpec((seq, gb * HEAD_DIM), lambda b, g, i: (b, c0 // gb + g),
                                    pipeline_mode=pl.Buffered(1))
    table = pl.BlockSpec((seq, HEAD_DIM), lambda b, g, i: (b, 0))
    row_spec = pl.BlockSpec((1, HEAD_DIM), lambda b, g, i: (0, 0))
    cmp_spec = pl.BlockSpec((1, gb, nslot, HEAD_DIM), lambda b, g, i: (b, g, 0, 0))
    return pl.pallas_call(
        functools.partial(_nsa_batch_kernel, tq=tq, seq=seq),
        grid=(batch, NSA_GROUPS // gb, nq),
        in_specs=[pl.BlockSpec((tq, gw), lambda b, g, i: (b * nq + i, q_col // (gb * NSA_HG) + g)),
                  pl.BlockSpec((tq, HEAD_DIM), lambda b, g, i: (b * nq + i, gl_col)),
                  cmp_spec, cmp_spec,
                  heads(ks_col), heads(vs_col), heads(kw_col), heads(vw_col), table, table,
                  row_spec, row_spec, row_spec],
        out_specs=pl.BlockSpec((tq, gw), lambda b, g, i: (b * nq + i, g)),
        out_shape=jax.ShapeDtypeStruct((batch * seq, NSA_HEADS * HEAD_DIM), BF16),
        scratch_shapes=[pltpu.VMEM((seq, gb * HEAD_DIM), BF16)] * 4
                       + [pltpu.VMEM((gb, nq, rows, tq), F32),
                          pltpu.VMEM((gb, rows, tq), F32), pltpu.VMEM((gb, rows, tq), F32),
                          pltpu.VMEM((gb, rows, HEAD_DIM), F32)],
        compiler_params=_params(3),
        name="nsa",
    )(p, p, kcmp, vcmp, p, p, p, p, cos, sin, _row1(gq), _row1(gks), _row1(gkw))


def _sb_kernel(q_ref, k_ref, v_ref, o_ref, kb_ref, vb_ref, *, tq, tk, seq):
    per_q = tq // tk
    kb_ref[...] = k_ref[...].astype(BF16)
    vb_ref[...] = v_ref[...].astype(BF16)

    later = (lax.broadcasted_iota(jnp.int32, (tk, tk), 0)
             > lax.broadcasted_iota(jnp.int32, (tk, tk), 1)).astype(BF16)
    row = lax.broadcasted_iota(jnp.int32, (tq, tk), 0)
    col = lax.broadcasted_iota(jnp.int32, (tq, tk), 1)

    def tiles(qb, first, key0s, run, acc):
        keys = [slice((first - j) * tk, (first - j + 1) * tk) for j in range(len(key0s))]
        zs = [_nt(qb, kb_ref[kk, :]) for kk in keys]
        log_betas, log_rests, stricts = [], [], []
        for z, key0 in zip(zs, key0s):
            log_beta = jnp.minimum(z, 0.0) - jnp.log(1.0 + jnp.exp(-jnp.abs(z)))
            log_rest = log_beta - z
            strict = None
            if key0 is not None:
                strict = key0 + col < row
                log_rest = jnp.where(strict, log_rest, 0.0)
            log_betas.append(log_beta)
            log_rests.append(log_rest)
            stricts.append(strict)
        insides = []
        for log_rest in log_rests:
            hi = log_rest.astype(BF16)
            lo = (log_rest - hi.astype(F32)).astype(BF16)
            both = jnp.dot(jnp.concatenate([hi, lo], axis=0), later, preferred_element_type=F32)
            insides.append(both[:tq] + both[tq:])
        for kk, log_beta, log_rest, strict, inside in zip(keys, log_betas, log_rests, stricts, insides):
            a = jnp.exp(log_beta + inside + jnp.concatenate([run] * (tk // HEAD_DIM), axis=1))
            if strict is not None:
                a = jnp.where(strict, a, 0.0)
            acc = acc + jnp.dot(a.astype(BF16), vb_ref[kk, :], preferred_element_type=F32)
            run = run + jnp.sum(log_rest, axis=-1, keepdims=True)
        return run, acc

    for qt in range(seq // tq):
        own = slice(qt * tq, (qt + 1) * tq)
        qb = (q_ref[own, :] * SCALE).astype(BF16)
        run = jnp.zeros((tq, HEAD_DIM), F32)
        acc = jnp.zeros((tq, HEAD_DIM), F32)
        run, acc = tiles(qb, (qt + 1) * per_q - 1, [(per_q - 1 - j) * tk for j in range(per_q)], run, acc)
        for i in range(qt):
            run, acc = tiles(qb, (qt - i) * per_q - 1, [None] * per_q, run, acc)
        o_ref[own, :] = acc.astype(o_ref.dtype)


def sb_attention(p, q_col, k_col, v_col, batch, seq, tq=512, tk=256):
    tq = min(tq, seq)
    tk = min(tk, tq)
    head = lambda c0: pl.BlockSpec((seq, HEAD_DIM), lambda b, h: (b, c0 + h))
    return pl.pallas_call(
        functools.partial(_sb_kernel, tq=tq, tk=tk, seq=seq),
        grid=(batch, SB_HEADS),
        in_specs=[head(q_col), head(k_col), head(v_col)],
        out_specs=pl.BlockSpec((seq, HEAD_DIM), lambda b, h: (b, h)),
        out_shape=jax.ShapeDtypeStruct((batch * seq, SB_HEADS * HEAD_DIM), BF16),
        scratch_shapes=[pltpu.VMEM((seq, HEAD_DIM), BF16), pltpu.VMEM((seq, HEAD_DIM), BF16)],
        compiler_params=_params(2),
        name="stick_breaking",
    )(p, p, p)


def _dil_kernel(q0_ref, q1_ref, q2_ref, k0_ref, k1_ref, k2_ref, v0_ref, v1_ref, v2_ref,
                cos_ref, sin_ref, gq_ref, gk_ref, o_ref,
                tmp_ref, qd_ref, kd_ref, vd_ref, og0_ref, og1_ref, og2_ref, lse0_ref, lse1_ref, lse2_ref,
                *, seq):
    q_refs, k_refs, v_refs = (q0_ref, q1_ref, q2_ref), (k0_ref, k1_ref, k2_ref), (v0_ref, v1_ref, v2_ref)
    og_refs, lse_refs = (og0_ref, og1_ref, og2_ref), (lse0_ref, lse1_ref, lse2_ref)
    ta = DIL_SPAN
    row = lax.broadcasted_iota(jnp.int32, (ta, ta), 0)
    col = lax.broadcasted_iota(jnp.int32, (ta, ta), 1)

    for gi, (window, dil) in enumerate(DIL_CONFIGS):
        assert window == dil * DIL_SPAN and seq % (dil * ta) == 0
        n_a = seq // dil
        tiles_per_class = n_a // ta

        def class_major(dst_ref, rows, val, dil=dil, n_a=n_a):
            if dil == 1:
                dst_ref[rows, :] = val.astype(BF16)
                return
            tmp_ref[rows, :] = val
            per = (rows.stop - rows.start) // dil
            a0 = rows.start // dil
            for rho in range(dil):
                src = pl.ds(rows.start + rho, per, stride=dil)
                dst_ref[rho * n_a + a0:rho * n_a + a0 + per, :] = tmp_ref[src, :].astype(BF16)

        for r0 in range(0, seq, PREP_ROWS):
            rows = slice(r0, r0 + PREP_ROWS)
            cos, sin = cos_ref[rows, :], sin_ref[rows, :]
            class_major(qd_ref, rows, _rope(_rms(q_refs[gi][rows, :], gq_ref[...]), cos, sin) * SCALE)
            class_major(kd_ref, rows, _rope(_rms(k_refs[gi][rows, :], gk_ref[...]), cos, sin))
            class_major(vd_ref, rows, v_refs[gi][rows, :])

        for j0 in range(0, seq // ta, DIL_TILE_BATCH):
            batch = range(j0, min(j0 + DIL_TILE_BATCH, seq // ta))
            scores, values = [], []
            for j in batch:
                rows = slice(j * ta, (j + 1) * ta)
                qj = qd_ref[rows, :]
                s, v = [jnp.where(col <= row, _nt(qj, kd_ref[rows, :]), -jnp.inf)], [vd_ref[rows, :]]
                if j % tiles_per_class > 0:
                    prev = slice((j - 1) * ta, j * ta)
                    s.append(jnp.where(col >= row, _nt(qj, kd_ref[prev, :]), -jnp.inf))
                    v.append(vd_ref[prev, :])
                scores.append(s)
                values.append(v)
            for j, (acc, m, l) in zip(batch, _softmax_chunks_many(scores, values)):
                rho, at = divmod(j, tiles_per_class)
                tokens = pl.ds(dil * at * ta + rho, ta, stride=dil) if dil > 1 else slice(j * ta, (j + 1) * ta)
                og_refs[gi][tokens, :] = acc / l
                lse_refs[gi][tokens, :] = jnp.broadcast_to(m + jnp.log(l), (ta, HEAD_DIM))

    for r0 in range(0, seq, PREP_ROWS):
        rows = slice(r0, r0 + PREP_ROWS)
        lses = [ref[rows, :] for ref in lse_refs]
        top = functools.reduce(jnp.maximum, lses)
        ws = [jnp.exp(x - top) for x in lses]
        total = functools.reduce(lambda a, b: a + b, ws)
        out = functools.reduce(lambda a, b: a + b, [(w / total) * ref[rows, :] for w, ref in zip(ws, og_refs)])
        o_ref[rows, :] = out.astype(o_ref.dtype)


def dilated_attention(p, q_col, k_col, v_col, gq, gk, cos, sin, batch, seq):
    head = lambda c0: pl.BlockSpec((seq, HEAD_DIM), lambda b, h: (b, c0 + h))
    table = pl.BlockSpec((seq, HEAD_DIM), lambda b, h: (b, 0))
    row_spec = pl.BlockSpec((1, HEAD_DIM), lambda b, h: (0, 0))
    groups = range(DIL_GROUPS)
    return pl.pallas_call(
        functools.partial(_dil_kernel, seq=seq),
        grid=(batch, DIL_HEADS),
        in_specs=([head(q_col + gi * DIL_HEADS) for gi in groups]
                  + [head(k_col + gi * DIL_HEADS) for gi in groups]
                  + [head(v_col + gi * DIL_HEADS) for gi in groups]
                  + [table, table, row_spec, row_spec]),
        out_specs=pl.BlockSpec((seq, HEAD_DIM), lambda b, h: (b, h)),
        out_shape=jax.ShapeDtypeStruct((batch * seq, DIL_HEADS * HEAD_DIM), BF16),
        scratch_shapes=[pltpu.VMEM((seq, HEAD_DIM), F32)]
                       + [pltpu.VMEM((seq, HEAD_DIM), BF16)] * 3
                       + [pltpu.VMEM((seq, HEAD_DIM), F32)] * (2 * DIL_GROUPS),
        compiler_params=_params(2),
        name="dilated",
    )(*([p] * 9), cos, sin, _row1(gq), _row1(gk))


def _xattn_kernel(q_ref, kv_ref, gq_ref, gk_ref, o_ref):
    gq, gk = gq_ref[...], gk_ref[...]
    width = XATTN_HEADS * HEAD_DIM
    for h in range(XATTN_HEADS):
        hs = slice(h * HEAD_DIM, (h + 1) * HEAD_DIM)
        q = (_rms(q_ref[:, hs], gq) * SCALE).astype(BF16)
        k = _rms(kv_ref[:, hs], gk).astype(BF16)
        v = kv_ref[:, width + h * HEAD_DIM:width + (h + 1) * HEAD_DIM].astype(BF16)
        s = _nt(q, k)
        e = jnp.exp(s - jnp.max(s, axis=-1, keepdims=True))
        p = e / jnp.sum(e, axis=-1, keepdims=True)
        o_ref[:, hs] = jnp.dot(p.astype(BF16), v, preferred_element_type=F32).astype(o_ref.dtype)


def memory_cross_attention(q, kv, gq, gk, batch, seq, mem_len, tq=512):
    tq = min(tq, seq)
    nq = seq // tq
    width = XATTN_HEADS * HEAD_DIM
    return pl.pallas_call(
        _xattn_kernel,
        grid=(batch, nq),
        in_specs=[pl.BlockSpec((tq, width), lambda b, i: (b * nq + i, 0)),
                  pl.BlockSpec((mem_len, 2 * width), lambda b, i: (b, 0)),
                  pl.BlockSpec((1, HEAD_DIM), lambda b, i: (0, 0)),
                  pl.BlockSpec((1, HEAD_DIM), lambda b, i: (0, 0))],
        out_specs=pl.BlockSpec((tq, width), lambda b, i: (b * nq + i, 0)),
        out_shape=jax.ShapeDtypeStruct((batch * seq, width), BF16),
        compiler_params=_params(2),
        name="xattn",
    )(q, kv, _row1(gq), _row1(gk))


def _even_mixer(x2, h, cos, sin, batch, seq, e, w_in, w_out, moba_gq, moba_gk, nsa_gq, nsa_gk_cmp,
                nsa_gk_slc, nsa_gk_win, pe_k, pe_v, phi_k, phi_v):
    hd = HEAD_DIM
    main = (3 * MOBA_HEADS + NSA_HEADS + 6 * NSA_GROUPS) * hd
    p = matmul(h, w_in, layer=e, tm=BIG_ROW_TILE)

    o_a = moba_attention(p, 0, 16, 32, moba_gq, moba_gk, cos, sin, batch, seq)
    kcmp, vcmp = nsa_compress(p, 64, 68, pe_k, pe_v, phi_k, phi_v, nsa_gk_cmp, cos, sin, batch, seq)
    o_b = nsa_attention_batched(p, 48, 72, 76, 80, 84, main // hd, kcmp, vcmp, nsa_gq, nsa_gk_slc,
                                nsa_gk_win, cos, sin, batch, seq)

    return matmul([o_a, o_b], w_out, layer=e, residual=x2, tm=BIG_ROW_TILE)


def _odd_mixer(x2, h, cos, sin, batch, seq, o, w_in, w_out, dil_gq, dil_gk):
    hd = HEAD_DIM
    p = matmul(h, w_in, layer=o, tm=BIG_ROW_TILE)
    o_c = sb_attention(p, 0, SB_HEADS, 2 * SB_HEADS, batch, seq)
    nd = DIL_GROUPS * DIL_HEADS
    o_d = dilated_attention(p, 3 * SB_HEADS, 3 * SB_HEADS + nd, 3 * SB_HEADS + 2 * nd, dil_gq, dil_gk,
                            cos, sin, batch, seq)
    return matmul([o_c, o_d], w_out, layer=o, residual=x2, tm=BIG_ROW_TILE)


def kernel(x, mem, positions, mix_norm, even_w_in, even_w_out, moba_gq, moba_gk, nsa_gq, nsa_gk_cmp, nsa_gk_slc, nsa_gk_win, nsa_pe_k, nsa_pe_v, nsa_phi_k, nsa_phi_v, odd_w_in, odd_w_out, dil_gq, dil_gk, xattn_norm, mem_norm, xattn_wq, xattn_wkv, xattn_wo, xattn_gq, xattn_gk, ffn_norm, ffn_wg, ffn_wu, ffn_wd):
    batch, seq, d = x.shape
    mem_len = mem.shape[1]
    depth = mix_norm.shape[0]
    x2 = x.reshape(batch * seq, d)
    mem2 = mem.reshape(batch * mem_len, d)
    cos, sin = rope_tables(positions)

    for layer in range(depth):
        h = rmsnorm_rows(x2, mix_norm[layer])
        if layer % 2 == 0:
            e = layer // 2
            x2 = _even_mixer(x2, h, cos, sin, batch, seq, e, even_w_in, even_w_out, moba_gq[e],
                             moba_gk[e], nsa_gq[e], nsa_gk_cmp[e], nsa_gk_slc[e], nsa_gk_win[e],
                             nsa_pe_k[e], nsa_pe_v[e], nsa_phi_k[e], nsa_phi_v[e])
        else:
            o = layer // 2
            x2 = _odd_mixer(x2, h, cos, sin, batch, seq, o, odd_w_in, odd_w_out, dil_gq[o], dil_gk[o])

        q = norm_matmul(x2, xattn_norm[layer], xattn_wq, layer)
        kv = norm_matmul(mem2, mem_norm[layer], xattn_wkv, layer)
        o_x = memory_cross_attention(q, kv, xattn_gq[layer], xattn_gk[layer], batch, seq, mem_len)
        x2, h = matmul_residual_norm(o_x, xattn_wo, layer, x2, ffn_norm[layer])
        hidden = swiglu_gate_up(h, ffn_wg, ffn_wu, layer)
        x2 = matmul(hidden, ffn_wd, layer=layer, residual=x2, tm=BIG_ROW_TILE)

    return x2.reshape(batch, seq, d)
```
